```python
import math
import jax, jax.numpy as jnp
from jax import lax
import numpy as np

D_MODEL = 1024
BATCH = 4
SEQ = 4096
DEPTH = 2

GRID_W = 64
CTX_LEN = 256
N_MIXERS = 2
N_HEADS = 8
N_KV_HEADS = 2
HEAD_DIM = D_MODEL // N_HEADS
KV_REP = N_HEADS // N_KV_HEADS
QKV_DIM = (N_HEADS + 2 * N_KV_HEADS) * HEAD_DIM
Q_BLOCK = 128
WINDOW = 128
ROPE_THETA = 10000.0
RMS_EPS = 1e-6
NEG_INF = -1e30
N_WINDOW_LAYERS = DEPTH // N_MIXERS
PEER_HEADS = 8
PEER_TOPK = 16
N_KEYS = 128
N_EXPERTS = N_KEYS * N_KEYS
PEER_QDIM = 256
PEER_BLOCK = 128
N_MOD = 6

kernel_name = "hybrid_dit_gqa_window_peer"


def rms_norm(x, gain):
    xf = x.astype(jnp.float32)
    y = xf * lax.rsqrt(jnp.mean(xf * xf, axis=-1, keepdims=True) + RMS_EPS)
    return (y * gain.astype(jnp.float32)).astype(x.dtype)


def axial_rope_tables(n_tokens, dtype):
    rows = n_tokens // GRID_W
    row = jnp.repeat(jnp.arange(rows), GRID_W).astype(jnp.float32)
    col = jnp.tile(jnp.arange(GRID_W), rows).astype(jnp.float32)
    half = HEAD_DIM // 2
    inv = ROPE_THETA ** (-jnp.arange(0, half, 2, dtype=jnp.float32) / half)
    ang = jnp.concatenate([row[:, None] * inv, col[:, None] * inv], axis=-1)
    return jnp.cos(ang).astype(dtype), jnp.sin(ang).astype(dtype)


def apply_axial_rope(x, cos, sin):
    n = x.shape[1]
    xr = x.reshape(x.shape[:-1] + (2, 2, HEAD_DIM // 4))
    x1 = xr[..., 0, :]
    x2 = xr[..., 1, :]
    c = cos.reshape(n, 1, 2, HEAD_DIM // 4)
    s = sin.reshape(n, 1, 2, HEAD_DIM // 4)
    out = jnp.stack([x1 * c - x2 * s, x1 * s + x2 * c], axis=-2)
    return out.reshape(x.shape)


def modulate(h, shift, scale):
    return h * (1 + scale) + shift


def project_qkv(h, w_qkv, q_gain, k_gain, rope):
    b, n, _ = h.shape
    y = h @ w_qkv
    q = y[..., : N_HEADS * HEAD_DIM].reshape(b, n, N_HEADS, HEAD_DIM)
    k = y[..., N_HEADS * HEAD_DIM:(N_HEADS + N_KV_HEADS) * HEAD_DIM].reshape(b, n, N_KV_HEADS, HEAD_DIM)
    v = y[..., (N_HEADS + N_KV_HEADS) * HEAD_DIM:].reshape(b, n, N_KV_HEADS, HEAD_DIM)
    q = rms_norm(q, q_gain)
    k = rms_norm(k, k_gain)
    if rope is not None:
        cos, sin = rope
        q = apply_axial_rope(q, cos, sin)
        k = apply_axial_rope(k, cos, sin)
    q = q.reshape(b, n, N_KV_HEADS, KV_REP, HEAD_DIM)
    return q, k, v


def latent_global_attention(q, k, v, kc, vc):
    b, n = q.shape[:2]
    nblk = n // Q_BLOCK
    scale = HEAD_DIM ** -0.5
    ka = jnp.concatenate([kc, k], axis=1)
    va = jnp.concatenate([vc, v], axis=1)
    qb = jnp.moveaxis(q.reshape(b, nblk, Q_BLOCK, N_KV_HEADS, KV_REP, HEAD_DIM), 1, 0)

    def attend_block(qblk):
        s = jnp.einsum('bqgrd,bkgd->bgrqk', qblk, ka).astype(jnp.float32) * scale
        p = jax.nn.softmax(s, axis=-1)
        return jnp.einsum('bgrqk,bkgd->bqgrd', p.astype(va.dtype), va)

    o = lax.map(attend_block, qb)
    return jnp.moveaxis(o, 0, 1).reshape(b, n, N_HEADS * HEAD_DIM)


def latent_window_attention(q, k, v, kc, vc, sink):
    b, n = q.shape[:2]
    nblk = n // Q_BLOCK
    span = Q_BLOCK + 2 * WINDOW
    scale = HEAD_DIM ** -0.5
    kp = jnp.pad(k, ((0, 0), (WINDOW, WINDOW), (0, 0), (0, 0)))
    vp = jnp.pad(v, ((0, 0), (WINDOW, WINDOW), (0, 0), (0, 0)))
    idx = jnp.arange(nblk)[:, None] * Q_BLOCK + jnp.arange(span)[None, :]
    kb = kp[:, idx]
    vb = vp[:, idx]
    kpos = idx - WINDOW
    qpos = jnp.arange(nblk)[:, None] * Q_BLOCK + jnp.arange(Q_BLOCK)[None, :]
    mask = ((kpos[:, None, :] >= 0) & (kpos[:, None, :] < n)
            & (jnp.abs(qpos[:, :, None] - kpos[:, None, :]) <= WINDOW))
    qb = q.reshape(b, nblk, Q_BLOCK, N_KV_HEADS, KV_REP, HEAD_DIM)
    s_loc = jnp.einsum('bnqgrd,bnkgd->bngrqk', qb, kb).astype(jnp.float32) * scale
    s_loc = jnp.where(mask[None, :, None, None], s_loc, NEG_INF)
    s_ctx = jnp.einsum('bnqgrd,bcgd->bngrqc', qb, kc).astype(jnp.float32) * scale
    sk = jnp.broadcast_to(sink.astype(jnp.float32).reshape(N_KV_HEADS, KV_REP, 1, 1),
                          s_loc.shape[:-1] + (1,))
    p = jax.nn.softmax(jnp.concatenate([s_loc, s_ctx, sk], axis=-1), axis=-1)
    p_loc = p[..., :span]
    p_ctx = p[..., span:span + kc.shape[1]]
    o = (jnp.einsum('bngrqk,bnkgd->bnqgrd', p_loc.astype(vb.dtype), vb)
         + jnp.einsum('bngrqc,bcgd->bnqgrd', p_ctx.astype(vc.dtype), vc))
    return o.reshape(b, n, N_HEADS * HEAD_DIM)


def context_attention(qc, kc, vc, sink):
    b, n = qc.shape[:2]
    scale = HEAD_DIM ** -0.5
    s = jnp.einsum('bqgrd,bkgd->bgrqk', qc, kc).astype(jnp.float32) * scale
    if sink is not None:
        sk = jnp.broadcast_to(sink.astype(jnp.float32).reshape(N_KV_HEADS, KV_REP, 1, 1),
                              s.shape[:-1] + (1,))
        p = jax.nn.softmax(jnp.concatenate([s, sk], axis=-1), axis=-1)[..., :n]
    else:
        p = jax.nn.softmax(s, axis=-1)
    o = jnp.einsum('bgrqk,bkgd->bqgrd', p.astype(vc.dtype), vc)
    return o.reshape(b, n, N_HEADS * HEAD_DIM)


def peer_block(hb, w_q, sub_keys, u_tab, v_tab):
    n = hb.shape[0]
    q = (hb @ w_q).reshape(n, PEER_HEADS, PEER_QDIM)
    q1 = q[..., : PEER_QDIM // 2]
    q2 = q[..., PEER_QDIM // 2:]
    s1 = jnp.einsum('nhd,kd->nhk', q1, sub_keys[0]).astype(jnp.float32)
    s2 = jnp.einsum('nhd,kd->nhk', q2, sub_keys[1]).astype(jnp.float32)
    v1, i1 = lax.top_k(s1, PEER_TOPK)
    v2, i2 = lax.top_k(s2, PEER_TOPK)
    cand_s = (v1[..., :, None] + v2[..., None, :]).reshape(n, PEER_HEADS, PEER_TOPK * PEER_TOPK)
    cand_i = (i1[..., :, None] * N_KEYS + i2[..., None, :]).reshape(n, PEER_HEADS, PEER_TOPK * PEER_TOPK)
    top_s, pos = lax.top_k(cand_s, PEER_TOPK)
    expert = jnp.take_along_axis(cand_i, pos, axis=-1)
    g = jax.nn.softmax(top_s, axis=-1)
    u = u_tab[expert]
    act = jax.nn.gelu(jnp.einsum('nhkd,nd->nhk', u, hb).astype(jnp.float32))
    w = (g * act).astype(hb.dtype)
    return jnp.einsum('nhk,nhkd->nd', w, v_tab[expert])


def peer_ffn(h, w_q, sub_keys, u_tab, v_tab):
    b, n, d = h.shape
    hb = h.reshape(-1, PEER_BLOCK, d)
    out = lax.map(lambda blk: peer_block(blk, w_q, sub_keys, u_tab, v_tab), hb)
    return out.reshape(b, n, d)


def setup_inputs(seed: int = 0) -> dict:
    key = jax.random.key(seed)
    ks = jax.random.split(key, 18)
    f32 = jnp.float32
    x = jax.random.normal(ks[0], (BATCH, SEQ, D_MODEL), f32)
    c = jax.random.normal(ks[1], (BATCH, D_MODEL), f32)
    ctx = jax.random.normal(ks[2], (BATCH, CTX_LEN, D_MODEL), f32)
    c_ctx = jax.random.normal(ks[3], (D_MODEL,), f32)
    ada_w = jax.random.normal(ks[4], (DEPTH, D_MODEL, N_MOD * D_MODEL), f32) * (0.5 * D_MODEL ** -0.5)
    ada_b = jax.random.normal(ks[5], (DEPTH, N_MOD * D_MODEL), f32) * 0.1
    norm1_gain = 1.0 + 0.02 * jax.random.normal(ks[6], (DEPTH, D_MODEL), f32)
    norm2_gain = 1.0 + 0.02 * jax.random.normal(ks[7], (DEPTH, D_MODEL), f32)
    w_qkv = jax.random.normal(ks[8], (DEPTH, D_MODEL, QKV_DIM), f32) * D_MODEL ** -0.5
    q_norm_gain = 1.0 + 0.02 * jax.random.normal(ks[9], (DEPTH, HEAD_DIM), f32)
    k_norm_gain = 1.0 + 0.02 * jax.random.normal(ks[10], (DEPTH, HEAD_DIM), f32)
    w_o = jax.random.normal(ks[11], (DEPTH, N_HEADS * HEAD_DIM, D_MODEL), f32) * (N_HEADS * HEAD_DIM) ** -0.5
    attn_sinks = jax.random.normal(ks[12], (N_WINDOW_LAYERS, N_HEADS), f32) * 0.5
    peer_w_q = jax.random.normal(ks[13], (DEPTH, D_MODEL, PEER_HEADS * PEER_QDIM), f32) * D_MODEL ** -0.5
    peer_sub_keys = jax.random.normal(ks[14], (DEPTH, 2, N_KEYS, PEER_QDIM // 2), f32) * (PEER_QDIM // 2) ** -0.5
    peer_u = jax.random.normal(ks[15], (DEPTH, N_EXPERTS, D_MODEL), f32) * D_MODEL ** -0.5
    peer_v = jax.random.normal(ks[16], (DEPTH, N_EXPERTS, D_MODEL), f32) * PEER_HEADS ** -0.5
    return {"x": x, "c": c, "ctx": ctx, "c_ctx": c_ctx, "ada_w": ada_w, "ada_b": ada_b,
            "norm1_gain": norm1_gain, "norm2_gain": norm2_gain, "w_qkv": w_qkv,
            "q_norm_gain": q_norm_gain, "k_norm_gain": k_norm_gain, "w_o": w_o,
            "attn_sinks": attn_sinks, "peer_w_q": peer_w_q, "peer_sub_keys": peer_sub_keys,
            "peer_u": peer_u, "peer_v": peer_v}


def reference(x, c, ctx, c_ctx, ada_w, ada_b, norm1_gain, norm2_gain, w_qkv, q_norm_gain,
              k_norm_gain, w_o, attn_sinks, peer_w_q, peer_sub_keys, peer_u, peer_v):
    n_lat = x.shape[1]
    rope = axial_rope_tables(n_lat, x.dtype)
    xc = ctx
    for i in range(DEPTH):
        last = i == DEPTH - 1
        mod = jax.nn.silu(c) @ ada_w[i] + ada_b[i]
        sh1, sc1, g1, sh2, sc2, g2 = [m[:, None, :] for m in jnp.split(mod, N_MOD, axis=-1)]
        modc = jax.nn.silu(c_ctx) @ ada_w[i] + ada_b[i]
        csh1, csc1, cg1, csh2, csc2, cg2 = jnp.split(modc, N_MOD, axis=-1)

        h = modulate(rms_norm(x, norm1_gain[i]), sh1, sc1)
        hc = modulate(rms_norm(xc, norm1_gain[i]), csh1, csc1)
        q, k, v = project_qkv(h, w_qkv[i], q_norm_gain[i], k_norm_gain[i], rope)
        qc, kc, vc = project_qkv(hc, w_qkv[i], q_norm_gain[i], k_norm_gain[i], None)
        if i % N_MIXERS == 0:
            o = latent_global_attention(q, k, v, kc, vc)
            sink = None
        else:
            sink = attn_sinks[i // N_MIXERS]
            o = latent_window_attention(q, k, v, kc, vc, sink)
        x = x + g1 * (o @ w_o[i])
        if not last:
            oc = context_attention(qc, kc, vc, sink)
            xc = xc + cg1 * (oc @ w_o[i])

        h2 = modulate(rms_norm(x, norm2_gain[i]), sh2, sc2)
        x = x + g2 * peer_ffn(h2, peer_w_q[i], peer_sub_keys[i], peer_u[i], peer_v[i])
        if not last:
            h2c = modulate(rms_norm(xc, norm2_gain[i]), csh2, csc2)
            xc = xc + cg2 * peer_ffn(h2c, peer_w_q[i], peer_sub_keys[i], peer_u[i], peer_v[i])
    return x
```

```python
import functools
import math

import jax
import jax.numpy as jnp
from jax import lax
from jax.experimental import pallas as pl
from jax.experimental.pallas import tpu as pltpu

F32 = jnp.float32
BF16 = jnp.bfloat16

D_MODEL = 1024
N_HEADS = 8
N_KV_HEADS = 2
HEAD_DIM = 128
KV_REP = N_HEADS // N_KV_HEADS
QKV_DIM = (N_HEADS + 2 * N_KV_HEADS) * HEAD_DIM
GRID_W = 64
WINDOW = 128
ROPE_THETA = 10000.0
RMS_EPS = 1e-6
NEG_INF = -1e30
PEER_HEADS = 8
PEER_TOPK = 16
N_KEYS = 128
PEER_QDIM = 256
N_MOD = 6
MOD_ROWS = 8

ROW_TILE = 256
Q_TILE = 128
PEER_TILE = 512
PEER_CHUNK = 8
SUB_ROWS = 32
VMEM_LIMIT = 56 * 1024 * 1024


def _cparams(sem):
    return pltpu.CompilerParams(dimension_semantics=sem, vmem_limit_bytes=VMEM_LIMIT)


def _nt_dot(a, b):
    return lax.dot_general(a, b, (((1,), (1,)), ((), ())), preferred_element_type=F32)


def _mod_kernel(c_ref, w_ref, b_ref, o_ref):
    c = c_ref[...]
    a = c * (1.0 / (1.0 + jnp.exp(-c)))
    o_ref[0] = jnp.dot(a, w_ref[0], preferred_element_type=F32,
                       precision=lax.Precision.HIGHEST) + b_ref[0]


def _modulation(cc, ada_w, ada_b):
    depth, d, n = ada_w.shape
    tn = 1536
    return pl.pallas_call(
        _mod_kernel,
        grid=(depth, n // tn),
        in_specs=[pl.BlockSpec((MOD_ROWS, d), lambda l, j: (0, 0)),
                  pl.BlockSpec((1, d, tn), lambda l, j: (l, 0, j)),
                  pl.BlockSpec((1, 1, tn), lambda l, j: (l, 0, j))],
        out_specs=pl.BlockSpec((1, MOD_ROWS, tn), lambda l, j: (l, 0, j)),
        out_shape=jax.ShapeDtypeStruct((depth, MOD_ROWS, n), F32),
        compiler_params=_cparams(("arbitrary", "arbitrary")),
    )(cc, ada_w, ada_b.reshape(depth, 1, n))


def _rms(x):
    return x * lax.rsqrt(jnp.mean(x * x, axis=-1, keepdims=True) + RMS_EPS)


def _qkv_kernel(x_ref, sh_ref, sc_ref, g_ref, w_ref, qg_ref, kg_ref, cos_ref, sin_ref,
                q_ref, k_ref, v_ref):
    h = _rms(x_ref[...]) * g_ref[...]
    h = h * (1.0 + sc_ref[0]) + sh_ref[0]
    y = jnp.dot(h.astype(BF16), w_ref[...], preferred_element_type=F32)
    cos = cos_ref[...]
    sin = sin_ref[...]
    lane = lax.broadcasted_iota(jnp.int32, cos.shape, 1)
    first = (lane % (HEAD_DIM // 2)) < (HEAD_DIM // 4)

    def head(col, gain):
        n = _rms(y[:, col * HEAD_DIM:(col + 1) * HEAD_DIM]) * gain
        sw = jnp.where(first, pltpu.roll(n, HEAD_DIM - HEAD_DIM // 4, 1),
                       pltpu.roll(n, HEAD_DIM // 4, 1))
        return n * cos + sw * sin

    for hh in range(N_HEADS):
        q_ref[:, hh * HEAD_DIM:(hh + 1) * HEAD_DIM] = head(hh, qg_ref[...]).astype(BF16)
    for g in range(N_KV_HEADS):
        k_ref[:, g * HEAD_DIM:(g + 1) * HEAD_DIM] = head(N_HEADS + g, kg_ref[...]).astype(BF16)
    v0 = (N_HEADS + N_KV_HEADS) * HEAD_DIM
    v_ref[...] = y[:, v0:].astype(BF16)


def _qkv(xs, mods, layer, gain, w, qg, kg, cos_t, sin_t, lat_tiles, tiles_per_seq, n_batch):
    n, d = xs.shape
    t = ROW_TILE

    def mrow(i):
        return layer * MOD_ROWS + jnp.where(i < lat_tiles, i // tiles_per_seq, n_batch)

    def trow(i):
        return jnp.where(i < lat_tiles, i % tiles_per_seq, tiles_per_seq)

    kvd = N_KV_HEADS * HEAD_DIM
    return pl.pallas_call(
        _qkv_kernel,
        grid=(n // t,),
        in_specs=[pl.BlockSpec((t, d), lambda i: (i, 0)),
                  pl.BlockSpec((1, 1, d), lambda i: (mrow(i), 0, 0)),
                  pl.BlockSpec((1, 1, d), lambda i: (mrow(i), 0, 1)),
                  pl.BlockSpec((1, d), lambda i: (0, 0)),
                  pl.BlockSpec((d, QKV_DIM), lambda i: (0, 0)),
                  pl.BlockSpec((1, HEAD_DIM), lambda i: (0, 0)),
                  pl.BlockSpec((1, HEAD_DIM), lambda i: (0, 0)),
                  pl.BlockSpec((t, HEAD_DIM), lambda i: (trow(i), 0)),
                  pl.BlockSpec((t, HEAD_DIM), lambda i: (trow(i), 0))],
        out_specs=[pl.BlockSpec((t, d), lambda i: (i, 0)),
                   pl.BlockSpec((t, kvd), lambda i: (i, 0)),
                   pl.BlockSpec((t, kvd), lambda i: (i, 0))],
        out_shape=[jax.ShapeDtypeStruct((n, d), BF16),
                   jax.ShapeDtypeStruct((n, kvd), BF16),
                   jax.ShapeDtypeStruct((n, kvd), BF16)],
        compiler_params=_cparams(("arbitrary",)),
    )(xs, mods, mods, gain, w, qg, kg, cos_t, sin_t)


def _stack_heads(q_ref):
    return jnp.concatenate(
        [q_ref[:, r * HEAD_DIM:(r + 1) * HEAD_DIM] for r in range(KV_REP)], axis=0)


def _unstack_heads(o, o_ref):
    rows = o.shape[0] // KV_REP
    for r in range(KV_REP):
        o_ref[:, r * HEAD_DIM:(r + 1) * HEAD_DIM] = o[r * rows:(r + 1) * rows].astype(BF16)


def _global_attn_kernel(lat_q_tiles, q_ref, kl_ref, kc_ref, vl_ref, vc_ref, o_ref):
    qt = pl.program_id(2)
    q4 = _stack_heads(q_ref)
    sc = _nt_dot(q4, kc_ref[...])

    @pl.when(qt < lat_q_tiles)
    def _():
        sl = _nt_dot(q4, kl_ref[...])
        m = jnp.maximum(jnp.max(sl, axis=-1, keepdims=True), jnp.max(sc, axis=-1, keepdims=True))
        p_l = jnp.exp(sl - m)
        p_c = jnp.exp(sc - m)
        den = jnp.sum(p_l, axis=-1, keepdims=True) + jnp.sum(p_c, axis=-1, keepdims=True)
        o = (jnp.dot(p_l.astype(BF16), vl_ref[...], preferred_element_type=F32)
             + jnp.dot(p_c.astype(BF16), vc_ref[...], preferred_element_type=F32))
        _unstack_heads(o * (1.0 / den), o_ref)

    @pl.when(qt >= lat_q_tiles)
    def _():
        m = jnp.max(sc, axis=-1, keepdims=True)
        p_c = jnp.exp(sc - m)
        den = jnp.sum(p_c, axis=-1, keepdims=True)
        o = jnp.dot(p_c.astype(BF16), vc_ref[...], preferred_element_type=F32)
        _unstack_heads(o * (1.0 / den), o_ref)


def _global_attention(q, k, v, n_batch, seq, ctx_len):
    n, d = q.shape
    lat_q = seq // Q_TILE
    ctx_q = ctx_len // Q_TILE
    ctx_blk0 = (n_batch * seq) // ctx_len
    gw = KV_REP * HEAD_DIM

    def qrow(b, g, t):
        return jnp.where(t < lat_q, b * lat_q + t, n_batch * lat_q + b * ctx_q + (t - lat_q))

    return pl.pallas_call(
        functools.partial(_global_attn_kernel, lat_q),
        grid=(n_batch, N_KV_HEADS, lat_q + ctx_q),
        in_specs=[pl.BlockSpec((Q_TILE, gw), lambda b, g, t: (qrow(b, g, t), g)),
                  pl.BlockSpec((seq, HEAD_DIM), lambda b, g, t: (b, g)),
                  pl.BlockSpec((ctx_len, HEAD_DIM), lambda b, g, t: (ctx_blk0 + b, g)),
                  pl.BlockSpec((seq, HEAD_DIM), lambda b, g, t: (b, g)),
                  pl.BlockSpec((ctx_len, HEAD_DIM), lambda b, g, t: (ctx_blk0 + b, g))],
        out_specs=pl.BlockSpec((Q_TILE, gw), lambda b, g, t: (qrow(b, g, t), g)),
        out_shape=jax.ShapeDtypeStruct((n, d), BF16),
        compiler_params=_cparams(("arbitrary", "arbitrary", "arbitrary")),
    )(q, k, k, v, v)


def _window_attn_kernel(seq, q_ref, kl_ref, kc_ref, vl_ref, vc_ref, sink_ref, o_ref):
    qt = pl.program_id(2)
    span = Q_TILE + 2 * WINDOW
    start = pl.multiple_of(jnp.clip(qt * Q_TILE - WINDOW, 0, seq - span), Q_TILE)
    q4 = _stack_heads(q_ref)
    rows = KV_REP * Q_TILE
    s_w = _nt_dot(q4, kl_ref[pl.ds(start, span), :])
    s_c = _nt_dot(q4, kc_ref[...])
    qpos = qt * Q_TILE + lax.broadcasted_iota(jnp.int32, (rows, 1), 0) % Q_TILE
    kpos = start + lax.broadcasted_iota(jnp.int32, (1, span), 1)
    s_w = jnp.where(jnp.abs(qpos - kpos) <= WINDOW, s_w, NEG_INF)
    sink = sink_ref[0]
    m = jnp.maximum(jnp.maximum(jnp.max(s_w, axis=-1, keepdims=True),
                                jnp.max(s_c, axis=-1, keepdims=True)), sink)
    p_w = jnp.exp(s_w - m)
    p_c = jnp.exp(s_c - m)
    den = (jnp.sum(p_w, axis=-1, keepdims=True) + jnp.sum(p_c, axis=-1, keepdims=True)
           + jnp.exp(sink - m))
    o = (jnp.dot(p_w.astype(BF16), vl_ref[pl.ds(start, span), :], preferred_element_type=F32)
         + jnp.dot(p_c.astype(BF16), vc_ref[...], preferred_element_type=F32))
    _unstack_heads(o * (1.0 / den), o_ref)


def _window_attention(q, k, v, sink_rows, n_batch, seq, ctx_len):
    n, d = q.shape
    lat_q = seq // Q_TILE
    ctx_blk0 = (n_batch * seq) // ctx_len
    gw = KV_REP * HEAD_DIM
    rows = KV_REP * Q_TILE
    return pl.pallas_call(
        functools.partial(_window_attn_kernel, seq),
        grid=(n_batch, N_KV_HEADS, lat_q),
        in_specs=[pl.BlockSpec((Q_TILE, gw), lambda b, g, t: (b * lat_q + t, g)),
                  pl.BlockSpec((seq, HEAD_DIM), lambda b, g, t: (b, g)),
                  pl.BlockSpec((ctx_len, HEAD_DIM), lambda b, g, t: (ctx_blk0 + b, g)),
                  pl.BlockSpec((seq, HEAD_DIM), lambda b, g, t: (b, g)),
                  pl.BlockSpec((ctx_len, HEAD_DIM), lambda b, g, t: (ctx_blk0 + b, g)),
                  pl.BlockSpec((1, rows, 1), lambda b, g, t: (g, 0, 0))],
        out_specs=pl.BlockSpec((Q_TILE, gw), lambda b, g, t: (b * lat_q + t, g)),
        out_shape=jax.ShapeDtypeStruct((n, d), BF16),
        compiler_params=_cparams(("arbitrary", "arbitrary", "arbitrary")),
    )(q, k, k, v, v, sink_rows)


def _peer_pre_kernel(o_ref, x_ref, wo_ref, g1_ref, n2_ref, sh_ref, sc_ref, wq_ref, keys_ref,
                     xn_ref, h2t_ref, st_ref):
    y = jnp.dot(o_ref[...], wo_ref[...], preferred_element_type=F32)
    xn = x_ref[...] + g1_ref[0] * y
    xn_ref[...] = xn
    h2 = _rms(xn) * n2_ref[...]
    h2 = h2 * (1.0 + sc_ref[0]) + sh_ref[0]
    h2t_ref[...] = h2.T.astype(BF16)
    qp = jnp.dot(h2.astype(BF16), wq_ref[...], preferred_element_type=F32)
    half = PEER_QDIM // 2
    for hs in range(2 * PEER_HEADS):
        qs = qp[:, hs * half:(hs + 1) * half].astype(BF16)
        st_ref[hs] = _nt_dot(keys_ref[hs % 2], qs)


def _peer_pre(o, xs, mods, layer, w_o, n2, w_q, keys, n_tiles, lat_tiles, tiles_per_seq, n_batch):
    n, d = xs.shape
    t = ROW_TILE
    qd = PEER_HEADS * PEER_QDIM

    def mrow(i):
        return layer * MOD_ROWS + jnp.where(i < lat_tiles, i // tiles_per_seq, n_batch)

    return pl.pallas_call(
        _peer_pre_kernel,
        grid=(n_tiles,),
        in_specs=[pl.BlockSpec((t, d), lambda i: (i, 0)),
                  pl.BlockSpec((t, d), lambda i: (i, 0)),
                  pl.BlockSpec((d, d), lambda i: (0, 0)),
                  pl.BlockSpec((1, 1, d), lambda i: (mrow(i), 0, 2)),
                  pl.BlockSpec((1, d), lambda i: (0, 0)),
                  pl.BlockSpec((1, 1, d), lambda i: (mrow(i), 0, 3)),
                  pl.BlockSpec((1, 1, d), lambda i: (mrow(i), 0, 4)),
                  pl.BlockSpec((d, qd), lambda i: (0, 0)),
                  pl.BlockSpec((2, N_KEYS, PEER_QDIM // 2), lambda i: (0, 0, 0))],
        out_specs=[pl.BlockSpec((t, d), lambda i: (i, 0)),
                   pl.BlockSpec((d, t), lambda i: (0, i)),
                   pl.BlockSpec((2 * PEER_HEADS, N_KEYS, t), lambda i: (0, 0, i))],
        out_shape=[jax.ShapeDtypeStruct((n, d), F32),
                   jax.ShapeDtypeStruct((d, n), BF16),
                   jax.ShapeDtypeStruct((2 * PEER_HEADS, N_KEYS, n), F32)],
        compiler_params=_cparams(("arbitrary",)),
    )(o, xs, w_o, mods, n2, mods, mods, w_q, keys)


def _top_values(s):
    vals = []
    for _ in range(PEER_TOPK):
        m = jnp.max(s, axis=0, keepdims=True)
        vals.append(m)
        s = jnp.where(s == m, -jnp.inf, s)
    return vals


def _peer_topk_kernel(st_ref, s2_ref, e2_ref, thr_ref, e1_ref):
    def per_head(h, carry):
        s1 = st_ref[2 * h]
        s2 = st_ref[2 * h + 1]
        v1 = _top_values(s1)
        v2 = jnp.concatenate(_top_values(s2), axis=0)
        cand = jnp.concatenate([v1[i] + v2 for i in range(PEER_TOPK)], axis=0)
        thr = _top_values(cand)[-1]
        v2max = v2[0:1]
        top = v1[0] + v2max
        z = jnp.sum(jnp.where(cand >= thr, jnp.exp(cand - top), 0.0), axis=0, keepdims=True)
        s2_ref[h] = s2
        e2_ref[h] = jnp.exp(s2 - v2max) * (1.0 / z)
        thr_ref[h] = thr - s1
        e1_ref[h] = jnp.exp(s1 - v1[0])
        return carry

    lax.fori_loop(0, PEER_HEADS, per_head, 0)


def _peer_topk(st, n_tiles):
    _, _, n = st.shape
    t = ROW_TILE
    spec = pl.BlockSpec((PEER_HEADS, N_KEYS, t), lambda i: (0, 0, i))
    shape = jax.ShapeDtypeStruct((PEER_HEADS, N_KEYS, n), F32)
    return pl.pallas_call(
        _peer_topk_kernel,
        grid=(n_tiles,),
        in_specs=[pl.BlockSpec((2 * PEER_HEADS, N_KEYS, t), lambda i: (0, 0, i))],
        out_specs=[spec, spec, spec, spec],
        out_shape=[shape, shape, shape, shape],
        compiler_params=_cparams(("arbitrary",)),
    )(st)


def _gelu_tanh(x):
    return 0.5 * x * (1.0 + jnp.tanh(math.sqrt(2.0 / math.pi) * (x + 0.044715 * (x * x * x))))


def _peer_dense_kernel(h2t_ref, s2_ref, e2_ref, thr_ref, e1_ref, u_ref, vt_ref, xn_ref, g2_ref,
                       out_ref, acc_ref, act_ref, slab_ref):
    j = pl.program_id(1)

    @pl.when(j == 0)
    def _():
        acc_ref[...] = jnp.zeros_like(acc_ref)

    act_ref[...] = jnp.dot(u_ref[...], h2t_ref[...], preferred_element_type=F32)

    def per_first_key(al, carry):
        a = j * PEER_CHUNK + al
        for bc in range(N_KEYS // SUB_ROWS):
            rows = slice(bc * SUB_ROWS, (bc + 1) * SUB_ROWS)
            w = jnp.zeros((SUB_ROWS, act_ref.shape[1]), F32)
            for h in range(PEER_HEADS):
                thr = thr_ref[h, pl.ds(a, 1), :]
                e1a = e1_ref[h, pl.ds(a, 1), :]
                w = w + jnp.where(s2_ref[h, rows, :] >= thr, e2_ref[h, rows, :], 0.0) * e1a
            r0 = pl.multiple_of(al * N_KEYS + bc * SUB_ROWS, SUB_ROWS)
            x = act_ref[pl.ds(r0, SUB_ROWS), :]
            slab_ref[pl.ds(r0, SUB_ROWS), :] = (w * _gelu_tanh(x)).astype(BF16)
        return carry

    lax.fori_loop(0, PEER_CHUNK, per_first_key, 0)
    acc_ref[...] += jnp.dot(vt_ref[...], slab_ref[...], preferred_element_type=F32)

    @pl.when(j == pl.num_programs(1) - 1)
    def _():
        out_ref[...] = xn_ref[...] + g2_ref[0] * acc_ref[...].T


def _peer_dense(h2t, coefs, u, vt, xn, mods, layer, n_tiles, lat_tiles, tiles_per_seq, n_batch):
    d = xn.shape[1]
    t = PEER_TILE
    ce = PEER_CHUNK * N_KEYS
    n_exp = u.shape[0]

    def mrow(i):
        return layer * MOD_ROWS + jnp.where(i < lat_tiles, i // tiles_per_seq, n_batch)

    cspec = pl.BlockSpec((PEER_HEADS, N_KEYS, t), lambda i, j: (0, 0, i))
    return pl.pallas_call(
        _peer_dense_kernel,
        grid=(n_tiles, n_exp // ce),
        in_specs=[pl.BlockSpec((d, t), lambda i, j: (0, i)),
                  cspec, cspec, cspec, cspec,
                  pl.BlockSpec((ce, d), lambda i, j: (j, 0)),
                  pl.BlockSpec((d, ce), lambda i, j: (0, j)),
                  pl.BlockSpec((t, d), lambda i, j: (i, 0)),
                  pl.BlockSpec((1, 1, d), lambda i, j: (mrow(i), 0, 5))],
        out_specs=pl.BlockSpec((t, d), lambda i, j: (i, 0)),
        out_shape=jax.ShapeDtypeStruct((n_tiles * t, d), F32),
        scratch_shapes=[pltpu.VMEM((d, t), F32),
                        pltpu.VMEM((ce, t), F32),
                        pltpu.VMEM((ce, t), BF16)],
        compiler_params=_cparams(("arbitrary", "arbitrary")),
    )(h2t, *coefs, u, vt, xn, mods)


def _rope_tables(seq, ctx_len):
    pos = jnp.arange(seq)
    row = (pos // GRID_W).astype(F32)
    col = (pos % GRID_W).astype(F32)
    half = HEAD_DIM // 2
    inv = ROPE_THETA ** (-jnp.arange(0, half, 2, dtype=F32) / half)
    ar = row[:, None] * inv
    ac = col[:, None] * inv
    cos = jnp.concatenate([jnp.cos(ar), jnp.cos(ar), jnp.cos(ac), jnp.cos(ac)], axis=-1)
    sin = jnp.concatenate([-jnp.sin(ar), jnp.sin(ar), -jnp.sin(ac), jnp.sin(ac)], axis=-1)
    cos = jnp.concatenate([cos, jnp.ones((ctx_len, HEAD_DIM), F32)], axis=0)
    sin = jnp.concatenate([sin, jnp.zeros((ctx_len, HEAD_DIM), F32)], axis=0)
    return cos, sin


def kernel(x, c, ctx, c_ctx, ada_w, ada_b, norm1_gain, norm2_gain, w_qkv, q_norm_gain, k_norm_gain,
           w_o, attn_sinks, peer_w_q, peer_sub_keys, peer_u, peer_v):
    n_batch, seq, d = x.shape
    ctx_len = ctx.shape[1]
    depth = ada_w.shape[0]
    n_lat = n_batch * seq
    n_ctx = n_batch * ctx_len
    assert depth == 2 and d == D_MODEL and n_batch < MOD_ROWS
    assert seq % PEER_TILE == 0 and n_ctx % PEER_TILE == 0 and ctx_len == ROW_TILE
    assert seq >= Q_TILE + 2 * WINDOW and n_lat % ctx_len == 0

    cc = jnp.concatenate([c, c_ctx[None, :], jnp.zeros((MOD_ROWS - n_batch - 1, d), F32)], axis=0)
    mods = _modulation(cc, ada_w, ada_b).reshape(depth * MOD_ROWS, 1, N_MOD * d)
    cos_t, sin_t = _rope_tables(seq, ctx_len)
    xs = jnp.concatenate([x.reshape(n_lat, d), ctx.reshape(n_ctx, d)], axis=0)
    scale = HEAD_DIM ** -0.5

    for layer in range(depth):
        last = layer == depth - 1
        q, k, v = _qkv(xs, mods, layer, norm1_gain[layer][None, :], w_qkv[layer].astype(BF16),
                       (q_norm_gain[layer] * scale)[None, :], k_norm_gain[layer][None, :],
                       cos_t, sin_t, n_lat // ROW_TILE, seq // ROW_TILE, n_batch)
        if layer % 2 == 0:
            o = _global_attention(q, k, v, n_batch, seq, ctx_len)
        else:
            sink = attn_sinks[layer // 2].reshape(N_KV_HEADS, KV_REP, 1, 1)
            sink_rows = jnp.broadcast_to(sink, (N_KV_HEADS, KV_REP, Q_TILE, 1))
            o = _window_attention(q, k, v, sink_rows.reshape(N_KV_HEADS, KV_REP * Q_TILE, 1),
                                  n_batch, seq, ctx_len)
        n_rows = n_lat if last else n_lat + n_ctx
        xn, h2t, st = _peer_pre(o, xs, mods, layer, w_o[layer].astype(BF16),
                                norm2_gain[layer][None, :], peer_w_q[layer].astype(BF16),
                                peer_sub_keys[layer].astype(BF16), n_rows // ROW_TILE,
                                n_lat // ROW_TILE, seq // ROW_TILE, n_batch)
        coefs = _peer_topk(st, n_rows // ROW_TILE)
        xs = _peer_dense(h2t, coefs, peer_u[layer].astype(BF16), peer_v[layer].T.astype(BF16),
                         xn, mods, layer, n_rows // PEER_TILE, n_lat // PEER_TILE,
                         seq // PEER_TILE, n_batch)
    return xs[:n_lat].reshape(n_batch, seq, d)
```

```python
import functools
import math

import jax
import jax.numpy as jnp
from jax import lax
from jax.experimental import pallas as pl
from jax.experimental.pallas import tpu as pltpu

F32 = jnp.float32
BF16 = jnp.bfloat16

D_MODEL = 1024
N_HEADS = 8
N_KV_HEADS = 2
HEAD_DIM = 128
KV_REP = N_HEADS // N_KV_HEADS
QKV_DIM = (N_HEADS + 2 * N_KV_HEADS) * HEAD_DIM
GRID_W = 64
WINDOW = 128
ROPE_THETA = 10000.0
RMS_EPS = 1e-6
NEG_INF = -1e30
PEER_HEADS = 8
PEER_TOPK = 16
N_KEYS = 128
PEER_QDIM = 256
N_MOD = 6
MOD_ROWS = 8

ROW_TILE = 256
Q_TILE = 128
PEER_TILE = 512
PEER_CHUNK = 8
SUB_LANES = 128
PIECE_ROWS = 256
NOT_RANKED = 64.0
VMEM_LIMIT = 56 * 1024 * 1024


def _cparams(sem):
    return pltpu.CompilerParams(dimension_semantics=sem, vmem_limit_bytes=VMEM_LIMIT)


def _nt_dot(a, b):
    return lax.dot_general(a, b, (((1,), (1,)), ((), ())), preferred_element_type=F32)


def _mod_kernel(c_ref, w_ref, b_ref, o_ref):
    c = c_ref[...]
    a = c * (1.0 / (1.0 + jnp.exp(-c)))
    o_ref[0] = jnp.dot(a, w_ref[0], preferred_element_type=F32,
                       precision=lax.Precision.HIGHEST) + b_ref[0]


def _modulation(cc, ada_w, ada_b):
    depth, d, n = ada_w.shape
    tn = 1536
    return pl.pallas_call(
        _mod_kernel,
        grid=(depth, n // tn),
        in_specs=[pl.BlockSpec((MOD_ROWS, d), lambda l, j: (0, 0)),
                  pl.BlockSpec((1, d, tn), lambda l, j: (l, 0, j)),
                  pl.BlockSpec((1, 1, tn), lambda l, j: (l, 0, j))],
        out_specs=pl.BlockSpec((1, MOD_ROWS, tn), lambda l, j: (l, 0, j)),
        out_shape=jax.ShapeDtypeStruct((depth, MOD_ROWS, n), F32),
        compiler_params=_cparams(("arbitrary", "arbitrary")),
    )(cc, ada_w, ada_b.reshape(depth, 1, n))


def _rms(x):
    return x * lax.rsqrt(jnp.mean(x * x, axis=-1, keepdims=True) + RMS_EPS)


def _qkv_kernel(x_ref, sh_ref, sc_ref, g_ref, w_ref, qg_ref, kg_ref, cos_ref, sin_ref,
                q_ref, k_ref, v_ref):
    h = _rms(x_ref[...]) * g_ref[...]
    h = h * (1.0 + sc_ref[0]) + sh_ref[0]
    y = jnp.dot(h.astype(BF16), w_ref[...], preferred_element_type=F32)
    cos = cos_ref[...]
    sin = sin_ref[...]
    lane = lax.broadcasted_iota(jnp.int32, cos.shape, 1)
    first = (lane % (HEAD_DIM // 2)) < (HEAD_DIM // 4)

    def head(col, gain):
        n = _rms(y[:, col * HEAD_DIM:(col + 1) * HEAD_DIM]) * gain
        sw = jnp.where(first, pltpu.roll(n, HEAD_DIM - HEAD_DIM // 4, 1),
                       pltpu.roll(n, HEAD_DIM // 4, 1))
        return n * cos + sw * sin

    for hh in range(N_HEADS):
        q_ref[:, hh * HEAD_DIM:(hh + 1) * HEAD_DIM] = head(hh, qg_ref[...]).astype(BF16)
    for g in range(N_KV_HEADS):
        k_ref[:, g * HEAD_DIM:(g + 1) * HEAD_DIM] = head(N_HEADS + g, kg_ref[...]).astype(BF16)
    v0 = (N_HEADS + N_KV_HEADS) * HEAD_DIM
    v_ref[...] = y[:, v0:].astype(BF16)


def _qkv(xs, mods, layer, gain, w, qg, kg, cos_t, sin_t, lat_tiles, tiles_per_seq, n_batch):
    n, d = xs.shape
    t = ROW_TILE

    def mrow(i):
        return layer * MOD_ROWS + jnp.where(i < lat_tiles, i // tiles_per_seq, n_batch)

    def trow(i):
        return jnp.where(i < lat_tiles, i % tiles_per_seq, tiles_per_seq)

    kvd = N_KV_HEADS * HEAD_DIM
    return pl.pallas_call(
        _qkv_kernel,
        grid=(n // t,),
        in_specs=[pl.BlockSpec((t, d), lambda i: (i, 0)),
                  pl.BlockSpec((1, 1, d), lambda i: (mrow(i), 0, 0)),
                  pl.BlockSpec((1, 1, d), lambda i: (mrow(i), 0, 1)),
                  pl.BlockSpec((1, d), lambda i: (0, 0)),
                  pl.BlockSpec((d, QKV_DIM), lambda i: (0, 0)),
                  pl.BlockSpec((1, HEAD_DIM), lambda i: (0, 0)),
                  pl.BlockSpec((1, HEAD_DIM), lambda i: (0, 0)),
                  pl.BlockSpec((t, HEAD_DIM), lambda i: (trow(i), 0)),
                  pl.BlockSpec((t, HEAD_DIM), lambda i: (trow(i), 0))],
        out_specs=[pl.BlockSpec((t, d), lambda i: (i, 0)),
                   pl.BlockSpec((t, kvd), lambda i: (i, 0)),
                   pl.BlockSpec((t, kvd), lambda i: (i, 0))],
        out_shape=[jax.ShapeDtypeStruct((n, d), BF16),
                   jax.ShapeDtypeStruct((n, kvd), BF16),
                   jax.ShapeDtypeStruct((n, kvd), BF16)],
        compiler_params=_cparams(("arbitrary",)),
    )(xs, mods, mods, gain, w, qg, kg, cos_t, sin_t)


def _stack_heads(q_ref):
    return jnp.concatenate(
        [q_ref[:, r * HEAD_DIM:(r + 1) * HEAD_DIM] for r in range(KV_REP)], axis=0)


def _unstack_heads(o, o_ref):
    rows = o.shape[0] // KV_REP
    for r in range(KV_REP):
        o_ref[:, r * HEAD_DIM:(r + 1) * HEAD_DIM] = o[r * rows:(r + 1) * rows].astype(BF16)


def _global_attn_kernel(lat_q_tiles, q_ref, kl_ref, kc_ref, vl_ref, vc_ref, o_ref):
    qt = pl.program_id(2)
    q4 = _stack_heads(q_ref)
    sc = _nt_dot(q4, kc_ref[...])

    @pl.when(qt < lat_q_tiles)
    def _():
        sl = _nt_dot(q4, kl_ref[...])
        m = jnp.maximum(jnp.max(sl, axis=-1, keepdims=True), jnp.max(sc, axis=-1, keepdims=True))
        p_l = jnp.exp(sl - m)
        p_c = jnp.exp(sc - m)
        den = jnp.sum(p_l, axis=-1, keepdims=True) + jnp.sum(p_c, axis=-1, keepdims=True)
        o = (jnp.dot(p_l.astype(BF16), vl_ref[...], preferred_element_type=F32)
             + jnp.dot(p_c.astype(BF16), vc_ref[...], preferred_element_type=F32))
        _unstack_heads(o * (1.0 / den), o_ref)

    @pl.when(qt >= lat_q_tiles)
    def _():
        m = jnp.max(sc, axis=-1, keepdims=True)
        p_c = jnp.exp(sc - m)
        den = jnp.sum(p_c, axis=-1, keepdims=True)
        o = jnp.dot(p_c.astype(BF16), vc_ref[...], preferred_element_type=F32)
        _unstack_heads(o * (1.0 / den), o_ref)


def _global_attention(q, k, v, n_batch, seq, ctx_len):
    n, d = q.shape
    lat_q = seq // Q_TILE
    ctx_q = ctx_len // Q_TILE
    ctx_blk0 = (n_batch * seq) // ctx_len
    gw = KV_REP * HEAD_DIM

    def qrow(b, g, t):
        return jnp.where(t < lat_q, b * lat_q + t, n_batch * lat_q + b * ctx_q + (t - lat_q))

    return pl.pallas_call(
        functools.partial(_global_attn_kernel, lat_q),
        grid=(n_batch, N_KV_HEADS, lat_q + ctx_q),
        in_specs=[pl.BlockSpec((Q_TILE, gw), lambda b, g, t: (qrow(b, g, t), g)),
                  pl.BlockSpec((seq, HEAD_DIM), lambda b, g, t: (b, g)),
                  pl.BlockSpec((ctx_len, HEAD_DIM), lambda b, g, t: (ctx_blk0 + b, g)),
                  pl.BlockSpec((seq, HEAD_DIM), lambda b, g, t: (b, g)),
                  pl.BlockSpec((ctx_len, HEAD_DIM), lambda b, g, t: (ctx_blk0 + b, g))],
        out_specs=pl.BlockSpec((Q_TILE, gw), lambda b, g, t: (qrow(b, g, t), g)),
        out_shape=jax.ShapeDtypeStruct((n, d), BF16),
        compiler_params=_cparams(("arbitrary", "arbitrary", "arbitrary")),
    )(q, k, k, v, v)


def _window_attn_kernel(seq, q_ref, kl_ref, kc_ref, vl_ref, vc_ref, sink_ref, o_ref):
    qt = pl.program_id(2)
    span = Q_TILE + 2 * WINDOW
    start = pl.multiple_of(jnp.clip(qt * Q_TILE - WINDOW, 0, seq - span), Q_TILE)
    q4 = _stack_heads(q_ref)
    rows = KV_REP * Q_TILE
    s_w = _nt_dot(q4, kl_ref[pl.ds(start, span), :])
    s_c = _nt_dot(q4, kc_ref[...])
    qpos = qt * Q_TILE + lax.broadcasted_iota(jnp.int32, (rows, 1), 0) % Q_TILE
    kpos = start + lax.broadcasted_iota(jnp.int32, (1, span), 1)
    s_w = jnp.where(jnp.abs(qpos - kpos) <= WINDOW, s_w, NEG_INF)
    sink = sink_ref[0]
    m = jnp.maximum(jnp.maximum(jnp.max(s_w, axis=-1, keepdims=True),
                                jnp.max(s_c, axis=-1, keepdims=True)), sink)
    p_w = jnp.exp(s_w - m)
    p_c = jnp.exp(s_c - m)
    den = (jnp.sum(p_w, axis=-1, keepdims=True) + jnp.sum(p_c, axis=-1, keepdims=True)
           + jnp.exp(sink - m))
    o = (jnp.dot(p_w.astype(BF16), vl_ref[pl.ds(start, span), :], preferred_element_type=F32)
         + jnp.dot(p_c.astype(BF16), vc_ref[...], preferred_element_type=F32))
    _unstack_heads(o * (1.0 / den), o_ref)


def _window_attention(q, k, v, sink_rows, n_batch, seq, ctx_len):
    n, d = q.shape
    lat_q = seq // Q_TILE
    ctx_blk0 = (n_batch * seq) // ctx_len
    gw = KV_REP * HEAD_DIM
    rows = KV_REP * Q_TILE
    return pl.pallas_call(
        functools.partial(_window_attn_kernel, seq),
        grid=(n_batch, N_KV_HEADS, lat_q),
        in_specs=[pl.BlockSpec((Q_TILE, gw), lambda b, g, t: (b * lat_q + t, g)),
                  pl.BlockSpec((seq, HEAD_DIM), lambda b, g, t: (b, g)),
                  pl.BlockSpec((ctx_len, HEAD_DIM), lambda b, g, t: (ctx_blk0 + b, g)),
                  pl.BlockSpec((seq, HEAD_DIM), lambda b, g, t: (b, g)),
                  pl.BlockSpec((ctx_len, HEAD_DIM), lambda b, g, t: (ctx_blk0 + b, g)),
                  pl.BlockSpec((1, rows, 1), lambda b, g, t: (g, 0, 0))],
        out_specs=pl.BlockSpec((Q_TILE, gw), lambda b, g, t: (b * lat_q + t, g)),
        out_shape=jax.ShapeDtypeStruct((n, d), BF16),
        compiler_params=_cparams(("arbitrary", "arbitrary", "arbitrary")),
    )(q, k, k, v, v, sink_rows)


def _peer_pre_kernel(o_ref, x_ref, wo_ref, g1_ref, n2_ref, sh_ref, sc_ref, wq_ref, keys_ref,
                     xn_ref, h2t_ref, st_ref):
    y = jnp.dot(o_ref[...], wo_ref[...], preferred_element_type=F32)
    xn = x_ref[...] + g1_ref[0] * y
    xn_ref[...] = xn
    h2 = _rms(xn) * n2_ref[...]
    h2 = h2 * (1.0 + sc_ref[0]) + sh_ref[0]
    h2t_ref[...] = h2.T.astype(BF16)
    qp = jnp.dot(h2.astype(BF16), wq_ref[...], preferred_element_type=F32)
    half = PEER_QDIM // 2
    for hs in range(2 * PEER_HEADS):
        qs = qp[:, hs * half:(hs + 1) * half].astype(BF16)
        st_ref[hs] = _nt_dot(keys_ref[hs % 2], qs)


def _peer_pre(o, xs, mods, layer, w_o, n2, w_q, keys, n_tiles, lat_tiles, tiles_per_seq, n_batch):
    n, d = xs.shape
    t = ROW_TILE
    qd = PEER_HEADS * PEER_QDIM

    def mrow(i):
        return layer * MOD_ROWS + jnp.where(i < lat_tiles, i // tiles_per_seq, n_batch)

    return pl.pallas_call(
        _peer_pre_kernel,
        grid=(n_tiles,),
        in_specs=[pl.BlockSpec((t, d), lambda i: (i, 0)),
                  pl.BlockSpec((t, d), lambda i: (i, 0)),
                  pl.BlockSpec((d, d), lambda i: (0, 0)),
                  pl.BlockSpec((1, 1, d), lambda i: (mrow(i), 0, 2)),
                  pl.BlockSpec((1, d), lambda i: (0, 0)),
                  pl.BlockSpec((1, 1, d), lambda i: (mrow(i), 0, 3)),
                  pl.BlockSpec((1, 1, d), lambda i: (mrow(i), 0, 4)),
                  pl.BlockSpec((d, qd), lambda i: (0, 0)),
                  pl.BlockSpec((2, N_KEYS, PEER_QDIM // 2), lambda i: (0, 0, 0))],
        out_specs=[pl.BlockSpec((t, d), lambda i: (i, 0)),
                   pl.BlockSpec((d, t), lambda i: (0, i)),
                   pl.BlockSpec((2 * PEER_HEADS, N_KEYS, t), lambda i: (0, 0, i))],
        out_shape=[jax.ShapeDtypeStruct((n, d), F32),
                   jax.ShapeDtypeStruct((d, n), BF16),
                   jax.ShapeDtypeStruct((2 * PEER_HEADS, N_KEYS, n), F32)],
        compiler_params=_cparams(("arbitrary",)),
    )(o, xs, w_o, mods, n2, mods, mods, w_q, keys)


def _top_values(s, with_rank=False):
    vals = []
    rank = jnp.full(s.shape, NOT_RANKED, F32) if with_rank else None
    for j in range(PEER_TOPK):
        m = jnp.max(s, axis=0, keepdims=True)
        vals.append(m)
        hit = s == m
        if with_rank:
            rank = jnp.where(hit, float(j), rank)
        s = jnp.where(hit, -jnp.inf, s)
    return vals, rank


def _candidate_sums(v1, v2):
    assert PEER_TOPK == 16
    a1 = jnp.concatenate(v1, axis=0)
    a2 = jnp.concatenate(v2, axis=0)
    row = lax.broadcasted_iota(jnp.int32, (8, a1.shape[1]), 0)
    ninf = -jnp.inf
    return jnp.concatenate([
        v1[0] + a2,
        v1[1] + a2[0:8],
        jnp.where(row < 5, v1[2] + a2[0:8], ninf),
        jnp.where(row < 4, v1[3] + a2[0:8], ninf),
        a1[8:16] + v2[0],
        jnp.where(row >= 4, a1[0:8] + v2[0], ninf),
        jnp.where(row >= 4, a1[0:8] + v2[1], ninf),
        jnp.where(row == 4, a1[0:8] + v2[2], ninf),
    ], axis=0), a1


def _peer_topk_kernel(st_ref, r2_ref, e2_ref, c_ref, e1_ref):
    def per_head(h, carry):
        s1 = st_ref[2 * h]
        s2 = st_ref[2 * h + 1]
        v1, _ = _top_values(s1)
        v2, rank2 = _top_values(s2, with_rank=True)
        cand, a1 = _candidate_sums(v1, v2)
        thr = _top_values(cand)[0][-1]
        top = v1[0] + v2[0]
        z = jnp.sum(jnp.where(cand >= thr, jnp.exp(cand - top), 0.0), axis=0, keepdims=True)
        cnt = jnp.zeros(a1.shape, F32)
        for j in range(PEER_TOPK):
            cnt = cnt + jnp.where(a1 + v2[j] >= thr, 1.0, 0.0)
        c = jnp.zeros(s1.shape, F32)
        for i in range(PEER_TOPK):
            c = jnp.where(s1 == v1[i], cnt[i:i + 1], c)
        r2_ref[h] = rank2.astype(BF16)
        e2_ref[h] = (jnp.exp(s2 - v2[0]) * (1.0 / z)).astype(BF16)
        c_ref[h] = c
        e1_ref[h] = jnp.exp(s1 - v1[0])
        return carry

    lax.fori_loop(0, PEER_HEADS, per_head, 0)


def _peer_topk(st, n_tiles):
    _, _, n = st.shape
    t = ROW_TILE
    spec = pl.BlockSpec((PEER_HEADS, N_KEYS, t), lambda i: (0, 0, i))
    narrow = jax.ShapeDtypeStruct((PEER_HEADS, N_KEYS, n), BF16)
    wide = jax.ShapeDtypeStruct((PEER_HEADS, N_KEYS, n), F32)
    return pl.pallas_call(
        _peer_topk_kernel,
        grid=(n_tiles,),
        in_specs=[pl.BlockSpec((2 * PEER_HEADS, N_KEYS, t), lambda i: (0, 0, i))],
        out_specs=[spec, spec, spec, spec],
        out_shape=[narrow, narrow, wide, wide],
        compiler_params=_cparams(("arbitrary",)),
    )(st)


def _gelu_tanh(x):
    k = -2.0 * math.sqrt(2.0 / math.pi)
    return x / (1.0 + jnp.exp(x * (k + (k * 0.044715) * (x * x))))


def _peer_dense_kernel(h2t_ref, r2_in_ref, e2_in_ref, c_ref, e1_ref, u_ref, vt_ref, xn_ref, g2_ref,
                       out_ref, acc_ref, act_ref, slab_ref, r2_ref, e2_ref):
    j = pl.program_id(1)
    lanes = acc_ref.shape[1]

    @pl.when(j == 0)
    def _():
        acc_ref[...] = jnp.zeros_like(acc_ref)
        r2_ref[...] = r2_in_ref[...]
        e2_ref[...] = e2_in_ref[...]

    for p in range(u_ref.shape[0] // PIECE_ROWS):
        r = p * PIECE_ROWS
        act_ref[p] = jnp.dot(u_ref[r:r + PIECE_ROWS, :], h2t_ref[...], preferred_element_type=F32)
        for a2 in range(PIECE_ROWS // N_KEYS):
            a = j * PEER_CHUNK + p * (PIECE_ROWS // N_KEYS) + a2
            for lc in range(lanes // SUB_LANES):
                cols = slice(lc * SUB_LANES, (lc + 1) * SUB_LANES)
                w = jnp.zeros((N_KEYS, SUB_LANES), BF16)
                for h in range(PEER_HEADS):
                    c = jnp.broadcast_to(c_ref[h, pl.ds(a, 1), :][:, cols],
                                         (N_KEYS, SUB_LANES)).astype(BF16)
                    e1 = jnp.broadcast_to(e1_ref[h, pl.ds(a, 1), :][:, cols],
                                          (N_KEYS, SUB_LANES)).astype(BF16)
                    w = w + jnp.where(r2_ref[h, :, cols] < c, e2_ref[h, :, cols],
                                      jnp.zeros((), BF16)) * e1
                x = act_ref[p, a2 * N_KEYS:(a2 + 1) * N_KEYS, cols]
                slab_ref[p, a2 * N_KEYS:(a2 + 1) * N_KEYS, cols] = w * _gelu_tanh(x).astype(BF16)
        acc_ref[...] += jnp.dot(vt_ref[:, r:r + PIECE_ROWS], slab_ref[p], preferred_element_type=F32)

    @pl.when(j == pl.num_programs(1) - 1)
    def _():
        out_ref[...] = xn_ref[...] + g2_ref[0] * acc_ref[...].T


def _peer_dense(h2t, coefs, u, vt, xn, mods, layer, n_tiles, lat_tiles, tiles_per_seq, n_batch):
    d = xn.shape[1]
    t = PEER_TILE
    ce = PEER_CHUNK * N_KEYS
    n_exp = u.shape[0]

    def mrow(i):
        return layer * MOD_ROWS + jnp.where(i < lat_tiles, i // tiles_per_seq, n_batch)

    cspec = pl.BlockSpec((PEER_HEADS, N_KEYS, t), lambda i, j: (0, 0, i))
    return pl.pallas_call(
        _peer_dense_kernel,
        grid=(n_tiles, n_exp // ce),
        in_specs=[pl.BlockSpec((d, t), lambda i, j: (0, i)),
                  cspec, cspec, cspec, cspec,
                  pl.BlockSpec((ce, d), lambda i, j: (j, 0)),
                  pl.BlockSpec((d, ce), lambda i, j: (0, j)),
                  pl.BlockSpec((t, d), lambda i, j: (i, 0)),
                  pl.BlockSpec((1, 1, d), lambda i, j: (mrow(i), 0, 5))],
        out_specs=pl.BlockSpec((t, d), lambda i, j: (i, 0)),
        out_shape=jax.ShapeDtypeStruct((n_tiles * t, d), F32),
        scratch_shapes=[pltpu.VMEM((d, t), F32),
                        pltpu.VMEM((ce // PIECE_ROWS, PIECE_ROWS, t), F32),
                        pltpu.VMEM((ce // PIECE_ROWS, PIECE_ROWS, t), BF16),
                        pltpu.VMEM((PEER_HEADS, N_KEYS, t), BF16),
                        pltpu.VMEM((PEER_HEADS, N_KEYS, t), BF16)],
        compiler_params=_cparams(("arbitrary", "arbitrary")),
    )(h2t, *coefs, u, vt, xn, mods)


def _rope_tables(seq, ctx_len):
    pos = jnp.arange(seq)
    row = (pos // GRID_W).astype(F32)
    col = (pos % GRID_W).astype(F32)
    half = HEAD_DIM // 2
    inv = ROPE_THETA ** (-jnp.arange(0, half, 2, dtype=F32) / half)
    ar = row[:, None] * inv
    ac = col[:, None] * inv
    cos = jnp.concatenate([jnp.cos(ar), jnp.cos(ar), jnp.cos(ac), jnp.cos(ac)], axis=-1)
    sin = jnp.concatenate([-jnp.sin(ar), jnp.sin(ar), -jnp.sin(ac), jnp.sin(ac)], axis=-1)
    cos = jnp.concatenate([cos, jnp.ones((ctx_len, HEAD_DIM), F32)], axis=0)
    sin = jnp.concatenate([sin, jnp.zeros((ctx_len, HEAD_DIM), F32)], axis=0)
    return cos, sin


def kernel(x, c, ctx, c_ctx, ada_w, ada_b, norm1_gain, norm2_gain, w_qkv, q_norm_gain, k_norm_gain,
           w_o, attn_sinks, peer_w_q, peer_sub_keys, peer_u, peer_v):
    n_batch, seq, d = x.shape
    ctx_len = ctx.shape[1]
    depth = ada_w.shape[0]
    n_lat = n_batch * seq
    n_ctx = n_batch * ctx_len
    assert depth == 2 and d == D_MODEL and n_batch < MOD_ROWS
    assert seq % PEER_TILE == 0 and n_ctx % PEER_TILE == 0 and ctx_len == ROW_TILE
    assert seq >= Q_TILE + 2 * WINDOW and n_lat % ctx_len == 0

    cc = jnp.concatenate([c, c_ctx[None, :], jnp.zeros((MOD_ROWS - n_batch - 1, d), F32)], axis=0)
    mods = _modulation(cc, ada_w, ada_b).reshape(depth * MOD_ROWS, 1, N_MOD * d)
    cos_t, sin_t = _rope_tables(seq, ctx_len)
    xs = jnp.concatenate([x.reshape(n_lat, d), ctx.reshape(n_ctx, d)], axis=0)
    scale = HEAD_DIM ** -0.5

    for layer in range(depth):
        last = layer == depth - 1
        q, k, v = _qkv(xs, mods, layer, norm1_gain[layer][None, :], w_qkv[layer].astype(BF16),
                       (q_norm_gain[layer] * scale)[None, :], k_norm_gain[layer][None, :],
                       cos_t, sin_t, n_lat // ROW_TILE, seq // ROW_TILE, n_batch)
        if layer % 2 == 0:
            o = _global_attention(q, k, v, n_batch, seq, ctx_len)
        else:
            sink = attn_sinks[layer // 2].reshape(N_KV_HEADS, KV_REP, 1, 1)
            sink_rows = jnp.broadcast_to(sink, (N_KV_HEADS, KV_REP, Q_TILE, 1))
            o = _window_attention(q, k, v, sink_rows.reshape(N_KV_HEADS, KV_REP * Q_TILE, 1),
                                  n_batch, seq, ctx_len)
        n_rows = n_lat if last else n_lat + n_ctx
        xn, h2t, st = _peer_pre(o, xs, mods, layer, w_o[layer].astype(BF16),
                                norm2_gain[layer][None, :], peer_w_q[layer].astype(BF16),
                                peer_sub_keys[layer].astype(BF16), n_rows // ROW_TILE,
                                n_lat // ROW_TILE, seq // ROW_TILE, n_batch)
        coefs = _peer_topk(st, n_rows // ROW_TILE)
        xs = _peer_dense(h2t, coefs, peer_u[layer].astype(BF16), peer_v[layer].T.astype(BF16),
                         xn, mods, layer, n_rows // PEER_TILE, n_lat // PEER_TILE,
                         seq // PEER_TILE, n_batch)
    return xs[:n_lat].reshape(n_batch, seq, d)
```

```python
import functools
import math

import jax
import jax.numpy as jnp
from jax import lax
from jax.experimental import pallas as pl
from jax.experimental.pallas import tpu as pltpu

F32 = jnp.float32
BF16 = jnp.bfloat16

D_MODEL = 1024
N_HEADS = 8
N_KV_HEADS = 2
HEAD_DIM = 128
KV_REP = N_HEADS // N_KV_HEADS
QKV_DIM = (N_HEADS + 2 * N_KV_HEADS) * HEAD_DIM
GRID_W = 64
WINDOW = 128
ROPE_THETA = 10000.0
RMS_EPS = 1e-6
NEG_INF = -1e30
PEER_HEADS = 8
PEER_TOPK = 16
N_KEYS = 128
PEER_QDIM = 256
N_MOD = 6
MOD_ROWS = 8

ROW_TILE = 256
Q_TILE = 128
PEER_TILE = 512
PEER_CHUNK = 8
SUB_LANES = 128
PIECE_ROWS = 256
NOT_RANKED = 64.0
VMEM_LIMIT = 56 * 1024 * 1024


def _cparams(sem):
    return pltpu.CompilerParams(dimension_semantics=sem, vmem_limit_bytes=VMEM_LIMIT)


def _nt_dot(a, b):
    return lax.dot_general(a, b, (((1,), (1,)), ((), ())), preferred_element_type=F32)


def _mod_kernel(c_ref, w_ref, b_ref, o_ref):
    c = c_ref[...]
    a = c * (1.0 / (1.0 + jnp.exp(-c)))
    o_ref[0] = jnp.dot(a, w_ref[0], preferred_element_type=F32,
                       precision=lax.Precision.HIGHEST) + b_ref[0]


def _modulation(cc, ada_w, ada_b):
    depth, d, n = ada_w.shape
    tn = 1536
    return pl.pallas_call(
        _mod_kernel,
        grid=(depth, n // tn),
        in_specs=[pl.BlockSpec((MOD_ROWS, d), lambda l, j: (0, 0)),
                  pl.BlockSpec((1, d, tn), lambda l, j: (l, 0, j)),
                  pl.BlockSpec((1, 1, tn), lambda l, j: (l, 0, j))],
        out_specs=pl.BlockSpec((1, MOD_ROWS, tn), lambda l, j: (l, 0, j)),
        out_shape=jax.ShapeDtypeStruct((depth, MOD_ROWS, n), F32),
        compiler_params=_cparams(("arbitrary", "arbitrary")),
    )(cc, ada_w, ada_b.reshape(depth, 1, n))


def _rms(x):
    return x * lax.rsqrt(jnp.mean(x * x, axis=-1, keepdims=True) + RMS_EPS)


def _qkv_kernel(x_ref, sh_ref, sc_ref, g_ref, w_ref, qg_ref, kg_ref, cos_ref, sin_ref,
                q_ref, k_ref, v_ref, vt_ref):
    h = _rms(x_ref[...]) * g_ref[...]
    h = h * (1.0 + sc_ref[0]) + sh_ref[0]
    y = jnp.dot(h.astype(BF16), w_ref[...], preferred_element_type=F32)
    cos = cos_ref[...]
    sin = sin_ref[...]
    lane = lax.broadcasted_iota(jnp.int32, cos.shape, 1)
    first = (lane % (HEAD_DIM // 2)) < (HEAD_DIM // 4)

    def head(col, gain):
        n = _rms(y[:, col * HEAD_DIM:(col + 1) * HEAD_DIM]) * gain
        sw = jnp.where(first, pltpu.roll(n, HEAD_DIM - HEAD_DIM // 4, 1),
                       pltpu.roll(n, HEAD_DIM // 4, 1))
        return n * cos + sw * sin

    for hh in range(N_HEADS):
        q_ref[:, hh * HEAD_DIM:(hh + 1) * HEAD_DIM] = head(hh, qg_ref[...]).astype(BF16)
    for g in range(N_KV_HEADS):
        k_ref[:, g * HEAD_DIM:(g + 1) * HEAD_DIM] = head(N_HEADS + g, kg_ref[...]).astype(BF16)
    v0 = (N_HEADS + N_KV_HEADS) * HEAD_DIM
    v_ref[...] = y[:, v0:].astype(BF16)
    vt_ref[...] = y[:, v0:].T.astype(BF16)


def _qkv(xs, mods, layer, gain, w, qg, kg, cos_t, sin_t, lat_tiles, tiles_per_seq, n_batch):
    n, d = xs.shape
    t = ROW_TILE

    def mrow(i):
        return layer * MOD_ROWS + jnp.where(i < lat_tiles, i // tiles_per_seq, n_batch)

    def trow(i):
        return jnp.where(i < lat_tiles, i % tiles_per_seq, tiles_per_seq)

    kvd = N_KV_HEADS * HEAD_DIM
    return pl.pallas_call(
        _qkv_kernel,
        grid=(n // t,),
        in_specs=[pl.BlockSpec((t, d), lambda i: (i, 0)),
                  pl.BlockSpec((1, 1, d), lambda i: (mrow(i), 0, 0)),
                  pl.BlockSpec((1, 1, d), lambda i: (mrow(i), 0, 1)),
                  pl.BlockSpec((1, d), lambda i: (0, 0)),
                  pl.BlockSpec((d, QKV_DIM), lambda i: (0, 0)),
                  pl.BlockSpec((1, HEAD_DIM), lambda i: (0, 0)),
                  pl.BlockSpec((1, HEAD_DIM), lambda i: (0, 0)),
                  pl.BlockSpec((t, HEAD_DIM), lambda i: (trow(i), 0)),
                  pl.BlockSpec((t, HEAD_DIM), lambda i: (trow(i), 0))],
        out_specs=[pl.BlockSpec((t, d), lambda i: (i, 0)),
                   pl.BlockSpec((t, kvd), lambda i: (i, 0)),
                   pl.BlockSpec((t, kvd), lambda i: (i, 0)),
                   pl.BlockSpec((kvd, t), lambda i: (0, i))],
        out_shape=[jax.ShapeDtypeStruct((n, d), BF16),
                   jax.ShapeDtypeStruct((n, kvd), BF16),
                   jax.ShapeDtypeStruct((n, kvd), BF16),
                   jax.ShapeDtypeStruct((kvd, n), BF16)],
        compiler_params=_cparams(("arbitrary",)),
    )(xs, mods, mods, gain, w, qg, kg, cos_t, sin_t)


def _stack_heads(q_ref):
    return jnp.concatenate(
        [q_ref[:, r * HEAD_DIM:(r + 1) * HEAD_DIM] for r in range(KV_REP)], axis=0)


def _unstack_heads(o, o_ref):
    rows = o.shape[0] // KV_REP
    for r in range(KV_REP):
        o_ref[:, r * HEAD_DIM:(r + 1) * HEAD_DIM] = o[r * rows:(r + 1) * rows].astype(BF16)


def _global_attn_kernel(lat_q_tiles, q_ref, kl_ref, kc_ref, vtl_ref, vtc_ref, o_ref):
    qt = pl.program_id(2)
    q4 = _stack_heads(q_ref)
    sc = _nt_dot(kc_ref[...], q4)

    def finish(ot, den):
        ot = ot * (1.0 / den)
        for r in range(KV_REP):
            blk = ot[:, r * Q_TILE:(r + 1) * Q_TILE]
            o_ref[:, r * HEAD_DIM:(r + 1) * HEAD_DIM] = blk.T.astype(BF16)

    @pl.when(qt < lat_q_tiles)
    def _():
        sl = _nt_dot(kl_ref[...], q4)
        m = jnp.maximum(jnp.max(sl, axis=0, keepdims=True), jnp.max(sc, axis=0, keepdims=True))
        p_l = jnp.exp2(sl - m)
        p_c = jnp.exp2(sc - m)
        den = jnp.sum(p_l, axis=0, keepdims=True) + jnp.sum(p_c, axis=0, keepdims=True)
        finish(jnp.dot(vtl_ref[...], p_l.astype(BF16), preferred_element_type=F32)
               + jnp.dot(vtc_ref[...], p_c.astype(BF16), preferred_element_type=F32), den)

    @pl.when(qt >= lat_q_tiles)
    def _():
        p_c = jnp.exp2(sc - jnp.max(sc, axis=0, keepdims=True))
        finish(jnp.dot(vtc_ref[...], p_c.astype(BF16), preferred_element_type=F32),
               jnp.sum(p_c, axis=0, keepdims=True))


def _global_attention(q, k, vt, n_batch, seq, ctx_len):
    n, d = q.shape
    lat_q = seq // Q_TILE
    ctx_q = ctx_len // Q_TILE
    ctx_blk0 = (n_batch * seq) // ctx_len
    gw = KV_REP * HEAD_DIM

    def qrow(b, g, t):
        return jnp.where(t < lat_q, b * lat_q + t, n_batch * lat_q + b * ctx_q + (t - lat_q))

    return pl.pallas_call(
        functools.partial(_global_attn_kernel, lat_q),
        grid=(n_batch, N_KV_HEADS, lat_q + ctx_q),
        in_specs=[pl.BlockSpec((Q_TILE, gw), lambda b, g, t: (qrow(b, g, t), g)),
                  pl.BlockSpec((seq, HEAD_DIM), lambda b, g, t: (b, g)),
                  pl.BlockSpec((ctx_len, HEAD_DIM), lambda b, g, t: (ctx_blk0 + b, g)),
                  pl.BlockSpec((HEAD_DIM, seq), lambda b, g, t: (g, b)),
                  pl.BlockSpec((HEAD_DIM, ctx_len), lambda b, g, t: (g, ctx_blk0 + b))],
        out_specs=pl.BlockSpec((Q_TILE, gw), lambda b, g, t: (qrow(b, g, t), g)),
        out_shape=jax.ShapeDtypeStruct((n, d), BF16),
        compiler_params=_cparams(("arbitrary", "arbitrary", "arbitrary")),
    )(q, k, k, vt, vt)


def _window_attn_kernel(seq, q_ref, kl_ref, kc_ref, vl_ref, vc_ref, sink_ref, o_ref):
    qt = pl.program_id(2)
    span = Q_TILE + 2 * WINDOW
    start = pl.multiple_of(jnp.clip(qt * Q_TILE - WINDOW, 0, seq - span), Q_TILE)
    q4 = _stack_heads(q_ref)
    rows = KV_REP * Q_TILE
    s_w = _nt_dot(q4, kl_ref[pl.ds(start, span), :])
    s_c = _nt_dot(q4, kc_ref[...])
    qpos = qt * Q_TILE + lax.broadcasted_iota(jnp.int32, (rows, 1), 0) % Q_TILE
    kpos = start + lax.broadcasted_iota(jnp.int32, (1, span), 1)
    s_w = jnp.where(jnp.abs(qpos - kpos) <= WINDOW, s_w, NEG_INF)
    sink = sink_ref[0]
    m = jnp.maximum(jnp.maximum(jnp.max(s_w, axis=-1, keepdims=True),
                                jnp.max(s_c, axis=-1, keepdims=True)), sink)
    p_w = jnp.exp2(s_w - m)
    p_c = jnp.exp2(s_c - m)
    den = (jnp.sum(p_w, axis=-1, keepdims=True) + jnp.sum(p_c, axis=-1, keepdims=True)
           + jnp.exp2(sink - m))
    o = (jnp.dot(p_w.astype(BF16), vl_ref[pl.ds(start, span), :], preferred_element_type=F32)
         + jnp.dot(p_c.astype(BF16), vc_ref[...], preferred_element_type=F32))
    _unstack_heads(o * (1.0 / den), o_ref)


def _window_attention(q, k, v, sink_rows, n_batch, seq, ctx_len):
    n, d = q.shape
    lat_q = seq // Q_TILE
    ctx_blk0 = (n_batch * seq) // ctx_len
    gw = KV_REP * HEAD_DIM
    rows = KV_REP * Q_TILE
    return pl.pallas_call(
        functools.partial(_window_attn_kernel, seq),
        grid=(n_batch, N_KV_HEADS, lat_q),
        in_specs=[pl.BlockSpec((Q_TILE, gw), lambda b, g, t: (b * lat_q + t, g)),
                  pl.BlockSpec((seq, HEAD_DIM), lambda b, g, t: (b, g)),
                  pl.BlockSpec((ctx_len, HEAD_DIM), lambda b, g, t: (ctx_blk0 + b, g)),
                  pl.BlockSpec((seq, HEAD_DIM), lambda b, g, t: (b, g)),
                  pl.BlockSpec((ctx_len, HEAD_DIM), lambda b, g, t: (ctx_blk0 + b, g)),
                  pl.BlockSpec((1, rows, 1), lambda b, g, t: (g, 0, 0))],
        out_specs=pl.BlockSpec((Q_TILE, gw), lambda b, g, t: (b * lat_q + t, g)),
        out_shape=jax.ShapeDtypeStruct((n, d), BF16),
        compiler_params=_cparams(("arbitrary", "arbitrary", "arbitrary")),
    )(q, k, k, v, v, sink_rows)


def _peer_pre_kernel(o_ref, x_ref, wo_ref, g1_ref, n2_ref, sh_ref, sc_ref, wq_ref, keys_ref,
                     xn_ref, h2t_ref, st_ref):
    y = jnp.dot(o_ref[...], wo_ref[...], preferred_element_type=F32)
    xn = x_ref[...] + g1_ref[0] * y
    xn_ref[...] = xn
    h2 = _rms(xn) * n2_ref[...]
    h2 = h2 * (1.0 + sc_ref[0]) + sh_ref[0]
    h2t_ref[...] = h2.T.astype(BF16)
    qp = jnp.dot(h2.astype(BF16), wq_ref[...], preferred_element_type=F32)
    half = PEER_QDIM // 2
    for hs in range(2 * PEER_HEADS):
        qs = qp[:, hs * half:(hs + 1) * half].astype(BF16)
        st_ref[hs] = _nt_dot(keys_ref[hs % 2], qs)


def _peer_pre(o, xs, mods, layer, w_o, n2, w_q, keys, n_tiles, lat_tiles, tiles_per_seq, n_batch):
    n, d = xs.shape
    t = ROW_TILE
    qd = PEER_HEADS * PEER_QDIM

    def mrow(i):
        return layer * MOD_ROWS + jnp.where(i < lat_tiles, i // tiles_per_seq, n_batch)

    return pl.pallas_call(
        _peer_pre_kernel,
        grid=(n_tiles,),
        in_specs=[pl.BlockSpec((t, d), lambda i: (i, 0)),
                  pl.BlockSpec((t, d), lambda i: (i, 0)),
                  pl.BlockSpec((d, d), lambda i: (0, 0)),
                  pl.BlockSpec((1, 1, d), lambda i: (mrow(i), 0, 2)),
                  pl.BlockSpec((1, d), lambda i: (0, 0)),
                  pl.BlockSpec((1, 1, d), lambda i: (mrow(i), 0, 3)),
                  pl.BlockSpec((1, 1, d), lambda i: (mrow(i), 0, 4)),
                  pl.BlockSpec((d, qd), lambda i: (0, 0)),
                  pl.BlockSpec((2, N_KEYS, PEER_QDIM // 2), lambda i: (0, 0, 0))],
        out_specs=[pl.BlockSpec((t, d), lambda i: (i, 0)),
                   pl.BlockSpec((d, t), lambda i: (0, i)),
                   pl.BlockSpec((2 * PEER_HEADS, N_KEYS, t), lambda i: (0, 0, i))],
        out_shape=[jax.ShapeDtypeStruct((n, d), F32),
                   jax.ShapeDtypeStruct((d, n), BF16),
                   jax.ShapeDtypeStruct((2 * PEER_HEADS, N_KEYS, n), F32)],
        compiler_params=_cparams(("arbitrary",)),
    )(o, xs, w_o, mods, n2, mods, mods, w_q, keys)


def _top_values(s, with_rank=False):
    vals = []
    rank = jnp.full(s.shape, NOT_RANKED, F32) if with_rank else None
    for j in range(PEER_TOPK):
        m = jnp.max(s, axis=0, keepdims=True)
        vals.append(m)
        hit = s == m
        if with_rank:
            rank = jnp.where(hit, float(j), rank)
        s = jnp.where(hit, -jnp.inf, s)
    return vals, rank


def _candidate_sums(v1, v2):
    assert PEER_TOPK == 16
    a1 = jnp.concatenate(v1, axis=0)
    a2 = jnp.concatenate(v2, axis=0)
    row = lax.broadcasted_iota(jnp.int32, (8, a1.shape[1]), 0)
    ninf = -jnp.inf
    return jnp.concatenate([
        v1[0] + a2,
        v1[1] + a2[0:8],
        jnp.where(row < 5, v1[2] + a2[0:8], ninf),
        jnp.where(row < 4, v1[3] + a2[0:8], ninf),
        a1[8:16] + v2[0],
        jnp.where(row >= 4, a1[0:8] + v2[0], ninf),
        jnp.where(row >= 4, a1[0:8] + v2[1], ninf),
        jnp.where(row == 4, a1[0:8] + v2[2], ninf),
    ], axis=0), a1


def _peer_topk_kernel(st_ref, r2_ref, e2_ref, c_ref, e1_ref):
    def per_head(h, carry):
        s1 = st_ref[2 * h]
        s2 = st_ref[2 * h + 1]
        v1, _ = _top_values(s1)
        v2, rank2 = _top_values(s2, with_rank=True)
        cand, a1 = _candidate_sums(v1, v2)
        thr = _top_values(cand)[0][-1]
        top = v1[0] + v2[0]
        z = jnp.sum(jnp.where(cand >= thr, jnp.exp(cand - top), 0.0), axis=0, keepdims=True)
        cnt = jnp.zeros(a1.shape, F32)
        for j in range(PEER_TOPK):
            cnt = cnt + jnp.where(a1 + v2[j] >= thr, 1.0, 0.0)
        c = jnp.zeros(s1.shape, F32)
        for i in range(PEER_TOPK):
            c = jnp.where(s1 == v1[i], cnt[i:i + 1], c)
        r2_ref[h] = rank2.astype(BF16)
        e2_ref[h] = (jnp.exp(s2 - v2[0]) * (1.0 / z)).astype(BF16)
        c_ref[h] = c
        e1_ref[h] = jnp.exp(s1 - v1[0])
        return carry

    lax.fori_loop(0, PEER_HEADS, per_head, 0)


def _peer_topk(st, n_tiles):
    _, _, n = st.shape
    t = ROW_TILE
    spec = pl.BlockSpec((PEER_HEADS, N_KEYS, t), lambda i: (0, 0, i))
    narrow = jax.ShapeDtypeStruct((PEER_HEADS, N_KEYS, n), BF16)
    wide = jax.ShapeDtypeStruct((PEER_HEADS, N_KEYS, n), F32)
    return pl.pallas_call(
        _peer_topk_kernel,
        grid=(n_tiles,),
        in_specs=[pl.BlockSpec((2 * PEER_HEADS, N_KEYS, t), lambda i: (0, 0, i))],
        out_specs=[spec, spec, spec, spec],
        out_shape=[narrow, narrow, wide, wide],
        compiler_params=_cparams(("arbitrary",)),
    )(st)


def _gelu_tanh(x):
    k = -2.0 * math.sqrt(2.0 / math.pi)
    return x / (1.0 + jnp.exp(x * (k + (k * 0.044715) * (x * x))))


def _peer_dense_kernel(h2t_ref, r2_in_ref, e2_in_ref, c_ref, e1_ref, u_ref, vt_ref, xn_ref, g2_ref,
                       out_ref, acc_ref, act_ref, slab_ref, r2_ref, e2_ref):
    j = pl.program_id(1)
    lanes = acc_ref.shape[1]

    @pl.when(j == 0)
    def _():
        acc_ref[...] = jnp.zeros_like(acc_ref)
        r2_ref[...] = r2_in_ref[...]
        e2_ref[...] = e2_in_ref[...]

    for p in range(u_ref.shape[0] // PIECE_ROWS):
        r = p * PIECE_ROWS
        act_ref[p] = jnp.dot(u_ref[r:r + PIECE_ROWS, :], h2t_ref[...], preferred_element_type=F32)
        for a2 in range(PIECE_ROWS // N_KEYS):
            a = j * PEER_CHUNK + p * (PIECE_ROWS // N_KEYS) + a2
            for lc in range(lanes // SUB_LANES):
                cols = slice(lc * SUB_LANES, (lc + 1) * SUB_LANES)
                w = jnp.zeros((N_KEYS, SUB_LANES), BF16)
                for h in range(PEER_HEADS):
                    c = jnp.broadcast_to(c_ref[h, pl.ds(a, 1), :][:, cols],
                                         (N_KEYS, SUB_LANES)).astype(BF16)
                    e1 = jnp.broadcast_to(e1_ref[h, pl.ds(a, 1), :][:, cols],
                                          (N_KEYS, SUB_LANES)).astype(BF16)
                    w = w + jnp.where(r2_ref[h, :, cols] < c, e2_ref[h, :, cols],
                                      jnp.zeros((), BF16)) * e1
                x = act_ref[p, a2 * N_KEYS:(a2 + 1) * N_KEYS, cols]
                slab_ref[p, a2 * N_KEYS:(a2 + 1) * N_KEYS, cols] = w * _gelu_tanh(x).astype(BF16)
        acc_ref[...] += jnp.dot(vt_ref[:, r:r + PIECE_ROWS], slab_ref[p], preferred_element_type=F32)

    @pl.when(j == pl.num_programs(1) - 1)
    def _():
        out_ref[...] = xn_ref[...] + g2_ref[0] * acc_ref[...].T


def _peer_dense(h2t, coefs, u, vt, xn, mods, layer, n_tiles, lat_tiles, tiles_per_seq, n_batch):
    d = xn.shape[1]
    t = PEER_TILE
    ce = PEER_CHUNK * N_KEYS
    n_exp = u.shape[0]

    def mrow(i):
        return layer * MOD_ROWS + jnp.where(i < lat_tiles, i // tiles_per_seq, n_batch)

    cspec = pl.BlockSpec((PEER_HEADS, N_KEYS, t), lambda i, j: (0, 0, i))
    return pl.pallas_call(
        _peer_dense_kernel,
        grid=(n_tiles, n_exp // ce),
        in_specs=[pl.BlockSpec((d, t), lambda i, j: (0, i)),
                  cspec, cspec, cspec, cspec,
                  pl.BlockSpec((ce, d), lambda i, j: (j, 0)),
                  pl.BlockSpec((d, ce), lambda i, j: (0, j)),
                  pl.BlockSpec((t, d), lambda i, j: (i, 0)),
                  pl.BlockSpec((1, 1, d), lambda i, j: (mrow(i), 0, 5))],
        out_specs=pl.BlockSpec((t, d), lambda i, j: (i, 0)),
        out_shape=jax.ShapeDtypeStruct((n_tiles * t, d), F32),
        scratch_shapes=[pltpu.VMEM((d, t), F32),
                        pltpu.VMEM((ce // PIECE_ROWS, PIECE_ROWS, t), F32),
                        pltpu.VMEM((ce // PIECE_ROWS, PIECE_ROWS, t), BF16),
                        pltpu.VMEM((PEER_HEADS, N_KEYS, t), BF16),
                        pltpu.VMEM((PEER_HEADS, N_KEYS, t), BF16)],
        compiler_params=_cparams(("arbitrary", "arbitrary")),
    )(h2t, *coefs, u, vt, xn, mods)


def _rope_tables(seq, ctx_len):
    pos = jnp.arange(seq)
    row = (pos // GRID_W).astype(F32)
    col = (pos % GRID_W).astype(F32)
    half = HEAD_DIM // 2
    inv = ROPE_THETA ** (-jnp.arange(0, half, 2, dtype=F32) / half)
    ar = row[:, None] * inv
    ac = col[:, None] * inv
    cos = jnp.concatenate([jnp.cos(ar), jnp.cos(ar), jnp.cos(ac), jnp.cos(ac)], axis=-1)
    sin = jnp.concatenate([-jnp.sin(ar), jnp.sin(ar), -jnp.sin(ac), jnp.sin(ac)], axis=-1)
    cos = jnp.concatenate([cos, jnp.ones((ctx_len, HEAD_DIM), F32)], axis=0)
    sin = jnp.concatenate([sin, jnp.zeros((ctx_len, HEAD_DIM), F32)], axis=0)
    return cos, sin


def kernel(x, c, ctx, c_ctx, ada_w, ada_b, norm1_gain, norm2_gain, w_qkv, q_norm_gain, k_norm_gain,
           w_o, attn_sinks, peer_w_q, peer_sub_keys, peer_u, peer_v):
    n_batch, seq, d = x.shape
    ctx_len = ctx.shape[1]
    depth = ada_w.shape[0]
    n_lat = n_batch * seq
    n_ctx = n_batch * ctx_len
    assert depth == 2 and d == D_MODEL and n_batch < MOD_ROWS
    assert seq % PEER_TILE == 0 and n_ctx % PEER_TILE == 0 and ctx_len == ROW_TILE
    assert seq >= Q_TILE + 2 * WINDOW and n_lat % ctx_len == 0

    cc = jnp.concatenate([c, c_ctx[None, :], jnp.zeros((MOD_ROWS - n_batch - 1, d), F32)], axis=0)
    mods = _modulation(cc, ada_w, ada_b).reshape(depth * MOD_ROWS, 1, N_MOD * d)
    cos_t, sin_t = _rope_tables(seq, ctx_len)
    xs = jnp.concatenate([x.reshape(n_lat, d), ctx.reshape(n_ctx, d)], axis=0)
    scale = HEAD_DIM ** -0.5 * math.log2(math.e)

    for layer in range(depth):
        last = layer == depth - 1
        q, k, v, vt = _qkv(xs, mods, layer, norm1_gain[layer][None, :], w_qkv[layer].astype(BF16),
                       (q_norm_gain[layer] * scale)[None, :], k_norm_gain[layer][None, :],
                       cos_t, sin_t, n_lat // ROW_TILE, seq // ROW_TILE, n_batch)
        if layer % 2 == 0:
            o = _global_attention(q, k, vt, n_batch, seq, ctx_len)
        else:
            sink = (attn_sinks[layer // 2] * math.log2(math.e)).reshape(N_KV_HEADS, KV_REP, 1, 1)
            sink_rows = jnp.broadcast_to(sink, (N_KV_HEADS, KV_REP, Q_TILE, 1))
            o = _window_attention(q, k, v, sink_rows.reshape(N_KV_HEADS, KV_REP * Q_TILE, 1),
                                  n_batch, seq, ctx_len)
        n_rows = n_lat if last else n_lat + n_ctx
        xn, h2t, st = _peer_pre(o, xs, mods, layer, w_o[layer].astype(BF16),
                                norm2_gain[layer][None, :], peer_w_q[layer].astype(BF16),
                                peer_sub_keys[layer].astype(BF16), n_rows // ROW_TILE,
                                n_lat // ROW_TILE, seq // ROW_TILE, n_batch)
        coefs = _peer_topk(st, n_rows // ROW_TILE)
        xs = _peer_dense(h2t, coefs, peer_u[layer].astype(BF16), peer_v[layer].T.astype(BF16),
                         xn, mods, layer, n_rows // PEER_TILE, n_lat // PEER_TILE,
                         seq // PEER_TILE, n_batch)
    return xs[:n_lat].reshape(n_batch, seq, d)
```

```python
import functools
import math

import jax
import jax.numpy as jnp
from jax import lax
from jax.experimental import pallas as pl
from jax.experimental.pallas import tpu as pltpu

F32 = jnp.float32
BF16 = jnp.bfloat16

D_MODEL = 1024
N_HEADS = 8
N_KV_HEADS = 2
HEAD_DIM = 128
KV_REP = N_HEADS // N_KV_HEADS
QKV_DIM = (N_HEADS + 2 * N_KV_HEADS) * HEAD_DIM
GRID_W = 64
WINDOW = 128
ROPE_THETA = 10000.0
RMS_EPS = 1e-6
NEG_INF = -1e30
PEER_HEADS = 8
PEER_TOPK = 16
N_KEYS = 128
PEER_QDIM = 256
N_MOD = 6
MOD_ROWS = 8

ROW_TILE = 256
Q_TILE = 128
PEER_TILE = 512
PEER_CHUNK = 8
SUB_LANES = 128
PIECE_ROWS = 256
NOT_RANKED = 64.0
VMEM_LIMIT = 56 * 1024 * 1024


def _cparams(sem):
    return pltpu.CompilerParams(dimension_semantics=sem, vmem_limit_bytes=VMEM_LIMIT)


def _nt_dot(a, b):
    return lax.dot_general(a, b, (((1,), (1,)), ((), ())), preferred_element_type=F32)


def _mod_kernel(c_ref, w_ref, b_ref, o_ref):
    c = c_ref[...]
    a = c * (1.0 / (1.0 + jnp.exp(-c)))
    o_ref[0] = jnp.dot(a, w_ref[0], preferred_element_type=F32,
                       precision=lax.Precision.HIGHEST) + b_ref[0]


def _modulation(cc, ada_w, ada_b):
    depth, d, n = ada_w.shape
    tn = 1536
    return pl.pallas_call(
        _mod_kernel,
        grid=(depth, n // tn),
        in_specs=[pl.BlockSpec((MOD_ROWS, d), lambda l, j: (0, 0)),
                  pl.BlockSpec((1, d, tn), lambda l, j: (l, 0, j)),
                  pl.BlockSpec((1, 1, tn), lambda l, j: (l, 0, j))],
        out_specs=pl.BlockSpec((1, MOD_ROWS, tn), lambda l, j: (l, 0, j)),
        out_shape=jax.ShapeDtypeStruct((depth, MOD_ROWS, n), F32),
        compiler_params=_cparams(("arbitrary", "arbitrary")),
    )(cc, ada_w, ada_b.reshape(depth, 1, n))


def _rms(x):
    return x * lax.rsqrt(jnp.mean(x * x, axis=-1, keepdims=True) + RMS_EPS)


def _qkv_kernel(x_ref, sh_ref, sc_ref, g_ref, w_ref, qg_ref, kg_ref, cos_ref, sin_ref,
                q_ref, k_ref, vt_ref):
    h = _rms(x_ref[...]) * g_ref[...]
    h = h * (1.0 + sc_ref[0]) + sh_ref[0]
    y = jnp.dot(h.astype(BF16), w_ref[...], preferred_element_type=F32)
    cos = cos_ref[...]
    sin = sin_ref[...]
    lane = lax.broadcasted_iota(jnp.int32, cos.shape, 1)
    first = (lane % (HEAD_DIM // 2)) < (HEAD_DIM // 4)

    def head(col, gain):
        n = _rms(y[:, col * HEAD_DIM:(col + 1) * HEAD_DIM]) * gain
        sw = jnp.where(first, pltpu.roll(n, HEAD_DIM - HEAD_DIM // 4, 1),
                       pltpu.roll(n, HEAD_DIM // 4, 1))
        return n * cos + sw * sin

    for hh in range(N_HEADS):
        q_ref[:, hh * HEAD_DIM:(hh + 1) * HEAD_DIM] = head(hh, qg_ref[...]).astype(BF16)
    for g in range(N_KV_HEADS):
        k_ref[:, g * HEAD_DIM:(g + 1) * HEAD_DIM] = head(N_HEADS + g, kg_ref[...]).astype(BF16)
    v0 = (N_HEADS + N_KV_HEADS) * HEAD_DIM
    vt_ref[...] = y[:, v0:].T.astype(BF16)


def _qkv(xs, mods, layer, gain, w, qg, kg, cos_t, sin_t, lat_tiles, tiles_per_seq, n_batch):
    n, d = xs.shape
    t = ROW_TILE

    def mrow(i):
        return layer * MOD_ROWS + jnp.where(i < lat_tiles, i // tiles_per_seq, n_batch)

    def trow(i):
        return jnp.where(i < lat_tiles, i % tiles_per_seq, tiles_per_seq)

    kvd = N_KV_HEADS * HEAD_DIM
    return pl.pallas_call(
        _qkv_kernel,
        grid=(n // t,),
        in_specs=[pl.BlockSpec((t, d), lambda i: (i, 0)),
                  pl.BlockSpec((1, 1, d), lambda i: (mrow(i), 0, 0)),
                  pl.BlockSpec((1, 1, d), lambda i: (mrow(i), 0, 1)),
                  pl.BlockSpec((1, d), lambda i: (0, 0)),
                  pl.BlockSpec((d, QKV_DIM), lambda i: (0, 0)),
                  pl.BlockSpec((1, HEAD_DIM), lambda i: (0, 0)),
                  pl.BlockSpec((1, HEAD_DIM), lambda i: (0, 0)),
                  pl.BlockSpec((t, HEAD_DIM), lambda i: (trow(i), 0)),
                  pl.BlockSpec((t, HEAD_DIM), lambda i: (trow(i), 0))],
        out_specs=[pl.BlockSpec((t, d), lambda i: (i, 0)),
                   pl.BlockSpec((t, kvd), lambda i: (i, 0)),
                   pl.BlockSpec((kvd, t), lambda i: (0, i))],
        out_shape=[jax.ShapeDtypeStruct((n, d), BF16),
                   jax.ShapeDtypeStruct((n, kvd), BF16),
                   jax.ShapeDtypeStruct((kvd, n), BF16)],
        compiler_params=_cparams(("arbitrary",)),
    )(xs, mods, mods, gain, w, qg, kg, cos_t, sin_t)


def _stack_heads(q_ref):
    return jnp.concatenate(
        [q_ref[:, r * HEAD_DIM:(r + 1) * HEAD_DIM] for r in range(KV_REP)], axis=0)


def _store_heads(ot, o_ref):
    for r in range(KV_REP):
        blk = ot[:, r * Q_TILE:(r + 1) * Q_TILE]
        o_ref[:, r * HEAD_DIM:(r + 1) * HEAD_DIM] = blk.T.astype(BF16)


def _global_attn_kernel(lat_q_tiles, q_ref, kl_ref, kc_ref, vtl_ref, vtc_ref, o_ref):
    qt = pl.program_id(2)
    q4 = _stack_heads(q_ref)
    sc = _nt_dot(kc_ref[...], q4)

    def finish(ot, den):
        _store_heads(ot * (1.0 / den), o_ref)

    @pl.when(qt < lat_q_tiles)
    def _():
        sl = _nt_dot(kl_ref[...], q4)
        m = jnp.maximum(jnp.max(sl, axis=0, keepdims=True), jnp.max(sc, axis=0, keepdims=True))
        p_l = jnp.exp2(sl - m)
        p_c = jnp.exp2(sc - m)
        den = jnp.sum(p_l, axis=0, keepdims=True) + jnp.sum(p_c, axis=0, keepdims=True)
        finish(jnp.dot(vtl_ref[...], p_l.astype(BF16), preferred_element_type=F32)
               + jnp.dot(vtc_ref[...], p_c.astype(BF16), preferred_element_type=F32), den)

    @pl.when(qt >= lat_q_tiles)
    def _():
        p_c = jnp.exp2(sc - jnp.max(sc, axis=0, keepdims=True))
        finish(jnp.dot(vtc_ref[...], p_c.astype(BF16), preferred_element_type=F32),
               jnp.sum(p_c, axis=0, keepdims=True))


def _global_attention(q, k, vt, n_batch, seq, ctx_len):
    n, d = q.shape
    lat_q = seq // Q_TILE
    ctx_q = ctx_len // Q_TILE
    ctx_blk0 = (n_batch * seq) // ctx_len
    gw = KV_REP * HEAD_DIM

    def qrow(b, g, t):
        return jnp.where(t < lat_q, b * lat_q + t, n_batch * lat_q + b * ctx_q + (t - lat_q))

    return pl.pallas_call(
        functools.partial(_global_attn_kernel, lat_q),
        grid=(n_batch, N_KV_HEADS, lat_q + ctx_q),
        in_specs=[pl.BlockSpec((Q_TILE, gw), lambda b, g, t: (qrow(b, g, t), g)),
                  pl.BlockSpec((seq, HEAD_DIM), lambda b, g, t: (b, g)),
                  pl.BlockSpec((ctx_len, HEAD_DIM), lambda b, g, t: (ctx_blk0 + b, g)),
                  pl.BlockSpec((HEAD_DIM, seq), lambda b, g, t: (g, b)),
                  pl.BlockSpec((HEAD_DIM, ctx_len), lambda b, g, t: (g, ctx_blk0 + b))],
        out_specs=pl.BlockSpec((Q_TILE, gw), lambda b, g, t: (qrow(b, g, t), g)),
        out_shape=jax.ShapeDtypeStruct((n, d), BF16),
        compiler_params=_cparams(("arbitrary", "arbitrary", "arbitrary")),
    )(q, k, k, vt, vt)


def _window_attn_kernel(seq, q_ref, kl_ref, kc_ref, vtl_ref, vtc_ref, sink_ref, o_ref):
    qt = pl.program_id(2)
    span = Q_TILE + 2 * WINDOW
    start = pl.multiple_of(jnp.clip(qt * Q_TILE - WINDOW, 0, seq - span), Q_TILE)
    q4 = _stack_heads(q_ref)
    cols = KV_REP * Q_TILE
    s_w = _nt_dot(kl_ref[pl.ds(start, span), :], q4)
    s_c = _nt_dot(kc_ref[...], q4)
    kpos = start + lax.broadcasted_iota(jnp.int32, (span, 1), 0)
    qpos = qt * Q_TILE + lax.broadcasted_iota(jnp.int32, (1, cols), 1) % Q_TILE
    s_w = jnp.where(jnp.abs(qpos - kpos) <= WINDOW, s_w, NEG_INF)
    sink = sink_ref[0]
    m = jnp.maximum(jnp.maximum(jnp.max(s_w, axis=0, keepdims=True),
                                jnp.max(s_c, axis=0, keepdims=True)), sink)
    p_w = jnp.exp2(s_w - m)
    p_c = jnp.exp2(s_c - m)
    den = (jnp.sum(p_w, axis=0, keepdims=True) + jnp.sum(p_c, axis=0, keepdims=True)
           + jnp.exp2(sink - m))
    ot = (jnp.dot(vtl_ref[:, pl.ds(start, span)], p_w.astype(BF16), preferred_element_type=F32)
          + jnp.dot(vtc_ref[...], p_c.astype(BF16), preferred_element_type=F32))
    _store_heads(ot * (1.0 / den), o_ref)


def _window_attention(q, k, vt, sink_cols, n_batch, seq, ctx_len):
    n, d = q.shape
    lat_q = seq // Q_TILE
    ctx_blk0 = (n_batch * seq) // ctx_len
    gw = KV_REP * HEAD_DIM
    return pl.pallas_call(
        functools.partial(_window_attn_kernel, seq),
        grid=(n_batch, N_KV_HEADS, lat_q),
        in_specs=[pl.BlockSpec((Q_TILE, gw), lambda b, g, t: (b * lat_q + t, g)),
                  pl.BlockSpec((seq, HEAD_DIM), lambda b, g, t: (b, g)),
                  pl.BlockSpec((ctx_len, HEAD_DIM), lambda b, g, t: (ctx_blk0 + b, g)),
                  pl.BlockSpec((HEAD_DIM, seq), lambda b, g, t: (g, b)),
                  pl.BlockSpec((HEAD_DIM, ctx_len), lambda b, g, t: (g, ctx_blk0 + b)),
                  pl.BlockSpec((1, 1, KV_REP * Q_TILE), lambda b, g, t: (g, 0, 0))],
        out_specs=pl.BlockSpec((Q_TILE, gw), lambda b, g, t: (b * lat_q + t, g)),
        out_shape=jax.ShapeDtypeStruct((n, d), BF16),
        compiler_params=_cparams(("arbitrary", "arbitrary", "arbitrary")),
    )(q, k, k, vt, vt, sink_cols)


def _peer_pre_kernel(o_ref, x_ref, wo_ref, g1_ref, n2_ref, sh_ref, sc_ref, wq_ref, keys_ref,
                     xn_ref, h2t_ref, st_ref):
    y = jnp.dot(o_ref[...], wo_ref[...], preferred_element_type=F32)
    xn = x_ref[...] + g1_ref[0] * y
    xn_ref[...] = xn
    h2 = _rms(xn) * n2_ref[...]
    h2 = h2 * (1.0 + sc_ref[0]) + sh_ref[0]
    h2t_ref[...] = h2.T.astype(BF16)
    qp = jnp.dot(h2.astype(BF16), wq_ref[...], preferred_element_type=F32)
    half = PEER_QDIM // 2
    for hs in range(2 * PEER_HEADS):
        qs = qp[:, hs * half:(hs + 1) * half].astype(BF16)
        st_ref[hs] = _nt_dot(keys_ref[hs % 2], qs)


def _peer_pre(o, xs, mods, layer, w_o, n2, w_q, keys, n_tiles, lat_tiles, tiles_per_seq, n_batch):
    n, d = xs.shape
    t = ROW_TILE
    qd = PEER_HEADS * PEER_QDIM

    def mrow(i):
        return layer * MOD_ROWS + jnp.where(i < lat_tiles, i // tiles_per_seq, n_batch)

    return pl.pallas_call(
        _peer_pre_kernel,
        grid=(n_tiles,),
        in_specs=[pl.BlockSpec((t, d), lambda i: (i, 0)),
                  pl.BlockSpec((t, d), lambda i: (i, 0)),
                  pl.BlockSpec((d, d), lambda i: (0, 0)),
                  pl.BlockSpec((1, 1, d), lambda i: (mrow(i), 0, 2)),
                  pl.BlockSpec((1, d), lambda i: (0, 0)),
                  pl.BlockSpec((1, 1, d), lambda i: (mrow(i), 0, 3)),
                  pl.BlockSpec((1, 1, d), lambda i: (mrow(i), 0, 4)),
                  pl.BlockSpec((d, qd), lambda i: (0, 0)),
                  pl.BlockSpec((2, N_KEYS, PEER_QDIM // 2), lambda i: (0, 0, 0))],
        out_specs=[pl.BlockSpec((t, d), lambda i: (i, 0)),
                   pl.BlockSpec((d, t), lambda i: (0, i)),
                   pl.BlockSpec((2 * PEER_HEADS, N_KEYS, t), lambda i: (0, 0, i))],
        out_shape=[jax.ShapeDtypeStruct((n, d), F32),
                   jax.ShapeDtypeStruct((d, n), BF16),
                   jax.ShapeDtypeStruct((2 * PEER_HEADS, N_KEYS, n), F32)],
        compiler_params=_cparams(("arbitrary",)),
    )(o, xs, w_o, mods, n2, mods, mods, w_q, keys)


def _top_values(s, with_rank=False):
    vals = []
    rank = jnp.full(s.shape, NOT_RANKED, F32) if with_rank else None
    for j in range(PEER_TOPK):
        m = jnp.max(s, axis=0, keepdims=True)
        vals.append(m)
        hit = s == m
        if with_rank:
            rank = jnp.where(hit, float(j), rank)
        s = jnp.where(hit, -jnp.inf, s)
    return vals, rank


def _candidate_sums(v1, v2):
    assert PEER_TOPK == 16
    a1 = jnp.concatenate(v1, axis=0)
    a2 = jnp.concatenate(v2, axis=0)
    row = lax.broadcasted_iota(jnp.int32, (8, a1.shape[1]), 0)
    ninf = -jnp.inf
    return jnp.concatenate([
        v1[0] + a2,
        v1[1] + a2[0:8],
        jnp.where(row < 5, v1[2] + a2[0:8], ninf),
        jnp.where(row < 4, v1[3] + a2[0:8], ninf),
        a1[8:16] + v2[0],
        jnp.where(row >= 4, a1[0:8] + v2[0], ninf),
        jnp.where(row >= 4, a1[0:8] + v2[1], ninf),
        jnp.where(row == 4, a1[0:8] + v2[2], ninf),
    ], axis=0), a1


def _peer_topk_kernel(st_ref, r2_ref, e2_ref, c_ref, e1_ref):
    def per_head(h, carry):
        s1 = st_ref[2 * h]
        s2 = st_ref[2 * h + 1]
        v1, _ = _top_values(s1)
        v2, rank2 = _top_values(s2, with_rank=True)
        cand, a1 = _candidate_sums(v1, v2)
        thr = _top_values(cand)[0][-1]
        top = v1[0] + v2[0]
        z = jnp.sum(jnp.where(cand >= thr, jnp.exp(cand - top), 0.0), axis=0, keepdims=True)
        cnt = jnp.zeros(a1.shape, F32)
        for j in range(PEER_TOPK):
            cnt = cnt + jnp.where(a1 + v2[j] >= thr, 1.0, 0.0)
        c = jnp.zeros(s1.shape, F32)
        for i in range(PEER_TOPK):
            c = jnp.where(s1 == v1[i], cnt[i:i + 1], c)
        r2_ref[h] = rank2.astype(BF16)
        e2_ref[h] = (jnp.exp(s2 - v2[0]) * (1.0 / z)).astype(BF16)
        c_ref[h] = c
        e1_ref[h] = jnp.exp(s1 - v1[0])
        return carry

    lax.fori_loop(0, PEER_HEADS, per_head, 0)


def _peer_topk(st, n_tiles):
    _, _, n = st.shape
    t = ROW_TILE
    spec = pl.BlockSpec((PEER_HEADS, N_KEYS, t), lambda i: (0, 0, i))
    narrow = jax.ShapeDtypeStruct((PEER_HEADS, N_KEYS, n), BF16)
    wide = jax.ShapeDtypeStruct((PEER_HEADS, N_KEYS, n), F32)
    return pl.pallas_call(
        _peer_topk_kernel,
        grid=(n_tiles,),
        in_specs=[pl.BlockSpec((2 * PEER_HEADS, N_KEYS, t), lambda i: (0, 0, i))],
        out_specs=[spec, spec, spec, spec],
        out_shape=[narrow, narrow, wide, wide],
        compiler_params=_cparams(("arbitrary",)),
    )(st)


def _gelu_tanh(x):
    k = -2.0 * math.sqrt(2.0 / math.pi)
    return x / (1.0 + jnp.exp(x * (k + (k * 0.044715) * (x * x))))


def _peer_dense_kernel(h2t_ref, r2_in_ref, e2_in_ref, c_ref, e1_ref, u_ref, vt_ref, xn_ref, g2_ref,
                       out_ref, acc_ref, act_ref, slab_ref, r2_ref, e2_ref):
    j = pl.program_id(1)
    lanes = acc_ref.shape[1]

    @pl.when(j == 0)
    def _():
        acc_ref[...] = jnp.zeros_like(acc_ref)
        r2_ref[...] = r2_in_ref[...]
        e2_ref[...] = e2_in_ref[...]

    for p in range(u_ref.shape[0] // PIECE_ROWS):
        r = p * PIECE_ROWS
        act_ref[p] = jnp.dot(u_ref[r:r + PIECE_ROWS, :], h2t_ref[...], preferred_element_type=F32)
        for a2 in range(PIECE_ROWS // N_KEYS):
            a = j * PEER_CHUNK + p * (PIECE_ROWS // N_KEYS) + a2
            for lc in range(lanes // SUB_LANES):
                cols = slice(lc * SUB_LANES, (lc + 1) * SUB_LANES)
                w = jnp.zeros((N_KEYS, SUB_LANES), BF16)
                for h in range(PEER_HEADS):
                    c = jnp.broadcast_to(c_ref[h, pl.ds(a, 1), :][:, cols],
                                         (N_KEYS, SUB_LANES)).astype(BF16)
                    e1 = jnp.broadcast_to(e1_ref[h, pl.ds(a, 1), :][:, cols],
                                          (N_KEYS, SUB_LANES)).astype(BF16)
                    w = w + jnp.where(r2_ref[h, :, cols] < c, e2_ref[h, :, cols],
                                      jnp.zeros((), BF16)) * e1
                x = act_ref[p, a2 * N_KEYS:(a2 + 1) * N_KEYS, cols]
                slab_ref[p, a2 * N_KEYS:(a2 + 1) * N_KEYS, cols] = w * _gelu_tanh(x).astype(BF16)
        acc_ref[...] += jnp.dot(vt_ref[:, r:r + PIECE_ROWS], slab_ref[p], preferred_element_type=F32)

    @pl.when(j == pl.num_programs(1) - 1)
    def _():
        out_ref[...] = xn_ref[...] + g2_ref[0] * acc_ref[...].T


def _peer_dense(h2t, coefs, u, vt, xn, mods, layer, n_tiles, lat_tiles, tiles_per_seq, n_batch):
    d = xn.shape[1]
    t = PEER_TILE
    ce = PEER_CHUNK * N_KEYS
    n_exp = u.shape[0]

    def mrow(i):
        return layer * MOD_ROWS + jnp.where(i < lat_tiles, i // tiles_per_seq, n_batch)

    cspec = pl.BlockSpec((PEER_HEADS, N_KEYS, t), lambda i, j: (0, 0, i))
    return pl.pallas_call(
        _peer_dense_kernel,
        grid=(n_tiles, n_exp // ce),
        in_specs=[pl.BlockSpec((d, t), lambda i, j: (0, i)),
                  cspec, cspec, cspec, cspec,
                  pl.BlockSpec((ce, d), lambda i, j: (j, 0)),
                  pl.BlockSpec((d, ce), lambda i, j: (0, j)),
                  pl.BlockSpec((t, d), lambda i, j: (i, 0)),
                  pl.BlockSpec((1, 1, d), lambda i, j: (mrow(i), 0, 5))],
        out_specs=pl.BlockSpec((t, d), lambda i, j: (i, 0)),
        out_shape=jax.ShapeDtypeStruct((n_tiles * t, d), F32),
        scratch_shapes=[pltpu.VMEM((d, t), F32),
                        pltpu.VMEM((ce // PIECE_ROWS, PIECE_ROWS, t), F32),
                        pltpu.VMEM((ce // PIECE_ROWS, PIECE_ROWS, t), BF16),
                        pltpu.VMEM((PEER_HEADS, N_KEYS, t), BF16),
                        pltpu.VMEM((PEER_HEADS, N_KEYS, t), BF16)],
        compiler_params=_cparams(("arbitrary", "arbitrary")),
    )(h2t, *coefs, u, vt, xn, mods)


def _rope_tables(seq, ctx_len):
    pos = jnp.arange(seq)
    row = (pos // GRID_W).astype(F32)
    col = (pos % GRID_W).astype(F32)
    half = HEAD_DIM // 2
    inv = ROPE_THETA ** (-jnp.arange(0, half, 2, dtype=F32) / half)
    ar = row[:, None] * inv
    ac = col[:, None] * inv
    cos = jnp.concatenate([jnp.cos(ar), jnp.cos(ar), jnp.cos(ac), jnp.cos(ac)], axis=-1)
    sin = jnp.concatenate([-jnp.sin(ar), jnp.sin(ar), -jnp.sin(ac), jnp.sin(ac)], axis=-1)
    cos = jnp.concatenate([cos, jnp.ones((ctx_len, HEAD_DIM), F32)], axis=0)
    sin = jnp.concatenate([sin, jnp.zeros((ctx_len, HEAD_DIM), F32)], axis=0)
    return cos, sin


def kernel(x, c, ctx, c_ctx, ada_w, ada_b, norm1_gain, norm2_gain, w_qkv, q_norm_gain, k_norm_gain,
           w_o, attn_sinks, peer_w_q, peer_sub_keys, peer_u, peer_v):
    n_batch, seq, d = x.shape
    ctx_len = ctx.shape[1]
    depth = ada_w.shape[0]
    n_lat = n_batch * seq
    n_ctx = n_batch * ctx_len
    assert depth == 2 and d == D_MODEL and n_batch < MOD_ROWS
    assert seq % PEER_TILE == 0 and n_ctx % PEER_TILE == 0 and ctx_len == ROW_TILE
    assert seq >= Q_TILE + 2 * WINDOW and n_lat % ctx_len == 0

    cc = jnp.concatenate([c, c_ctx[None, :], jnp.zeros((MOD_ROWS - n_batch - 1, d), F32)], axis=0)
    mods = _modulation(cc, ada_w, ada_b).reshape(depth * MOD_ROWS, 1, N_MOD * d)
    cos_t, sin_t = _rope_tables(seq, ctx_len)
    xs = jnp.concatenate([x.reshape(n_lat, d), ctx.reshape(n_ctx, d)], axis=0)
    scale = HEAD_DIM ** -0.5 * math.log2(math.e)

    for layer in range(depth):
        last = layer == depth - 1
        q, k, vt = _qkv(xs, mods, layer, norm1_gain[layer][None, :], w_qkv[layer].astype(BF16),
                       (q_norm_gain[layer] * scale)[None, :], k_norm_gain[layer][None, :],
                       cos_t, sin_t, n_lat // ROW_TILE, seq // ROW_TILE, n_batch)
        if layer % 2 == 0:
            o = _global_attention(q, k, vt, n_batch, seq, ctx_len)
        else:
            sink = (attn_sinks[layer // 2] * math.log2(math.e)).reshape(N_KV_HEADS, 1, KV_REP, 1)
            sink_cols = jnp.broadcast_to(sink, (N_KV_HEADS, 1, KV_REP, Q_TILE))
            o = _window_attention(q, k, vt, sink_cols.reshape(N_KV_HEADS, 1, KV_REP * Q_TILE),
                                  n_batch, seq, ctx_len)
        n_rows = n_lat if last else n_lat + n_ctx
        xn, h2t, st = _peer_pre(o, xs, mods, layer, w_o[layer].astype(BF16),
                                norm2_gain[layer][None, :], peer_w_q[layer].astype(BF16),
                                peer_sub_keys[layer].astype(BF16), n_rows // ROW_TILE,
                                n_lat // ROW_TILE, seq // ROW_TILE, n_batch)
        coefs = _peer_topk(st, n_rows // ROW_TILE)
        xs = _peer_dense(h2t, coefs, peer_u[layer].astype(BF16), peer_v[layer].T.astype(BF16),
                         xn, mods, layer, n_rows // PEER_TILE, n_lat // PEER_TILE,
                         seq // PEER_TILE, n_batch)
    return xs[:n_lat].reshape(n_batch, seq, d)
```

```python
import functools
import math

import jax
import jax.numpy as jnp
from jax import lax
from jax.experimental import pallas as pl
from jax.experimental.pallas import tpu as pltpu

F32 = jnp.float32
BF16 = jnp.bfloat16

D_MODEL = 1024
N_HEADS = 8
N_KV_HEADS = 2
HEAD_DIM = 128
KV_REP = N_HEADS // N_KV_HEADS
QKV_DIM = (N_HEADS + 2 * N_KV_HEADS) * HEAD_DIM
GRID_W = 64
WINDOW = 128
ROPE_THETA = 10000.0
RMS_EPS = 1e-6
NEG_INF = -1e30
PEER_HEADS = 8
PEER_TOPK = 16
N_KEYS = 128
PEER_QDIM = 256
N_MOD = 6
MOD_ROWS = 8

ROW_TILE = 256
Q_TILE = 128
PEER_TILE = 512
PEER_CHUNK = 8
SUB_LANES = 128
PIECE_ROWS = 256
SUBLANES = 8
THRESHOLD_SLACK = 2.0 ** -22
VMEM_LIMIT = 56 * 1024 * 1024


def _cparams(sem):
    return pltpu.CompilerParams(dimension_semantics=sem, vmem_limit_bytes=VMEM_LIMIT)


def _nt_dot(a, b):
    return lax.dot_general(a, b, (((1,), (1,)), ((), ())), preferred_element_type=F32)


def _mod_kernel(c_ref, w_ref, b_ref, o_ref):
    c = c_ref[...]
    a = c * (1.0 / (1.0 + jnp.exp(-c)))
    o_ref[0] = jnp.dot(a, w_ref[0], preferred_element_type=F32,
                       precision=lax.Precision.HIGHEST) + b_ref[0]


def _modulation(cc, ada_w, ada_b):
    depth, d, n = ada_w.shape
    tn = 1536
    return pl.pallas_call(
        _mod_kernel,
        grid=(depth, n // tn),
        in_specs=[pl.BlockSpec((MOD_ROWS, d), lambda l, j: (0, 0)),
                  pl.BlockSpec((1, d, tn), lambda l, j: (l, 0, j)),
                  pl.BlockSpec((1, 1, tn), lambda l, j: (l, 0, j))],
        out_specs=pl.BlockSpec((1, MOD_ROWS, tn), lambda l, j: (l, 0, j)),
        out_shape=jax.ShapeDtypeStruct((depth, MOD_ROWS, n), F32),
        compiler_params=_cparams(("arbitrary", "arbitrary")),
    )(cc, ada_w, ada_b.reshape(depth, 1, n))


def _rms(x):
    return x * lax.rsqrt(jnp.mean(x * x, axis=-1, keepdims=True) + RMS_EPS)


def _qkv_kernel(x_ref, sh_ref, sc_ref, g_ref, w_ref, qg_ref, kg_ref, cos_ref, sin_ref,
                q_ref, k_ref, vt_ref):
    h = _rms(x_ref[...]) * g_ref[...]
    h = h * (1.0 + sc_ref[0]) + sh_ref[0]
    y = jnp.dot(h.astype(BF16), w_ref[...], preferred_element_type=F32)
    cos = cos_ref[...]
    sin = sin_ref[...]
    lane = lax.broadcasted_iota(jnp.int32, cos.shape, 1)
    first = (lane % (HEAD_DIM // 2)) < (HEAD_DIM // 4)

    def head(col, gain):
        n = _rms(y[:, col * HEAD_DIM:(col + 1) * HEAD_DIM]) * gain
        sw = jnp.where(first, pltpu.roll(n, HEAD_DIM - HEAD_DIM // 4, 1),
                       pltpu.roll(n, HEAD_DIM // 4, 1))
        return n * cos + sw * sin

    for hh in range(N_HEADS):
        q_ref[:, hh * HEAD_DIM:(hh + 1) * HEAD_DIM] = head(hh, qg_ref[...]).astype(BF16)
    for g in range(N_KV_HEADS):
        k_ref[:, g * HEAD_DIM:(g + 1) * HEAD_DIM] = head(N_HEADS + g, kg_ref[...]).astype(BF16)
    v0 = (N_HEADS + N_KV_HEADS) * HEAD_DIM
    vt_ref[...] = y[:, v0:].T.astype(BF16)


def _qkv(xs, mods, layer, gain, w, qg, kg, cos_t, sin_t, lat_tiles, tiles_per_seq, n_batch):
    n, d = xs.shape
    t = ROW_TILE

    def mrow(i):
        return layer * MOD_ROWS + jnp.where(i < lat_tiles, i // tiles_per_seq, n_batch)

    def trow(i):
        return jnp.where(i < lat_tiles, i % tiles_per_seq, tiles_per_seq)

    kvd = N_KV_HEADS * HEAD_DIM
    return pl.pallas_call(
        _qkv_kernel,
        grid=(n // t,),
        in_specs=[pl.BlockSpec((t, d), lambda i: (i, 0)),
                  pl.BlockSpec((1, 1, d), lambda i: (mrow(i), 0, 0)),
                  pl.BlockSpec((1, 1, d), lambda i: (mrow(i), 0, 1)),
                  pl.BlockSpec((1, d), lambda i: (0, 0)),
                  pl.BlockSpec((d, QKV_DIM), lambda i: (0, 0)),
                  pl.BlockSpec((1, HEAD_DIM), lambda i: (0, 0)),
                  pl.BlockSpec((1, HEAD_DIM), lambda i: (0, 0)),
                  pl.BlockSpec((t, HEAD_DIM), lambda i: (trow(i), 0)),
                  pl.BlockSpec((t, HEAD_DIM), lambda i: (trow(i), 0))],
        out_specs=[pl.BlockSpec((t, d), lambda i: (i, 0)),
                   pl.BlockSpec((t, kvd), lambda i: (i, 0)),
                   pl.BlockSpec((kvd, t), lambda i: (0, i))],
        out_shape=[jax.ShapeDtypeStruct((n, d), BF16),
                   jax.ShapeDtypeStruct((n, kvd), BF16),
                   jax.ShapeDtypeStruct((kvd, n), BF16)],
        compiler_params=_cparams(("arbitrary",)),
    )(xs, mods, mods, gain, w, qg, kg, cos_t, sin_t)


def _stack_heads(q_ref):
    return jnp.concatenate(
        [q_ref[:, r * HEAD_DIM:(r + 1) * HEAD_DIM] for r in range(KV_REP)], axis=0)


def _store_heads(ot, o_ref):
    for r in range(KV_REP):
        blk = ot[:, r * Q_TILE:(r + 1) * Q_TILE]
        o_ref[:, r * HEAD_DIM:(r + 1) * HEAD_DIM] = blk.T.astype(BF16)


def _global_attn_kernel(lat_q_tiles, q_ref, kl_ref, kc_ref, vtl_ref, vtc_ref, o_ref):
    qt = pl.program_id(2)
    q4 = _stack_heads(q_ref)
    sc = _nt_dot(kc_ref[...], q4)

    def finish(ot, den):
        _store_heads(ot * (1.0 / den), o_ref)

    @pl.when(qt < lat_q_tiles)
    def _():
        sl = _nt_dot(kl_ref[...], q4)
        m = jnp.maximum(jnp.max(sl, axis=0, keepdims=True), jnp.max(sc, axis=0, keepdims=True))
        p_l = jnp.exp2(sl - m)
        p_c = jnp.exp2(sc - m)
        den = jnp.sum(p_l, axis=0, keepdims=True) + jnp.sum(p_c, axis=0, keepdims=True)
        finish(jnp.dot(vtl_ref[...], p_l.astype(BF16), preferred_element_type=F32)
               + jnp.dot(vtc_ref[...], p_c.astype(BF16), preferred_element_type=F32), den)

    @pl.when(qt >= lat_q_tiles)
    def _():
        p_c = jnp.exp2(sc - jnp.max(sc, axis=0, keepdims=True))
        finish(jnp.dot(vtc_ref[...], p_c.astype(BF16), preferred_element_type=F32),
               jnp.sum(p_c, axis=0, keepdims=True))


def _global_attention(q, k, vt, n_batch, seq, ctx_len):
    n, d = q.shape
    lat_q = seq // Q_TILE
    ctx_q = ctx_len // Q_TILE
    ctx_blk0 = (n_batch * seq) // ctx_len
    gw = KV_REP * HEAD_DIM

    def qrow(b, g, t):
        return jnp.where(t < lat_q, b * lat_q + t, n_batch * lat_q + b * ctx_q + (t - lat_q))

    return pl.pallas_call(
        functools.partial(_global_attn_kernel, lat_q),
        grid=(n_batch, N_KV_HEADS, lat_q + ctx_q),
        in_specs=[pl.BlockSpec((Q_TILE, gw), lambda b, g, t: (qrow(b, g, t), g)),
                  pl.BlockSpec((seq, HEAD_DIM), lambda b, g, t: (b, g)),
                  pl.BlockSpec((ctx_len, HEAD_DIM), lambda b, g, t: (ctx_blk0 + b, g)),
                  pl.BlockSpec((HEAD_DIM, seq), lambda b, g, t: (g, b)),
                  pl.BlockSpec((HEAD_DIM, ctx_len), lambda b, g, t: (g, ctx_blk0 + b))],
        out_specs=pl.BlockSpec((Q_TILE, gw), lambda b, g, t: (qrow(b, g, t), g)),
        out_shape=jax.ShapeDtypeStruct((n, d), BF16),
        compiler_params=_cparams(("arbitrary", "arbitrary", "arbitrary")),
    )(q, k, k, vt, vt)


def _window_attn_kernel(seq, q_ref, kl_ref, kc_ref, vtl_ref, vtc_ref, sink_ref, o_ref):
    qt = pl.program_id(2)
    span = Q_TILE + 2 * WINDOW
    start = pl.multiple_of(jnp.clip(qt * Q_TILE - WINDOW, 0, seq - span), Q_TILE)
    q4 = _stack_heads(q_ref)
    cols = KV_REP * Q_TILE
    s_w = _nt_dot(kl_ref[pl.ds(start, span), :], q4)
    s_c = _nt_dot(kc_ref[...], q4)
    kpos = start + lax.broadcasted_iota(jnp.int32, (span, 1), 0)
    qpos = qt * Q_TILE + lax.broadcasted_iota(jnp.int32, (1, cols), 1) % Q_TILE
    s_w = jnp.where(jnp.abs(qpos - kpos) <= WINDOW, s_w, NEG_INF)
    sink = sink_ref[0]
    m = jnp.maximum(jnp.maximum(jnp.max(s_w, axis=0, keepdims=True),
                                jnp.max(s_c, axis=0, keepdims=True)), sink)
    p_w = jnp.exp2(s_w - m)
    p_c = jnp.exp2(s_c - m)
    den = (jnp.sum(p_w, axis=0, keepdims=True) + jnp.sum(p_c, axis=0, keepdims=True)
           + jnp.exp2(sink - m))
    ot = (jnp.dot(vtl_ref[:, pl.ds(start, span)], p_w.astype(BF16), preferred_element_type=F32)
          + jnp.dot(vtc_ref[...], p_c.astype(BF16), preferred_element_type=F32))
    _store_heads(ot * (1.0 / den), o_ref)


def _window_attention(q, k, vt, sink_cols, n_batch, seq, ctx_len):
    n, d = q.shape
    lat_q = seq // Q_TILE
    ctx_blk0 = (n_batch * seq) // ctx_len
    gw = KV_REP * HEAD_DIM
    return pl.pallas_call(
        functools.partial(_window_attn_kernel, seq),
        grid=(n_batch, N_KV_HEADS, lat_q),
        in_specs=[pl.BlockSpec((Q_TILE, gw), lambda b, g, t: (b * lat_q + t, g)),
                  pl.BlockSpec((seq, HEAD_DIM), lambda b, g, t: (b, g)),
                  pl.BlockSpec((ctx_len, HEAD_DIM), lambda b, g, t: (ctx_blk0 + b, g)),
                  pl.BlockSpec((HEAD_DIM, seq), lambda b, g, t: (g, b)),
                  pl.BlockSpec((HEAD_DIM, ctx_len), lambda b, g, t: (g, ctx_blk0 + b)),
                  pl.BlockSpec((1, 1, KV_REP * Q_TILE), lambda b, g, t: (g, 0, 0))],
        out_specs=pl.BlockSpec((Q_TILE, gw), lambda b, g, t: (b * lat_q + t, g)),
        out_shape=jax.ShapeDtypeStruct((n, d), BF16),
        compiler_params=_cparams(("arbitrary", "arbitrary", "arbitrary")),
    )(q, k, k, vt, vt, sink_cols)


def _peer_pre_kernel(o_ref, x_ref, wo_ref, g1_ref, n2_ref, sh_ref, sc_ref, wq_ref, keys_ref,
                     xn_ref, h2t_ref, st_ref):
    y = jnp.dot(o_ref[...], wo_ref[...], preferred_element_type=F32)
    xn = x_ref[...] + g1_ref[0] * y
    xn_ref[...] = xn
    h2 = _rms(xn) * n2_ref[...]
    h2 = h2 * (1.0 + sc_ref[0]) + sh_ref[0]
    h2t_ref[...] = h2.T.astype(BF16)
    qp = jnp.dot(h2.astype(BF16), wq_ref[...], preferred_element_type=F32)
    half = PEER_QDIM // 2
    for hs in range(2 * PEER_HEADS):
        qs = qp[:, hs * half:(hs + 1) * half].astype(BF16)
        st_ref[hs] = _nt_dot(keys_ref[hs % 2], qs)


def _peer_pre(o, xs, mods, layer, w_o, n2, w_q, keys, n_tiles, lat_tiles, tiles_per_seq, n_batch):
    n, d = xs.shape
    t = ROW_TILE
    qd = PEER_HEADS * PEER_QDIM

    def mrow(i):
        return layer * MOD_ROWS + jnp.where(i < lat_tiles, i // tiles_per_seq, n_batch)

    return pl.pallas_call(
        _peer_pre_kernel,
        grid=(n_tiles,),
        in_specs=[pl.BlockSpec((t, d), lambda i: (i, 0)),
                  pl.BlockSpec((t, d), lambda i: (i, 0)),
                  pl.BlockSpec((d, d), lambda i: (0, 0)),
                  pl.BlockSpec((1, 1, d), lambda i: (mrow(i), 0, 2)),
                  pl.BlockSpec((1, d), lambda i: (0, 0)),
                  pl.BlockSpec((1, 1, d), lambda i: (mrow(i), 0, 3)),
                  pl.BlockSpec((1, 1, d), lambda i: (mrow(i), 0, 4)),
                  pl.BlockSpec((d, qd), lambda i: (0, 0)),
                  pl.BlockSpec((2, N_KEYS, PEER_QDIM // 2), lambda i: (0, 0, 0))],
        out_specs=[pl.BlockSpec((t, d), lambda i: (i, 0)),
                   pl.BlockSpec((d, t), lambda i: (0, i)),
                   pl.BlockSpec((2 * PEER_HEADS, N_KEYS, t), lambda i: (0, 0, i))],
        out_shape=[jax.ShapeDtypeStruct((n, d), F32),
                   jax.ShapeDtypeStruct((d, n), BF16),
                   jax.ShapeDtypeStruct((2 * PEER_HEADS, N_KEYS, n), F32)],
        compiler_params=_cparams(("arbitrary",)),
    )(o, xs, w_o, mods, n2, mods, mods, w_q, keys)


def _sorting_network(n):
    def merge(lo, hi, r):
        step = r * 2
        if step < hi - lo:
            yield from merge(lo, hi, step)
            yield from merge(lo + r, hi, step)
            yield from [(i, i + r) for i in range(lo + r, hi - r, step)]
        else:
            yield (lo, lo + r)

    def sort(lo, hi):
        if hi - lo >= 1:
            mid = lo + (hi - lo) // 2
            yield from sort(lo, mid)
            yield from sort(mid + 1, hi)
            yield from merge(lo, hi, 1)

    return list(sort(0, n - 1))


def _exchange(x, i, j):
    x[i], x[j] = jnp.maximum(x[i], x[j]), jnp.minimum(x[i], x[j])


def _top_values(s):
    k = PEER_TOPK
    assert s.shape[0] == k * SUBLANES
    x = [s[i * SUBLANES:(i + 1) * SUBLANES] for i in range(k)]
    for i, j in _sorting_network(k):
        _exchange(x, i, j)
    shift = SUBLANES // 2
    while shift:
        x = [jnp.maximum(x[i], pltpu.roll(x[k - 1 - i], shift, 0)) for i in range(k)]
        d = k // 2
        while d:
            for i in range(k):
                if not i & d:
                    _exchange(x, i, i + d)
            d //= 2
        shift //= 2
    return [xi[0:1] for xi in x]


def _kth_largest(s, k):
    for _ in range(k - 1):
        s = jnp.where(s == jnp.max(s, axis=0, keepdims=True), -jnp.inf, s)
    return jnp.max(s, axis=0, keepdims=True)


def _candidate_sums(v1, v2):
    assert PEER_TOPK == 16
    a1 = jnp.concatenate(v1, axis=0)
    a2 = jnp.concatenate(v2, axis=0)
    row = lax.broadcasted_iota(jnp.int32, (8, a1.shape[1]), 0)
    ninf = -jnp.inf
    return jnp.concatenate([
        v1[0] + a2,
        v1[1] + a2[0:8],
        jnp.where(row < 5, v1[2] + a2[0:8], ninf),
        jnp.where(row < 4, v1[3] + a2[0:8], ninf),
        a1[8:16] + v2[0],
        jnp.where(row >= 4, a1[0:8] + v2[0], ninf),
        jnp.where(row >= 4, a1[0:8] + v2[1], ninf),
        jnp.where(row == 4, a1[0:8] + v2[2], ninf),
    ], axis=0)


def _peer_topk_kernel(st_ref, s2_ref, e2_ref, thr_ref, e1_ref):
    def per_head(h, carry):
        s1 = st_ref[2 * h]
        s2 = st_ref[2 * h + 1]
        v1 = _top_values(s1)
        v2 = _top_values(s2)
        cand = _candidate_sums(v1, v2)
        thr = _kth_largest(cand, PEER_TOPK)
        top = v1[0] + v2[0]
        z = jnp.sum(jnp.where(cand >= thr, jnp.exp(cand - top), 0.0), axis=0, keepdims=True)
        guard = thr - jnp.abs(thr) * THRESHOLD_SLACK
        s2_ref[h] = s2
        e2_ref[h] = jnp.exp(s2 - v2[0]) * (1.0 / z)
        thr_ref[h] = guard - s1
        e1_ref[h] = jnp.exp(s1 - v1[0])
        return carry

    lax.fori_loop(0, PEER_HEADS, per_head, 0)


def _peer_topk(st, n_tiles):
    _, _, n = st.shape
    t = ROW_TILE
    spec = pl.BlockSpec((PEER_HEADS, N_KEYS, t), lambda i: (0, 0, i))
    shape = jax.ShapeDtypeStruct((PEER_HEADS, N_KEYS, n), F32)
    return pl.pallas_call(
        _peer_topk_kernel,
        grid=(n_tiles,),
        in_specs=[pl.BlockSpec((2 * PEER_HEADS, N_KEYS, t), lambda i: (0, 0, i))],
        out_specs=[spec, spec, spec, spec],
        out_shape=[shape, shape, shape, shape],
        compiler_params=_cparams(("arbitrary",)),
    )(st)


def _gelu_tanh(x):
    k = -2.0 * math.sqrt(2.0 / math.pi)
    return x / (1.0 + jnp.exp(x * (k + (k * 0.044715) * (x * x))))


def _peer_dense_kernel(h2t_ref, s2_ref, e2_ref, thr_ref, e1_ref, u_ref, vt_ref, xn_ref, g2_ref,
                       out_ref, acc_ref, act_ref, slab_ref):
    j = pl.program_id(1)
    lanes = acc_ref.shape[1]

    @pl.when(j == 0)
    def _():
        acc_ref[...] = jnp.zeros_like(acc_ref)

    for p in range(u_ref.shape[0] // PIECE_ROWS):
        r = p * PIECE_ROWS
        act_ref[p] = jnp.dot(u_ref[r:r + PIECE_ROWS, :], h2t_ref[...], preferred_element_type=F32)
        for a2 in range(PIECE_ROWS // N_KEYS):
            a = j * PEER_CHUNK + p * (PIECE_ROWS // N_KEYS) + a2
            for lc in range(lanes // SUB_LANES):
                cols = slice(lc * SUB_LANES, (lc + 1) * SUB_LANES)
                w = jnp.zeros((N_KEYS, SUB_LANES), F32)
                for h in range(PEER_HEADS):
                    thr = thr_ref[h, pl.ds(a, 1), :][:, cols]
                    e1 = e1_ref[h, pl.ds(a, 1), :][:, cols]
                    w = w + jnp.where(s2_ref[h, :, cols] >= thr, e2_ref[h, :, cols], 0.0) * e1
                x = act_ref[p, a2 * N_KEYS:(a2 + 1) * N_KEYS, cols]
                slab_ref[p, a2 * N_KEYS:(a2 + 1) * N_KEYS, cols] = (w * _gelu_tanh(x)).astype(BF16)
        acc_ref[...] += jnp.dot(vt_ref[:, r:r + PIECE_ROWS], slab_ref[p], preferred_element_type=F32)

    @pl.when(j == pl.num_programs(1) - 1)
    def _():
        out_ref[...] = xn_ref[...] + g2_ref[0] * acc_ref[...].T


def _peer_dense(h2t, coefs, u, vt, xn, mods, layer, n_tiles, lat_tiles, tiles_per_seq, n_batch):
    d = xn.shape[1]
    t = PEER_TILE
    ce = PEER_CHUNK * N_KEYS
    n_exp = u.shape[0]

    def mrow(i):
        return layer * MOD_ROWS + jnp.where(i < lat_tiles, i // tiles_per_seq, n_batch)

    cspec = pl.BlockSpec((PEER_HEADS, N_KEYS, t), lambda i, j: (0, 0, i))
    return pl.pallas_call(
        _peer_dense_kernel,
        grid=(n_tiles, n_exp // ce),
        in_specs=[pl.BlockSpec((d, t), lambda i, j: (0, i)),
                  cspec, cspec, cspec, cspec,
                  pl.BlockSpec((ce, d), lambda i, j: (j, 0)),
                  pl.BlockSpec((d, ce), lambda i, j: (0, j)),
                  pl.BlockSpec((t, d), lambda i, j: (i, 0)),
                  pl.BlockSpec((1, 1, d), lambda i, j: (mrow(i), 0, 5))],
        out_specs=pl.BlockSpec((t, d), lambda i, j: (i, 0)),
        out_shape=jax.ShapeDtypeStruct((n_tiles * t, d), F32),
        scratch_shapes=[pltpu.VMEM((d, t), F32),
                        pltpu.VMEM((ce // PIECE_ROWS, PIECE_ROWS, t), F32),
                        pltpu.VMEM((ce // PIECE_ROWS, PIECE_ROWS, t), BF16)],
        compiler_params=_cparams(("arbitrary", "arbitrary")),
    )(h2t, *coefs, u, vt, xn, mods)


def _rope_tables(seq, ctx_len):
    pos = jnp.arange(seq)
    row = (pos // GRID_W).astype(F32)
    col = (pos % GRID_W).astype(F32)
    half = HEAD_DIM // 2
    inv = ROPE_THETA ** (-jnp.arange(0, half, 2, dtype=F32) / half)
    ar = row[:, None] * inv
    ac = col[:, None] * inv
    cos = jnp.concatenate([jnp.cos(ar), jnp.cos(ar), jnp.cos(ac), jnp.cos(ac)], axis=-1)
    sin = jnp.concatenate([-jnp.sin(ar), jnp.sin(ar), -jnp.sin(ac), jnp.sin(ac)], axis=-1)
    cos = jnp.concatenate([cos, jnp.ones((ctx_len, HEAD_DIM), F32)], axis=0)
    sin = jnp.concatenate([sin, jnp.zeros((ctx_len, HEAD_DIM), F32)], axis=0)
    return cos, sin


def kernel(x, c, ctx, c_ctx, ada_w, ada_b, norm1_gain, norm2_gain, w_qkv, q_norm_gain, k_norm_gain,
           w_o, attn_sinks, peer_w_q, peer_sub_keys, peer_u, peer_v):
    n_batch, seq, d = x.shape
    ctx_len = ctx.shape[1]
    depth = ada_w.shape[0]
    n_lat = n_batch * seq
    n_ctx = n_batch * ctx_len
    assert depth == 2 and d == D_MODEL and n_batch < MOD_ROWS
    assert seq % PEER_TILE == 0 and n_ctx % PEER_TILE == 0 and ctx_len == ROW_TILE
    assert seq >= Q_TILE + 2 * WINDOW and n_lat % ctx_len == 0

    cc = jnp.concatenate([c, c_ctx[None, :], jnp.zeros((MOD_ROWS - n_batch - 1, d), F32)], axis=0)
    mods = _modulation(cc, ada_w, ada_b).reshape(depth * MOD_ROWS, 1, N_MOD * d)
    cos_t, sin_t = _rope_tables(seq, ctx_len)
    xs = jnp.concatenate([x.reshape(n_lat, d), ctx.reshape(n_ctx, d)], axis=0)
    scale = HEAD_DIM ** -0.5 * math.log2(math.e)

    for layer in range(depth):
        last = layer == depth - 1
        q, k, vt = _qkv(xs, mods, layer, norm1_gain[layer][None, :], w_qkv[layer].astype(BF16),
                       (q_norm_gain[layer] * scale)[None, :], k_norm_gain[layer][None, :],
                       cos_t, sin_t, n_lat // ROW_TILE, seq // ROW_TILE, n_batch)
        if layer % 2 == 0:
            o = _global_attention(q, k, vt, n_batch, seq, ctx_len)
        else:
            sink = (attn_sinks[layer // 2] * math.log2(math.e)).reshape(N_KV_HEADS, 1, KV_REP, 1)
            sink_cols = jnp.broadcast_to(sink, (N_KV_HEADS, 1, KV_REP, Q_TILE))
            o = _window_attention(q, k, vt, sink_cols.reshape(N_KV_HEADS, 1, KV_REP * Q_TILE),
                                  n_batch, seq, ctx_len)
        n_rows = n_lat if last else n_lat + n_ctx
        xn, h2t, st = _peer_pre(o, xs, mods, layer, w_o[layer].astype(BF16),
                                norm2_gain[layer][None, :], peer_w_q[layer].astype(BF16),
                                peer_sub_keys[layer].astype(BF16), n_rows // ROW_TILE,
                                n_lat // ROW_TILE, seq // ROW_TILE, n_batch)
        coefs = _peer_topk(st, n_rows // ROW_TILE)
        xs = _peer_dense(h2t, coefs, peer_u[layer].astype(BF16), peer_v[layer].T.astype(BF16),
                         xn, mods, layer, n_rows // PEER_TILE, n_lat // PEER_TILE,
                         seq // PEER_TILE, n_batch)
    return xs[:n_lat].reshape(n_batch, seq, d)
```

```python
import functools
import math

import jax
import jax.numpy as jnp
from jax import lax
from jax.experimental import pallas as pl
from jax.experimental.pallas import tpu as pltpu

F32 = jnp.float32
BF16 = jnp.bfloat16

D_MODEL = 1024
N_HEADS = 8
N_KV_HEADS = 2
HEAD_DIM = 128
KV_REP = N_HEADS // N_KV_HEADS
QKV_DIM = (N_HEADS + 2 * N_KV_HEADS) * HEAD_DIM
GRID_W = 64
WINDOW = 128
ROPE_THETA = 10000.0
RMS_EPS = 1e-6
NEG_INF = -1e30
PEER_HEADS = 8
PEER_TOPK = 16
N_KEYS = 128
PEER_QDIM = 256
N_MOD = 6
MOD_ROWS = 8

ROW_TILE = 256
Q_TILE = 128
PEER_TILE = 512
PEER_CHUNK = 8
SUB_LANES = 128
GATE_ROWS = 32
SUBLANES = 8
THRESHOLD_SLACK = 2.0 ** -22
VMEM_LIMIT = 56 * 1024 * 1024


def _cparams(sem):
    return pltpu.CompilerParams(dimension_semantics=sem, vmem_limit_bytes=VMEM_LIMIT)


def _nt_dot(a, b):
    return lax.dot_general(a, b, (((1,), (1,)), ((), ())), preferred_element_type=F32)


def _mod_kernel(c_ref, w_ref, b_ref, o_ref):
    c = c_ref[...]
    a = c * (1.0 / (1.0 + jnp.exp(-c)))
    o_ref[0] = jnp.dot(a, w_ref[0], preferred_element_type=F32,
                       precision=lax.Precision.HIGHEST) + b_ref[0]


def _modulation(cc, ada_w, ada_b):
    depth, d, n = ada_w.shape
    tn = 1536
    return pl.pallas_call(
        _mod_kernel,
        grid=(depth, n // tn),
        in_specs=[pl.BlockSpec((MOD_ROWS, d), lambda l, j: (0, 0)),
                  pl.BlockSpec((1, d, tn), lambda l, j: (l, 0, j)),
                  pl.BlockSpec((1, 1, tn), lambda l, j: (l, 0, j))],
        out_specs=pl.BlockSpec((1, MOD_ROWS, tn), lambda l, j: (l, 0, j)),
        out_shape=jax.ShapeDtypeStruct((depth, MOD_ROWS, n), F32),
        compiler_params=_cparams(("arbitrary", "arbitrary")),
    )(cc, ada_w, ada_b.reshape(depth, 1, n))


def _rms(x):
    return x * lax.rsqrt(jnp.mean(x * x, axis=-1, keepdims=True) + RMS_EPS)


def _qkv_kernel(x_ref, sh_ref, sc_ref, g_ref, w_ref, qg_ref, kg_ref, cos_ref, sin_ref,
                q_ref, k_ref, vt_ref):
    h = _rms(x_ref[...]) * g_ref[...]
    h = h * (1.0 + sc_ref[0]) + sh_ref[0]
    y = jnp.dot(h.astype(BF16), w_ref[...], preferred_element_type=F32)
    cos = cos_ref[...]
    sin = sin_ref[...]
    lane = lax.broadcasted_iota(jnp.int32, cos.shape, 1)
    first = (lane % (HEAD_DIM // 2)) < (HEAD_DIM // 4)

    def head(col, gain):
        n = _rms(y[:, col * HEAD_DIM:(col + 1) * HEAD_DIM]) * gain
        sw = jnp.where(first, pltpu.roll(n, HEAD_DIM - HEAD_DIM // 4, 1),
                       pltpu.roll(n, HEAD_DIM // 4, 1))
        return n * cos + sw * sin

    for hh in range(N_HEADS):
        q_ref[:, hh * HEAD_DIM:(hh + 1) * HEAD_DIM] = head(hh, qg_ref[...]).astype(BF16)
    for g in range(N_KV_HEADS):
        k_ref[:, g * HEAD_DIM:(g + 1) * HEAD_DIM] = head(N_HEADS + g, kg_ref[...]).astype(BF16)
    v0 = (N_HEADS + N_KV_HEADS) * HEAD_DIM
    vt_ref[...] = y[:, v0:].T.astype(BF16)


def _qkv(xs, mods, layer, gain, w, qg, kg, cos_t, sin_t, lat_tiles, tiles_per_seq, n_batch):
    n, d = xs.shape
    t = ROW_TILE

    def mrow(i):
        return layer * MOD_ROWS + jnp.where(i < lat_tiles, i // tiles_per_seq, n_batch)

    def trow(i):
        return jnp.where(i < lat_tiles, i % tiles_per_seq, tiles_per_seq)

    kvd = N_KV_HEADS * HEAD_DIM
    return pl.pallas_call(
        _qkv_kernel,
        grid=(n // t,),
        in_specs=[pl.BlockSpec((t, d), lambda i: (i, 0)),
                  pl.BlockSpec((1, 1, d), lambda i: (mrow(i), 0, 0)),
                  pl.BlockSpec((1, 1, d), lambda i: (mrow(i), 0, 1)),
                  pl.BlockSpec((1, d), lambda i: (0, 0)),
                  pl.BlockSpec((d, QKV_DIM), lambda i: (0, 0)),
                  pl.BlockSpec((1, HEAD_DIM), lambda i: (0, 0)),
                  pl.BlockSpec((1, HEAD_DIM), lambda i: (0, 0)),
                  pl.BlockSpec((t, HEAD_DIM), lambda i: (trow(i), 0)),
                  pl.BlockSpec((t, HEAD_DIM), lambda i: (trow(i), 0))],
        out_specs=[pl.BlockSpec((t, d), lambda i: (i, 0)),
                   pl.BlockSpec((t, kvd), lambda i: (i, 0)),
                   pl.BlockSpec((kvd, t), lambda i: (0, i))],
        out_shape=[jax.ShapeDtypeStruct((n, d), BF16),
                   jax.ShapeDtypeStruct((n, kvd), BF16),
                   jax.ShapeDtypeStruct((kvd, n), BF16)],
        compiler_params=_cparams(("arbitrary",)),
    )(xs, mods, mods, gain, w, qg, kg, cos_t, sin_t)


def _stack_heads(q_ref):
    return jnp.concatenate(
        [q_ref[:, r * HEAD_DIM:(r + 1) * HEAD_DIM] for r in range(KV_REP)], axis=0)


def _store_heads(ot, o_ref):
    for r in range(KV_REP):
        blk = ot[:, r * Q_TILE:(r + 1) * Q_TILE]
        o_ref[:, r * HEAD_DIM:(r + 1) * HEAD_DIM] = blk.T.astype(BF16)


def _global_attn_kernel(lat_q_tiles, q_ref, kl_ref, kc_ref, vtl_ref, vtc_ref, o_ref):
    qt = pl.program_id(2)
    q4 = _stack_heads(q_ref)
    sc = _nt_dot(kc_ref[...], q4)

    def finish(ot, den):
        _store_heads(ot * (1.0 / den), o_ref)

    @pl.when(qt < lat_q_tiles)
    def _():
        sl = _nt_dot(kl_ref[...], q4)
        m = jnp.maximum(jnp.max(sl, axis=0, keepdims=True), jnp.max(sc, axis=0, keepdims=True))
        p_l = jnp.exp2(sl - m)
        p_c = jnp.exp2(sc - m)
        den = jnp.sum(p_l, axis=0, keepdims=True) + jnp.sum(p_c, axis=0, keepdims=True)
        finish(jnp.dot(vtl_ref[...], p_l.astype(BF16), preferred_element_type=F32)
               + jnp.dot(vtc_ref[...], p_c.astype(BF16), preferred_element_type=F32), den)

    @pl.when(qt >= lat_q_tiles)
    def _():
        p_c = jnp.exp2(sc - jnp.max(sc, axis=0, keepdims=True))
        finish(jnp.dot(vtc_ref[...], p_c.astype(BF16), preferred_element_type=F32),
               jnp.sum(p_c, axis=0, keepdims=True))


def _global_attention(q, k, vt, n_batch, seq, ctx_len):
    n, d = q.shape
    lat_q = seq // Q_TILE
    ctx_q = ctx_len // Q_TILE
    ctx_blk0 = (n_batch * seq) // ctx_len
    gw = KV_REP * HEAD_DIM

    def qrow(b, g, t):
        return jnp.where(t < lat_q, b * lat_q + t, n_batch * lat_q + b * ctx_q + (t - lat_q))

    return pl.pallas_call(
        functools.partial(_global_attn_kernel, lat_q),
        grid=(n_batch, N_KV_HEADS, lat_q + ctx_q),
        in_specs=[pl.BlockSpec((Q_TILE, gw), lambda b, g, t: (qrow(b, g, t), g)),
                  pl.BlockSpec((seq, HEAD_DIM), lambda b, g, t: (b, g)),
                  pl.BlockSpec((ctx_len, HEAD_DIM), lambda b, g, t: (ctx_blk0 + b, g)),
                  pl.BlockSpec((HEAD_DIM, seq), lambda b, g, t: (g, b)),
                  pl.BlockSpec((HEAD_DIM, ctx_len), lambda b, g, t: (g, ctx_blk0 + b))],
        out_specs=pl.BlockSpec((Q_TILE, gw), lambda b, g, t: (qrow(b, g, t), g)),
        out_shape=jax.ShapeDtypeStruct((n, d), BF16),
        compiler_params=_cparams(("arbitrary", "arbitrary", "arbitrary")),
    )(q, k, k, vt, vt)


def _window_attn_kernel(seq, q_ref, kl_ref, kc_ref, vtl_ref, vtc_ref, sink_ref, o_ref):
    qt = pl.program_id(2)
    span = Q_TILE + 2 * WINDOW
    start = pl.multiple_of(jnp.clip(qt * Q_TILE - WINDOW, 0, seq - span), Q_TILE)
    q4 = _stack_heads(q_ref)
    cols = KV_REP * Q_TILE
    s_w = _nt_dot(kl_ref[pl.ds(start, span), :], q4)
    s_c = _nt_dot(kc_ref[...], q4)
    kpos = start + lax.broadcasted_iota(jnp.int32, (span, 1), 0)
    qpos = qt * Q_TILE + lax.broadcasted_iota(jnp.int32, (1, cols), 1) % Q_TILE
    s_w = jnp.where(jnp.abs(qpos - kpos) <= WINDOW, s_w, NEG_INF)
    sink = sink_ref[0]
    m = jnp.maximum(jnp.maximum(jnp.max(s_w, axis=0, keepdims=True),
                                jnp.max(s_c, axis=0, keepdims=True)), sink)
    p_w = jnp.exp2(s_w - m)
    p_c = jnp.exp2(s_c - m)
    den = (jnp.sum(p_w, axis=0, keepdims=True) + jnp.sum(p_c, axis=0, keepdims=True)
           + jnp.exp2(sink - m))
    ot = (jnp.dot(vtl_ref[:, pl.ds(start, span)], p_w.astype(BF16), preferred_element_type=F32)
          + jnp.dot(vtc_ref[...], p_c.astype(BF16), preferred_element_type=F32))
    _store_heads(ot * (1.0 / den), o_ref)


def _window_attention(q, k, vt, sink_cols, n_batch, seq, ctx_len):
    n, d = q.shape
    lat_q = seq // Q_TILE
    ctx_blk0 = (n_batch * seq) // ctx_len
    gw = KV_REP * HEAD_DIM
    return pl.pallas_call(
        functools.partial(_window_attn_kernel, seq),
        grid=(n_batch, N_KV_HEADS, lat_q),
        in_specs=[pl.BlockSpec((Q_TILE, gw), lambda b, g, t: (b * lat_q + t, g)),
                  pl.BlockSpec((seq, HEAD_DIM), lambda b, g, t: (b, g)),
                  pl.BlockSpec((ctx_len, HEAD_DIM), lambda b, g, t: (ctx_blk0 + b, g)),
                  pl.BlockSpec((HEAD_DIM, seq), lambda b, g, t: (g, b)),
                  pl.BlockSpec((HEAD_DIM, ctx_len), lambda b, g, t: (g, ctx_blk0 + b)),
                  pl.BlockSpec((1, 1, KV_REP * Q_TILE), lambda b, g, t: (g, 0, 0))],
        out_specs=pl.BlockSpec((Q_TILE, gw), lambda b, g, t: (b * lat_q + t, g)),
        out_shape=jax.ShapeDtypeStruct((n, d), BF16),
        compiler_params=_cparams(("arbitrary", "arbitrary", "arbitrary")),
    )(q, k, k, vt, vt, sink_cols)


def _peer_pre_kernel(o_ref, x_ref, wo_ref, g1_ref, n2_ref, sh_ref, sc_ref, wq_ref, keys_ref,
                     xn_ref, h2t_ref, st_ref):
    y = jnp.dot(o_ref[...], wo_ref[...], preferred_element_type=F32)
    xn = x_ref[...] + g1_ref[0] * y
    xn_ref[...] = xn
    h2 = _rms(xn) * n2_ref[...]
    h2 = h2 * (1.0 + sc_ref[0]) + sh_ref[0]
    h2t_ref[...] = h2.T.astype(BF16)
    qp = jnp.dot(h2.astype(BF16), wq_ref[...], preferred_element_type=F32)
    half = PEER_QDIM // 2
    for hs in range(2 * PEER_HEADS):
        qs = qp[:, hs * half:(hs + 1) * half].astype(BF16)
        st_ref[hs] = _nt_dot(keys_ref[hs % 2], qs)


def _peer_pre(o, xs, mods, layer, w_o, n2, w_q, keys, n_tiles, lat_tiles, tiles_per_seq, n_batch):
    n, d = xs.shape
    t = ROW_TILE
    qd = PEER_HEADS * PEER_QDIM

    def mrow(i):
        return layer * MOD_ROWS + jnp.where(i < lat_tiles, i // tiles_per_seq, n_batch)

    return pl.pallas_call(
        _peer_pre_kernel,
        grid=(n_tiles,),
        in_specs=[pl.BlockSpec((t, d), lambda i: (i, 0)),
                  pl.BlockSpec((t, d), lambda i: (i, 0)),
                  pl.BlockSpec((d, d), lambda i: (0, 0)),
                  pl.BlockSpec((1, 1, d), lambda i: (mrow(i), 0, 2)),
                  pl.BlockSpec((1, d), lambda i: (0, 0)),
                  pl.BlockSpec((1, 1, d), lambda i: (mrow(i), 0, 3)),
                  pl.BlockSpec((1, 1, d), lambda i: (mrow(i), 0, 4)),
                  pl.BlockSpec((d, qd), lambda i: (0, 0)),
                  pl.BlockSpec((2, N_KEYS, PEER_QDIM // 2), lambda i: (0, 0, 0))],
        out_specs=[pl.BlockSpec((t, d), lambda i: (i, 0)),
                   pl.BlockSpec((d, t), lambda i: (0, i)),
                   pl.BlockSpec((2 * PEER_HEADS, N_KEYS, t), lambda i: (0, 0, i))],
        out_shape=[jax.ShapeDtypeStruct((n, d), F32),
                   jax.ShapeDtypeStruct((d, n), BF16),
                   jax.ShapeDtypeStruct((2 * PEER_HEADS, N_KEYS, n), F32)],
        compiler_params=_cparams(("arbitrary",)),
    )(o, xs, w_o, mods, n2, mods, mods, w_q, keys)


def _sorting_network(n):
    def merge(lo, hi, r):
        step = r * 2
        if step < hi - lo:
            yield from merge(lo, hi, step)
            yield from merge(lo + r, hi, step)
            yield from [(i, i + r) for i in range(lo + r, hi - r, step)]
        else:
            yield (lo, lo + r)

    def sort(lo, hi):
        if hi - lo >= 1:
            mid = lo + (hi - lo) // 2
            yield from sort(lo, mid)
            yield from sort(mid + 1, hi)
            yield from merge(lo, hi, 1)

    return list(sort(0, n - 1))


def _exchange(x, i, j):
    x[i], x[j] = jnp.maximum(x[i], x[j]), jnp.minimum(x[i], x[j])


def _top_values(s):
    k = PEER_TOPK
    assert s.shape[0] == k * SUBLANES
    x = [s[i * SUBLANES:(i + 1) * SUBLANES] for i in range(k)]
    for i, j in _sorting_network(k):
        _exchange(x, i, j)
    shift = SUBLANES // 2
    while shift:
        x = [jnp.maximum(x[i], pltpu.roll(x[k - 1 - i], shift, 0)) for i in range(k)]
        d = k // 2
        while d:
            for i in range(k):
                if not i & d:
                    _exchange(x, i, i + d)
            d //= 2
        shift //= 2
    return [xi[0:1] for xi in x]


def _kth_largest(s, k):
    for _ in range(k - 1):
        s = jnp.where(s == jnp.max(s, axis=0, keepdims=True), -jnp.inf, s)
    return jnp.max(s, axis=0, keepdims=True)


def _candidate_sums(v1, v2):
    assert PEER_TOPK == 16
    a1 = jnp.concatenate(v1, axis=0)
    a2 = jnp.concatenate(v2, axis=0)
    row = lax.broadcasted_iota(jnp.int32, (8, a1.shape[1]), 0)
    ninf = -jnp.inf
    return jnp.concatenate([
        v1[0] + a2,
        v1[1] + a2[0:8],
        jnp.where(row < 5, v1[2] + a2[0:8], ninf),
        jnp.where(row < 4, v1[3] + a2[0:8], ninf),
        a1[8:16] + v2[0],
        jnp.where(row >= 4, a1[0:8] + v2[0], ninf),
        jnp.where(row >= 4, a1[0:8] + v2[1], ninf),
        jnp.where(row == 4, a1[0:8] + v2[2], ninf),
    ], axis=0)


def _peer_topk_kernel(st_ref, s2_ref, e2_ref, thr_ref, e1_ref):
    def per_head(h, carry):
        s1 = st_ref[2 * h]
        s2 = st_ref[2 * h + 1]
        v1 = _top_values(s1)
        v2 = _top_values(s2)
        cand = _candidate_sums(v1, v2)
        thr = _kth_largest(cand, PEER_TOPK)
        top = v1[0] + v2[0]
        z = jnp.sum(jnp.where(cand >= thr, jnp.exp(cand - top), 0.0), axis=0, keepdims=True)
        guard = thr - jnp.abs(thr) * THRESHOLD_SLACK
        s2_ref[h] = s2
        e2_ref[h] = jnp.exp(s2 - v2[0]) * (1.0 / z)
        thr_ref[h] = guard - s1
        e1_ref[h] = jnp.exp(s1 - v1[0])
        return carry

    lax.fori_loop(0, PEER_HEADS, per_head, 0)


def _peer_topk(st, n_tiles):
    _, _, n = st.shape
    t = ROW_TILE
    spec = pl.BlockSpec((PEER_HEADS, N_KEYS, t), lambda i: (0, 0, i))
    shape = jax.ShapeDtypeStruct((PEER_HEADS, N_KEYS, n), F32)
    return pl.pallas_call(
        _peer_topk_kernel,
        grid=(n_tiles,),
        in_specs=[pl.BlockSpec((2 * PEER_HEADS, N_KEYS, t), lambda i: (0, 0, i))],
        out_specs=[spec, spec, spec, spec],
        out_shape=[shape, shape, shape, shape],
        compiler_params=_cparams(("arbitrary",)),
    )(st)


def _gelu_tanh(x):
    k = -2.0 * math.sqrt(2.0 / math.pi)
    return x / (1.0 + jnp.exp(x * (k + (k * 0.044715) * (x * x))))


def _peer_dense_kernel(h2t_ref, s2_ref, e2_ref, thr_ref, e1_ref, u_ref, vt_ref, xn_ref, g2_ref,
                       out_ref, acc_ref, act_ref, slab_ref):
    j = pl.program_id(1)
    lanes = acc_ref.shape[1]
    lane_blocks = lanes // SUB_LANES

    @pl.when(j == 0)
    def _():
        acc_ref[...] = jnp.zeros_like(acc_ref)

    act_ref[...] = jnp.dot(u_ref[...], h2t_ref[...], preferred_element_type=F32)

    def gate_block(blk, carry):
        bc = blk // lane_blocks
        lc = blk % lane_blocks
        rows = pl.ds(pl.multiple_of(bc * GATE_ROWS, GATE_ROWS), GATE_ROWS)
        cols = pl.ds(pl.multiple_of(lc * SUB_LANES, SUB_LANES), SUB_LANES)
        w = [jnp.zeros((GATE_ROWS, SUB_LANES), F32) for _ in range(PEER_CHUNK)]
        for h in range(PEER_HEADS):
            s2 = s2_ref[h, rows, cols]
            e2 = e2_ref[h, rows, cols]
            thr = thr_ref[h, pl.ds(j * PEER_CHUNK, PEER_CHUNK), cols]
            e1 = e1_ref[h, pl.ds(j * PEER_CHUNK, PEER_CHUNK), cols]
            for al in range(PEER_CHUNK):
                w[al] = w[al] + jnp.where(s2 >= thr[al:al + 1], e2, 0.0) * e1[al:al + 1]
        for al in range(PEER_CHUNK):
            arows = pl.ds(pl.multiple_of(al * N_KEYS + bc * GATE_ROWS, GATE_ROWS), GATE_ROWS)
            slab_ref[arows, cols] = (w[al] * _gelu_tanh(act_ref[arows, cols])).astype(BF16)
        return carry

    lax.fori_loop(0, (N_KEYS // GATE_ROWS) * lane_blocks, gate_block, 0)
    acc_ref[...] += jnp.dot(vt_ref[...], slab_ref[...], preferred_element_type=F32)

    @pl.when(j == pl.num_programs(1) - 1)
    def _():
        out_ref[...] = xn_ref[...] + g2_ref[0] * acc_ref[...].T


def _peer_dense(h2t, coefs, u, vt, xn, mods, layer, n_tiles, lat_tiles, tiles_per_seq, n_batch):
    d = xn.shape[1]
    t = PEER_TILE
    ce = PEER_CHUNK * N_KEYS
    n_exp = u.shape[0]

    def mrow(i):
        return layer * MOD_ROWS + jnp.where(i < lat_tiles, i // tiles_per_seq, n_batch)

    cspec = pl.BlockSpec((PEER_HEADS, N_KEYS, t), lambda i, j: (0, 0, i))
    return pl.pallas_call(
        _peer_dense_kernel,
        grid=(n_tiles, n_exp // ce),
        in_specs=[pl.BlockSpec((d, t), lambda i, j: (0, i)),
                  cspec, cspec, cspec, cspec,
                  pl.BlockSpec((ce, d), lambda i, j: (j, 0)),
                  pl.BlockSpec((d, ce), lambda i, j: (0, j)),
                  pl.BlockSpec((t, d), lambda i, j: (i, 0)),
                  pl.BlockSpec((1, 1, d), lambda i, j: (mrow(i), 0, 5))],
        out_specs=pl.BlockSpec((t, d), lambda i, j: (i, 0)),
        out_shape=jax.ShapeDtypeStruct((n_tiles * t, d), F32),
        scratch_shapes=[pltpu.VMEM((d, t), F32),
                        pltpu.VMEM((ce, t), F32),
                        pltpu.VMEM((ce, t), BF16)],
        compiler_params=_cparams(("arbitrary", "arbitrary")),
    )(h2t, *coefs, u, vt, xn, mods)


def _rope_tables(seq, ctx_len):
    pos = jnp.arange(seq)
    row = (pos // GRID_W).astype(F32)
    col = (pos % GRID_W).astype(F32)
    half = HEAD_DIM // 2
    inv = ROPE_THETA ** (-jnp.arange(0, half, 2, dtype=F32) / half)
    ar = row[:, None] * inv
    ac = col[:, None] * inv
    cos = jnp.concatenate([jnp.cos(ar), jnp.cos(ar), jnp.cos(ac), jnp.cos(ac)], axis=-1)
    sin = jnp.concatenate([-jnp.sin(ar), jnp.sin(ar), -jnp.sin(ac), jnp.sin(ac)], axis=-1)
    cos = jnp.concatenate([cos, jnp.ones((ctx_len, HEAD_DIM), F32)], axis=0)
    sin = jnp.concatenate([sin, jnp.zeros((ctx_len, HEAD_DIM), F32)], axis=0)
    return cos, sin


def kernel(x, c, ctx, c_ctx, ada_w, ada_b, norm1_gain, norm2_gain, w_qkv, q_norm_gain, k_norm_gain,
           w_o, attn_sinks, peer_w_q, peer_sub_keys, peer_u, peer_v):
    n_batch, seq, d = x.shape
    ctx_len = ctx.shape[1]
    depth = ada_w.shape[0]
    n_lat = n_batch * seq
    n_ctx = n_batch * ctx_len
    assert depth == 2 and d == D_MODEL and n_batch < MOD_ROWS
    assert seq % PEER_TILE == 0 and n_ctx % PEER_TILE == 0 and ctx_len == ROW_TILE
    assert seq >= Q_TILE + 2 * WINDOW and n_lat % ctx_len == 0

    cc = jnp.concatenate([c, c_ctx[None, :], jnp.zeros((MOD_ROWS - n_batch - 1, d), F32)], axis=0)
    mods = _modulation(cc, ada_w, ada_b).reshape(depth * MOD_ROWS, 1, N_MOD * d)
    cos_t, sin_t = _rope_tables(seq, ctx_len)
    xs = jnp.concatenate([x.reshape(n_lat, d), ctx.reshape(n_ctx, d)], axis=0)
    scale = HEAD_DIM ** -0.5 * math.log2(math.e)

    for layer in range(depth):
        last = layer == depth - 1
        q, k, vt = _qkv(xs, mods, layer, norm1_gain[layer][None, :], w_qkv[layer].astype(BF16),
                       (q_norm_gain[layer] * scale)[None, :], k_norm_gain[layer][None, :],
                       cos_t, sin_t, n_lat // ROW_TILE, seq // ROW_TILE, n_batch)
        if layer % 2 == 0:
            o = _global_attention(q, k, vt, n_batch, seq, ctx_len)
        else:
            sink = (attn_sinks[layer // 2] * math.log2(math.e)).reshape(N_KV_HEADS, 1, KV_REP, 1)
            sink_cols = jnp.broadcast_to(sink, (N_KV_HEADS, 1, KV_REP, Q_TILE))
            o = _window_attention(q, k, vt, sink_cols.reshape(N_KV_HEADS, 1, KV_REP * Q_TILE),
                                  n_batch, seq, ctx_len)
        n_rows = n_lat if last else n_lat + n_ctx
        xn, h2t, st = _peer_pre(o, xs, mods, layer, w_o[layer].astype(BF16),
                                norm2_gain[layer][None, :], peer_w_q[layer].astype(BF16),
                                peer_sub_keys[layer].astype(BF16), n_rows // ROW_TILE,
                                n_lat // ROW_TILE, seq // ROW_TILE, n_batch)
        coefs = _peer_topk(st, n_rows // ROW_TILE)
        xs = _peer_dense(h2t, coefs, peer_u[layer].astype(BF16), peer_v[layer].T.astype(BF16),
                         xn, mods, layer, n_rows // PEER_TILE, n_lat // PEER_TILE,
                         seq // PEER_TILE, n_batch)
    return xs[:n_lat].reshape(n_batch, seq, d)
```

```python
import functools
import math

import jax
import jax.numpy as jnp
from jax import lax
from jax.experimental import pallas as pl
from jax.experimental.pallas import tpu as pltpu

F32 = jnp.float32
BF16 = jnp.bfloat16

D_MODEL = 1024
N_HEADS = 8
N_KV_HEADS = 2
HEAD_DIM = 128
KV_REP = N_HEADS // N_KV_HEADS
QKV_DIM = (N_HEADS + 2 * N_KV_HEADS) * HEAD_DIM
GRID_W = 64
WINDOW = 128
ROPE_THETA = 10000.0
RMS_EPS = 1e-6
NEG_INF = -1e30
PEER_HEADS = 8
PEER_TOPK = 16
N_KEYS = 128
PEER_QDIM = 256
N_MOD = 6
MOD_ROWS = 8

ROW_TILE = 256
Q_TILE = 128
PEER_TILE = 512
PEER_CHUNK = 8
SUB_LANES = 128
GATE_ROWS = 32
SUBLANES = 8
THRESHOLD_SLACK = 2.0 ** -22
VMEM_LIMIT = 56 * 1024 * 1024


def _cparams(sem):
    return pltpu.CompilerParams(dimension_semantics=sem, vmem_limit_bytes=VMEM_LIMIT)


def _nt_dot(a, b):
    return lax.dot_general(a, b, (((1,), (1,)), ((), ())), preferred_element_type=F32)


def _mod_kernel(c_ref, w_ref, b_ref, o_ref):
    c = c_ref[...]
    a = c * (1.0 / (1.0 + jnp.exp(-c)))
    o_ref[0] = jnp.dot(a, w_ref[0], preferred_element_type=F32,
                       precision=lax.Precision.HIGHEST) + b_ref[0]


def _modulation(cc, ada_w, ada_b):
    depth, d, n = ada_w.shape
    tn = 1536
    return pl.pallas_call(
        _mod_kernel,
        grid=(depth, n // tn),
        in_specs=[pl.BlockSpec((MOD_ROWS, d), lambda l, j: (0, 0)),
                  pl.BlockSpec((1, d, tn), lambda l, j: (l, 0, j)),
                  pl.BlockSpec((1, 1, tn), lambda l, j: (l, 0, j))],
        out_specs=pl.BlockSpec((1, MOD_ROWS, tn), lambda l, j: (l, 0, j)),
        out_shape=jax.ShapeDtypeStruct((depth, MOD_ROWS, n), F32),
        compiler_params=_cparams(("arbitrary", "arbitrary")),
    )(cc, ada_w, ada_b.reshape(depth, 1, n))


def _rms(x):
    return x * lax.rsqrt(jnp.mean(x * x, axis=-1, keepdims=True) + RMS_EPS)


def _qkv_kernel(x_ref, sh_ref, sc_ref, g_ref, w_ref, qg_ref, kg_ref, cos_ref, sin_ref,
                q_ref, k_ref, vt_ref):
    h = _rms(x_ref[...]) * g_ref[...]
    h = h * (1.0 + sc_ref[0]) + sh_ref[0]
    y = jnp.dot(h.astype(BF16), w_ref[...], preferred_element_type=F32)
    cos = cos_ref[...]
    sin = sin_ref[...]
    lane = lax.broadcasted_iota(jnp.int32, cos.shape, 1)
    first = (lane % (HEAD_DIM // 2)) < (HEAD_DIM // 4)

    def head(col, gain):
        n = _rms(y[:, col * HEAD_DIM:(col + 1) * HEAD_DIM]) * gain
        sw = jnp.where(first, pltpu.roll(n, HEAD_DIM - HEAD_DIM // 4, 1),
                       pltpu.roll(n, HEAD_DIM // 4, 1))
        return n * cos + sw * sin

    for hh in range(N_HEADS):
        q_ref[:, hh * HEAD_DIM:(hh + 1) * HEAD_DIM] = head(hh, qg_ref[...]).astype(BF16)
    for g in range(N_KV_HEADS):
        k_ref[:, g * HEAD_DIM:(g + 1) * HEAD_DIM] = head(N_HEADS + g, kg_ref[...]).astype(BF16)
    v0 = (N_HEADS + N_KV_HEADS) * HEAD_DIM
    vt_ref[...] = y[:, v0:].T.astype(BF16)


def _qkv(xs, mods, layer, gain, w, qg, kg, cos_t, sin_t, lat_tiles, tiles_per_seq, n_batch):
    n, d = xs.shape
    t = ROW_TILE

    def mrow(i):
        return layer * MOD_ROWS + jnp.where(i < lat_tiles, i // tiles_per_seq, n_batch)

    def trow(i):
        return jnp.where(i < lat_tiles, i % tiles_per_seq, tiles_per_seq)

    kvd = N_KV_HEADS * HEAD_DIM
    return pl.pallas_call(
        _qkv_kernel,
        grid=(n // t,),
        in_specs=[pl.BlockSpec((t, d), lambda i: (i, 0)),
                  pl.BlockSpec((1, 1, d), lambda i: (mrow(i), 0, 0)),
                  pl.BlockSpec((1, 1, d), lambda i: (mrow(i), 0, 1)),
                  pl.BlockSpec((1, d), lambda i: (0, 0)),
                  pl.BlockSpec((d, QKV_DIM), lambda i: (0, 0)),
                  pl.BlockSpec((1, HEAD_DIM), lambda i: (0, 0)),
                  pl.BlockSpec((1, HEAD_DIM), lambda i: (0, 0)),
                  pl.BlockSpec((t, HEAD_DIM), lambda i: (trow(i), 0)),
                  pl.BlockSpec((t, HEAD_DIM), lambda i: (trow(i), 0))],
        out_specs=[pl.BlockSpec((t, d), lambda i: (i, 0)),
                   pl.BlockSpec((t, kvd), lambda i: (i, 0)),
                   pl.BlockSpec((kvd, t), lambda i: (0, i))],
        out_shape=[jax.ShapeDtypeStruct((n, d), BF16),
                   jax.ShapeDtypeStruct((n, kvd), BF16),
                   jax.ShapeDtypeStruct((kvd, n), BF16)],
        compiler_params=_cparams(("arbitrary",)),
    )(xs, mods, mods, gain, w, qg, kg, cos_t, sin_t)


def _stack_heads(q_ref):
    return jnp.concatenate(
        [q_ref[:, r * HEAD_DIM:(r + 1) * HEAD_DIM] for r in range(KV_REP)], axis=0)


def _store_heads(ot, o_ref):
    for r in range(KV_REP):
        blk = ot[:, r * Q_TILE:(r + 1) * Q_TILE]
        o_ref[:, r * HEAD_DIM:(r + 1) * HEAD_DIM] = blk.T.astype(BF16)


def _global_attn_kernel(lat_q_tiles, q_ref, kl_ref, kc_ref, vtl_ref, vtc_ref, o_ref):
    qt = pl.program_id(2)
    q4 = _stack_heads(q_ref)
    sc = _nt_dot(kc_ref[...], q4)

    def finish(ot, den):
        _store_heads(ot * (1.0 / den), o_ref)

    @pl.when(qt < lat_q_tiles)
    def _():
        sl = _nt_dot(kl_ref[...], q4)
        m = jnp.maximum(jnp.max(sl, axis=0, keepdims=True), jnp.max(sc, axis=0, keepdims=True))
        p_l = jnp.exp2(sl - m)
        p_c = jnp.exp2(sc - m)
        den = jnp.sum(p_l, axis=0, keepdims=True) + jnp.sum(p_c, axis=0, keepdims=True)
        finish(jnp.dot(vtl_ref[...], p_l.astype(BF16), preferred_element_type=F32)
               + jnp.dot(vtc_ref[...], p_c.astype(BF16), preferred_element_type=F32), den)

    @pl.when(qt >= lat_q_tiles)
    def _():
        p_c = jnp.exp2(sc - jnp.max(sc, axis=0, keepdims=True))
        finish(jnp.dot(vtc_ref[...], p_c.astype(BF16), preferred_element_type=F32),
               jnp.sum(p_c, axis=0, keepdims=True))


def _global_attention(q, k, vt, n_batch, seq, ctx_len):
    n, d = q.shape
    lat_q = seq // Q_TILE
    ctx_q = ctx_len // Q_TILE
    ctx_blk0 = (n_batch * seq) // ctx_len
    gw = KV_REP * HEAD_DIM

    def qrow(b, g, t):
        return jnp.where(t < lat_q, b * lat_q + t, n_batch * lat_q + b * ctx_q + (t - lat_q))

    return pl.pallas_call(
        functools.partial(_global_attn_kernel, lat_q),
        grid=(n_batch, N_KV_HEADS, lat_q + ctx_q),
        in_specs=[pl.BlockSpec((Q_TILE, gw), lambda b, g, t: (qrow(b, g, t), g)),
                  pl.BlockSpec((seq, HEAD_DIM), lambda b, g, t: (b, g)),
                  pl.BlockSpec((ctx_len, HEAD_DIM), lambda b, g, t: (ctx_blk0 + b, g)),
                  pl.BlockSpec((HEAD_DIM, seq), lambda b, g, t: (g, b)),
                  pl.BlockSpec((HEAD_DIM, ctx_len), lambda b, g, t: (g, ctx_blk0 + b))],
        out_specs=pl.BlockSpec((Q_TILE, gw), lambda b, g, t: (qrow(b, g, t), g)),
        out_shape=jax.ShapeDtypeStruct((n, d), BF16),
        compiler_params=_cparams(("arbitrary", "arbitrary", "arbitrary")),
    )(q, k, k, vt, vt)


def _window_attn_kernel(seq, q_ref, kl_ref, kc_ref, vtl_ref, vtc_ref, sink_ref, o_ref):
    qt = pl.program_id(2)
    span = Q_TILE + 2 * WINDOW
    start = pl.multiple_of(jnp.clip(qt * Q_TILE - WINDOW, 0, seq - span), Q_TILE)
    q4 = _stack_heads(q_ref)
    cols = KV_REP * Q_TILE
    s_w = _nt_dot(kl_ref[pl.ds(start, span), :], q4)
    s_c = _nt_dot(kc_ref[...], q4)
    kpos = start + lax.broadcasted_iota(jnp.int32, (span, 1), 0)
    qpos = qt * Q_TILE + lax.broadcasted_iota(jnp.int32, (1, cols), 1) % Q_TILE
    s_w = jnp.where(jnp.abs(qpos - kpos) <= WINDOW, s_w, NEG_INF)
    sink = sink_ref[0]
    m = jnp.maximum(jnp.maximum(jnp.max(s_w, axis=0, keepdims=True),
                                jnp.max(s_c, axis=0, keepdims=True)), sink)
    p_w = jnp.exp2(s_w - m)
    p_c = jnp.exp2(s_c - m)
    den = (jnp.sum(p_w, axis=0, keepdims=True) + jnp.sum(p_c, axis=0, keepdims=True)
           + jnp.exp2(sink - m))
    ot = (jnp.dot(vtl_ref[:, pl.ds(start, span)], p_w.astype(BF16), preferred_element_type=F32)
          + jnp.dot(vtc_ref[...], p_c.astype(BF16), preferred_element_type=F32))
    _store_heads(ot * (1.0 / den), o_ref)


def _window_attention(q, k, vt, sink_cols, n_batch, seq, ctx_len):
    n, d = q.shape
    lat_q = seq // Q_TILE
    ctx_blk0 = (n_batch * seq) // ctx_len
    gw = KV_REP * HEAD_DIM
    return pl.pallas_call(
        functools.partial(_window_attn_kernel, seq),
        grid=(n_batch, N_KV_HEADS, lat_q),
        in_specs=[pl.BlockSpec((Q_TILE, gw), lambda b, g, t: (b * lat_q + t, g)),
                  pl.BlockSpec((seq, HEAD_DIM), lambda b, g, t: (b, g)),
                  pl.BlockSpec((ctx_len, HEAD_DIM), lambda b, g, t: (ctx_blk0 + b, g)),
                  pl.BlockSpec((HEAD_DIM, seq), lambda b, g, t: (g, b)),
                  pl.BlockSpec((HEAD_DIM, ctx_len), lambda b, g, t: (g, ctx_blk0 + b)),
                  pl.BlockSpec((1, 1, KV_REP * Q_TILE), lambda b, g, t: (g, 0, 0))],
        out_specs=pl.BlockSpec((Q_TILE, gw), lambda b, g, t: (b * lat_q + t, g)),
        out_shape=jax.ShapeDtypeStruct((n, d), BF16),
        compiler_params=_cparams(("arbitrary", "arbitrary", "arbitrary")),
    )(q, k, k, vt, vt, sink_cols)


def _peer_pre_kernel(o_ref, x_ref, wo_ref, g1_ref, n2_ref, sh_ref, sc_ref, wq_ref, keys_ref,
                     xn_ref, h2t_ref, st_ref):
    y = jnp.dot(o_ref[...], wo_ref[...], preferred_element_type=F32)
    xn = x_ref[...] + g1_ref[0] * y
    xn_ref[...] = xn
    h2 = _rms(xn) * n2_ref[...]
    h2 = h2 * (1.0 + sc_ref[0]) + sh_ref[0]
    h2t_ref[...] = h2.T.astype(BF16)
    qp = jnp.dot(h2.astype(BF16), wq_ref[...], preferred_element_type=F32)
    half = PEER_QDIM // 2
    for hs in range(2 * PEER_HEADS):
        qs = qp[:, hs * half:(hs + 1) * half].astype(BF16)
        st_ref[hs] = _nt_dot(keys_ref[hs % 2], qs)


def _peer_pre(o, xs, mods, layer, w_o, n2, w_q, keys, n_tiles, lat_tiles, tiles_per_seq, n_batch):
    n, d = xs.shape
    t = ROW_TILE
    qd = PEER_HEADS * PEER_QDIM

    def mrow(i):
        return layer * MOD_ROWS + jnp.where(i < lat_tiles, i // tiles_per_seq, n_batch)

    return pl.pallas_call(
        _peer_pre_kernel,
        grid=(n_tiles,),
        in_specs=[pl.BlockSpec((t, d), lambda i: (i, 0)),
                  pl.BlockSpec((t, d), lambda i: (i, 0)),
                  pl.BlockSpec((d, d), lambda i: (0, 0)),
                  pl.BlockSpec((1, 1, d), lambda i: (mrow(i), 0, 2)),
                  pl.BlockSpec((1, d), lambda i: (0, 0)),
                  pl.BlockSpec((1, 1, d), lambda i: (mrow(i), 0, 3)),
                  pl.BlockSpec((1, 1, d), lambda i: (mrow(i), 0, 4)),
                  pl.BlockSpec((d, qd), lambda i: (0, 0)),
                  pl.BlockSpec((2, N_KEYS, PEER_QDIM // 2), lambda i: (0, 0, 0))],
        out_specs=[pl.BlockSpec((t, d), lambda i: (i, 0)),
                   pl.BlockSpec((d, t), lambda i: (0, i)),
                   pl.BlockSpec((2 * PEER_HEADS, N_KEYS, t), lambda i: (0, 0, i))],
        out_shape=[jax.ShapeDtypeStruct((n, d), F32),
                   jax.ShapeDtypeStruct((d, n), BF16),
                   jax.ShapeDtypeStruct((2 * PEER_HEADS, N_KEYS, n), F32)],
        compiler_params=_cparams(("arbitrary",)),
    )(o, xs, w_o, mods, n2, mods, mods, w_q, keys)


def _sorting_network(n):
    def merge(lo, hi, r):
        step = r * 2
        if step < hi - lo:
            yield from merge(lo, hi, step)
            yield from merge(lo + r, hi, step)
            yield from [(i, i + r) for i in range(lo + r, hi - r, step)]
        else:
            yield (lo, lo + r)

    def sort(lo, hi):
        if hi - lo >= 1:
            mid = lo + (hi - lo) // 2
            yield from sort(lo, mid)
            yield from sort(mid + 1, hi)
            yield from merge(lo, hi, 1)

    return list(sort(0, n - 1))


def _exchange(x, i, j):
    x[i], x[j] = jnp.maximum(x[i], x[j]), jnp.minimum(x[i], x[j])


def _top_values(s):
    k = PEER_TOPK
    assert s.shape[0] == k * SUBLANES
    x = [s[i * SUBLANES:(i + 1) * SUBLANES] for i in range(k)]
    for i, j in _sorting_network(k):
        _exchange(x, i, j)
    shift = SUBLANES // 2
    while shift:
        x = [jnp.maximum(x[i], pltpu.roll(x[k - 1 - i], shift, 0)) for i in range(k)]
        d = k // 2
        while d:
            for i in range(k):
                if not i & d:
                    _exchange(x, i, i + d)
            d //= 2
        shift //= 2
    return [xi[0:1] for xi in x]


def _kth_largest(s, k):
    for _ in range(k - 1):
        s = jnp.where(s == jnp.max(s, axis=0, keepdims=True), -jnp.inf, s)
    return jnp.max(s, axis=0, keepdims=True)


def _candidate_sums(v1, v2):
    assert PEER_TOPK == 16
    a1 = jnp.concatenate(v1, axis=0)
    a2 = jnp.concatenate(v2, axis=0)
    row = lax.broadcasted_iota(jnp.int32, (8, a1.shape[1]), 0)
    ninf = -jnp.inf
    return jnp.concatenate([
        v1[0] + a2,
        v1[1] + a2[0:8],
        jnp.where(row < 5, v1[2] + a2[0:8], ninf),
        jnp.where(row < 4, v1[3] + a2[0:8], ninf),
        a1[8:16] + v2[0],
        jnp.where(row >= 4, a1[0:8] + v2[0], ninf),
        jnp.where(row >= 4, a1[0:8] + v2[1], ninf),
        jnp.where(row == 4, a1[0:8] + v2[2], ninf),
    ], axis=0)


def _peer_topk_kernel(st_ref, s2_ref, e2_ref, thr_ref, e1_ref):
    def per_head(h, carry):
        s1 = st_ref[2 * h]
        s2 = st_ref[2 * h + 1]
        v1 = _top_values(s1)
        v2 = _top_values(s2)
        cand = _candidate_sums(v1, v2)
        thr = _kth_largest(cand, PEER_TOPK)
        top = v1[0] + v2[0]
        z = jnp.sum(jnp.where(cand >= thr, jnp.exp(cand - top), 0.0), axis=0, keepdims=True)
        guard = thr - jnp.abs(thr) * THRESHOLD_SLACK
        s2_ref[h] = s2
        e2_ref[h] = jnp.exp(s2 - v2[0]) * (1.0 / z)
        thr_ref[h] = guard - s1
        e1_ref[h] = jnp.exp(s1 - v1[0])
        return carry

    lax.fori_loop(0, PEER_HEADS, per_head, 0)


def _peer_topk(st, n_tiles):
    _, _, n = st.shape
    t = ROW_TILE
    spec = pl.BlockSpec((PEER_HEADS, N_KEYS, t), lambda i: (0, 0, i))
    shape = jax.ShapeDtypeStruct((PEER_HEADS, N_KEYS, n), F32)
    return pl.pallas_call(
        _peer_topk_kernel,
        grid=(n_tiles,),
        in_specs=[pl.BlockSpec((2 * PEER_HEADS, N_KEYS, t), lambda i: (0, 0, i))],
        out_specs=[spec, spec, spec, spec],
        out_shape=[shape, shape, shape, shape],
        compiler_params=_cparams(("arbitrary",)),
    )(st)


def _gelu_tanh(x):
    k = -2.0 * math.sqrt(2.0 / math.pi)
    return x / (1.0 + jnp.exp(x * (k + (k * 0.044715) * (x * x))))


def _peer_dense_kernel(h2t_ref, s2_ref, e2_ref, thr_ref, e1_ref, u_ref, v_ref, xn_ref, g2_ref,
                       out_ref, acc_ref, act_ref, slab_ref):
    j = pl.program_id(1)
    lanes = acc_ref.shape[1]
    lane_blocks = lanes // SUB_LANES

    @pl.when(j == 0)
    def _():
        acc_ref[...] = jnp.zeros_like(acc_ref)

    act_ref[...] = jnp.dot(u_ref[...], h2t_ref[...], preferred_element_type=F32)

    def gate_block(blk, carry):
        bc = blk // lane_blocks
        lc = blk % lane_blocks
        rows = pl.ds(pl.multiple_of(bc * GATE_ROWS, GATE_ROWS), GATE_ROWS)
        cols = pl.ds(pl.multiple_of(lc * SUB_LANES, SUB_LANES), SUB_LANES)
        w = [jnp.zeros((GATE_ROWS, SUB_LANES), F32) for _ in range(PEER_CHUNK)]
        for h in range(PEER_HEADS):
            s2 = s2_ref[h, rows, cols]
            e2 = e2_ref[h, rows, cols]
            thr = thr_ref[h, pl.ds(j * PEER_CHUNK, PEER_CHUNK), cols]
            e1 = e1_ref[h, pl.ds(j * PEER_CHUNK, PEER_CHUNK), cols]
            for al in range(PEER_CHUNK):
                w[al] = w[al] + jnp.where(s2 >= thr[al:al + 1], e2, 0.0) * e1[al:al + 1]
        for al in range(PEER_CHUNK):
            arows = pl.ds(pl.multiple_of(al * N_KEYS + bc * GATE_ROWS, GATE_ROWS), GATE_ROWS)
            slab_ref[arows, cols] = (w[al] * _gelu_tanh(act_ref[arows, cols])).astype(BF16)
        return carry

    lax.fori_loop(0, (N_KEYS // GATE_ROWS) * lane_blocks, gate_block, 0)
    acc_ref[...] += lax.dot_general(v_ref[...], slab_ref[...], (((0,), (0,)), ((), ())),
                                    preferred_element_type=F32)

    @pl.when(j == pl.num_programs(1) - 1)
    def _():
        out_ref[...] = xn_ref[...] + g2_ref[0] * acc_ref[...].T


def _peer_dense(h2t, coefs, u, v, xn, mods, layer, n_tiles, lat_tiles, tiles_per_seq, n_batch):
    d = xn.shape[1]
    t = PEER_TILE
    ce = PEER_CHUNK * N_KEYS
    n_exp = u.shape[0]

    def mrow(i):
        return layer * MOD_ROWS + jnp.where(i < lat_tiles, i // tiles_per_seq, n_batch)

    cspec = pl.BlockSpec((PEER_HEADS, N_KEYS, t), lambda i, j: (0, 0, i))
    return pl.pallas_call(
        _peer_dense_kernel,
        grid=(n_tiles, n_exp // ce),
        in_specs=[pl.BlockSpec((d, t), lambda i, j: (0, i)),
                  cspec, cspec, cspec, cspec,
                  pl.BlockSpec((ce, d), lambda i, j: (j, 0)),
                  pl.BlockSpec((ce, d), lambda i, j: (j, 0)),
                  pl.BlockSpec((t, d), lambda i, j: (i, 0)),
                  pl.BlockSpec((1, 1, d), lambda i, j: (mrow(i), 0, 5))],
        out_specs=pl.BlockSpec((t, d), lambda i, j: (i, 0)),
        out_shape=jax.ShapeDtypeStruct((n_tiles * t, d), F32),
        scratch_shapes=[pltpu.VMEM((d, t), F32),
                        pltpu.VMEM((ce, t), F32),
                        pltpu.VMEM((ce, t), BF16)],
        compiler_params=_cparams(("arbitrary", "arbitrary")),
    )(h2t, *coefs, u, v, xn, mods)


def _rope_tables(seq, ctx_len):
    pos = jnp.arange(seq)
    row = (pos // GRID_W).astype(F32)
    col = (pos % GRID_W).astype(F32)
    half = HEAD_DIM // 2
    inv = ROPE_THETA ** (-jnp.arange(0, half, 2, dtype=F32) / half)
    ar = row[:, None] * inv
    ac = col[:, None] * inv
    cos = jnp.concatenate([jnp.cos(ar), jnp.cos(ar), jnp.cos(ac), jnp.cos(ac)], axis=-1)
    sin = jnp.concatenate([-jnp.sin(ar), jnp.sin(ar), -jnp.sin(ac), jnp.sin(ac)], axis=-1)
    cos = jnp.concatenate([cos, jnp.ones((ctx_len, HEAD_DIM), F32)], axis=0)
    sin = jnp.concatenate([sin, jnp.zeros((ctx_len, HEAD_DIM), F32)], axis=0)
    return cos, sin


def kernel(x, c, ctx, c_ctx, ada_w, ada_b, norm1_gain, norm2_gain, w_qkv, q_norm_gain, k_norm_gain,
           w_o, attn_sinks, peer_w_q, peer_sub_keys, peer_u, peer_v):
    n_batch, seq, d = x.shape
    ctx_len = ctx.shape[1]
    depth = ada_w.shape[0]
    n_lat = n_batch * seq
    n_ctx = n_batch * ctx_len
    assert depth == 2 and d == D_MODEL and n_batch < MOD_ROWS
    assert seq % PEER_TILE == 0 and n_ctx % PEER_TILE == 0 and ctx_len == ROW_TILE
    assert seq >= Q_TILE + 2 * WINDOW and n_lat % ctx_len == 0

    cc = jnp.concatenate([c, c_ctx[None, :], jnp.zeros((MOD_ROWS - n_batch - 1, d), F32)], axis=0)
    mods = _modulation(cc, ada_w, ada_b).reshape(depth * MOD_ROWS, 1, N_MOD * d)
    cos_t, sin_t = _rope_tables(seq, ctx_len)
    xs = jnp.concatenate([x.reshape(n_lat, d), ctx.reshape(n_ctx, d)], axis=0)
    scale = HEAD_DIM ** -0.5 * math.log2(math.e)

    for layer in range(depth):
        last = layer == depth - 1
        q, k, vt = _qkv(xs, mods, layer, norm1_gain[layer][None, :], w_qkv[layer].astype(BF16),
                       (q_norm_gain[layer] * scale)[None, :], k_norm_gain[layer][None, :],
                       cos_t, sin_t, n_lat // ROW_TILE, seq // ROW_TILE, n_batch)
        if layer % 2 == 0:
            o = _global_attention(q, k, vt, n_batch, seq, ctx_len)
        else:
            sink = (attn_sinks[layer // 2] * math.log2(math.e)).reshape(N_KV_HEADS, 1, KV_REP, 1)
            sink_cols = jnp.broadcast_to(sink, (N_KV_HEADS, 1, KV_REP, Q_TILE))
            o = _window_attention(q, k, vt, sink_cols.reshape(N_KV_HEADS, 1, KV_REP * Q_TILE),
                                  n_batch, seq, ctx_len)
        n_rows = n_lat if last else n_lat + n_ctx
        xn, h2t, st = _peer_pre(o, xs, mods, layer, w_o[layer].astype(BF16),
                                norm2_gain[layer][None, :], peer_w_q[layer].astype(BF16),
                                peer_sub_keys[layer].astype(BF16), n_rows // ROW_TILE,
                                n_lat // ROW_TILE, seq // ROW_TILE, n_batch)
        coefs = _peer_topk(st, n_rows // ROW_TILE)
        xs = _peer_dense(h2t, coefs, peer_u[layer].astype(BF16), peer_v[layer].astype(BF16),
                         xn, mods, layer, n_rows // PEER_TILE, n_lat // PEER_TILE,
                         seq // PEER_TILE, n_batch)
    return xs[:n_lat].reshape(n_batch, seq, d)
```

```python
import functools
import math

import jax
import jax.numpy as jnp
from jax import lax
from jax.experimental import pallas as pl
from jax.experimental.pallas import tpu as pltpu

F32 = jnp.float32
BF16 = jnp.bfloat16

D_MODEL = 1024
N_HEADS = 8
N_KV_HEADS = 2
HEAD_DIM = 128
KV_REP = N_HEADS // N_KV_HEADS
QKV_DIM = (N_HEADS + 2 * N_KV_HEADS) * HEAD_DIM
GRID_W = 64
WINDOW = 128
ROPE_THETA = 10000.0
RMS_EPS = 1e-6
NEG_INF = -1e30
PEER_HEADS = 8
PEER_TOPK = 16
N_KEYS = 128
PEER_QDIM = 256
N_MOD = 6
MOD_ROWS = 8

ROW_TILE = 256
Q_TILE = 128
PEER_TILE = 512
PEER_CHUNK = 8
SUB_LANES = 128
GATE_ROWS = 32
SUBLANES = 8
THRESHOLD_SLACK = 2.0 ** -22
VMEM_LIMIT = 56 * 1024 * 1024


def _cparams(sem):
    return pltpu.CompilerParams(dimension_semantics=sem, vmem_limit_bytes=VMEM_LIMIT)


def _nt_dot(a, b):
    return lax.dot_general(a, b, (((1,), (1,)), ((), ())), preferred_element_type=F32)


def _mod_kernel(c_ref, w_ref, b_ref, o_ref):
    c = c_ref[...]
    a = c * (1.0 / (1.0 + jnp.exp(-c)))
    o_ref[0] = jnp.dot(a, w_ref[0], preferred_element_type=F32,
                       precision=lax.Precision.HIGHEST) + b_ref[0]


def _modulation(cc, ada_w, ada_b):
    depth, d, n = ada_w.shape
    tn = 1536
    return pl.pallas_call(
        _mod_kernel,
        grid=(depth, n // tn),
        in_specs=[pl.BlockSpec((MOD_ROWS, d), lambda l, j: (0, 0)),
                  pl.BlockSpec((1, d, tn), lambda l, j: (l, 0, j)),
                  pl.BlockSpec((1, 1, tn), lambda l, j: (l, 0, j))],
        out_specs=pl.BlockSpec((1, MOD_ROWS, tn), lambda l, j: (l, 0, j)),
        out_shape=jax.ShapeDtypeStruct((depth, MOD_ROWS, n), F32),
        compiler_params=_cparams(("arbitrary", "arbitrary")),
    )(cc, ada_w, ada_b.reshape(depth, 1, n))


def _rms(x):
    return x * lax.rsqrt(jnp.mean(x * x, axis=-1, keepdims=True) + RMS_EPS)


def _qkv_kernel(x_ref, sh_ref, sc_ref, g_ref, w_ref, qg_ref, kg_ref, cos_ref, sin_ref,
                q_ref, k_ref, vt_ref):
    h = _rms(x_ref[...]) * g_ref[...]
    h = h * (1.0 + sc_ref[0]) + sh_ref[0]
    y = jnp.dot(h.astype(BF16), w_ref[...], preferred_element_type=F32)
    cos = cos_ref[...]
    sin = sin_ref[...]
    lane = lax.broadcasted_iota(jnp.int32, cos.shape, 1)
    first = (lane % (HEAD_DIM // 2)) < (HEAD_DIM // 4)

    def head(col, gain):
        n = _rms(y[:, col * HEAD_DIM:(col + 1) * HEAD_DIM]) * gain
        sw = jnp.where(first, pltpu.roll(n, HEAD_DIM - HEAD_DIM // 4, 1),
                       pltpu.roll(n, HEAD_DIM // 4, 1))
        return n * cos + sw * sin

    for hh in range(N_HEADS):
        q_ref[:, hh * HEAD_DIM:(hh + 1) * HEAD_DIM] = head(hh, qg_ref[...]).astype(BF16)
    for g in range(N_KV_HEADS):
        k_ref[:, g * HEAD_DIM:(g + 1) * HEAD_DIM] = head(N_HEADS + g, kg_ref[...]).astype(BF16)
    v0 = (N_HEADS + N_KV_HEADS) * HEAD_DIM
    vt_ref[...] = y[:, v0:].T.astype(BF16)


def _qkv(xs, mods, layer, gain, w, qg, kg, cos_t, sin_t, lat_tiles, tiles_per_seq, n_batch):
    n, d = xs.shape
    t = ROW_TILE

    def mrow(i):
        return layer * MOD_ROWS + jnp.where(i < lat_tiles, i // tiles_per_seq, n_batch)

    def trow(i):
        return jnp.where(i < lat_tiles, i % tiles_per_seq, tiles_per_seq)

    kvd = N_KV_HEADS * HEAD_DIM
    return pl.pallas_call(
        _qkv_kernel,
        grid=(n // t,),
        in_specs=[pl.BlockSpec((t, d), lambda i: (i, 0)),
                  pl.BlockSpec((1, 1, d), lambda i: (mrow(i), 0, 0)),
                  pl.BlockSpec((1, 1, d), lambda i: (mrow(i), 0, 1)),
                  pl.BlockSpec((1, d), lambda i: (0, 0)),
                  pl.BlockSpec((d, QKV_DIM), lambda i: (0, 0)),
                  pl.BlockSpec((1, HEAD_DIM), lambda i: (0, 0)),
                  pl.BlockSpec((1, HEAD_DIM), lambda i: (0, 0)),
                  pl.BlockSpec((t, HEAD_DIM), lambda i: (trow(i), 0)),
                  pl.BlockSpec((t, HEAD_DIM), lambda i: (trow(i), 0))],
        out_specs=[pl.BlockSpec((t, d), lambda i: (i, 0)),
                   pl.BlockSpec((t, kvd), lambda i: (i, 0)),
                   pl.BlockSpec((kvd, t), lambda i: (0, i))],
        out_shape=[jax.ShapeDtypeStruct((n, d), BF16),
                   jax.ShapeDtypeStruct((n, kvd), BF16),
                   jax.ShapeDtypeStruct((kvd, n), BF16)],
        compiler_params=_cparams(("arbitrary",)),
    )(xs, mods, mods, gain, w, qg, kg, cos_t, sin_t)


def _stack_heads(q_ref):
    return jnp.concatenate(
        [q_ref[:, r * HEAD_DIM:(r + 1) * HEAD_DIM] for r in range(KV_REP)], axis=0)


def _store_heads(ot, o_ref):
    for r in range(KV_REP):
        blk = ot[:, r * Q_TILE:(r + 1) * Q_TILE]
        o_ref[:, r * HEAD_DIM:(r + 1) * HEAD_DIM] = blk.T.astype(BF16)


def _global_attn_kernel(lat_q_tiles, q_ref, kl_ref, kc_ref, vtl_ref, vtc_ref, o_ref):
    qt = pl.program_id(2)
    q4 = _stack_heads(q_ref)
    sc = _nt_dot(kc_ref[...], q4)

    def finish(ot, den):
        _store_heads(ot * (1.0 / den), o_ref)

    @pl.when(qt < lat_q_tiles)
    def _():
        sl = _nt_dot(kl_ref[...], q4)
        m = jnp.maximum(jnp.max(sl, axis=0, keepdims=True), jnp.max(sc, axis=0, keepdims=True))
        p_l = jnp.exp2(sl - m)
        p_c = jnp.exp2(sc - m)
        den = jnp.sum(p_l, axis=0, keepdims=True) + jnp.sum(p_c, axis=0, keepdims=True)
        finish(jnp.dot(vtl_ref[...], p_l.astype(BF16), preferred_element_type=F32)
               + jnp.dot(vtc_ref[...], p_c.astype(BF16), preferred_element_type=F32), den)

    @pl.when(qt >= lat_q_tiles)
    def _():
        p_c = jnp.exp2(sc - jnp.max(sc, axis=0, keepdims=True))
        finish(jnp.dot(vtc_ref[...], p_c.astype(BF16), preferred_element_type=F32),
               jnp.sum(p_c, axis=0, keepdims=True))


def _global_attention(q, k, vt, n_batch, seq, ctx_len):
    n, d = q.shape
    lat_q = seq // Q_TILE
    ctx_q = ctx_len // Q_TILE
    ctx_blk0 = (n_batch * seq) // ctx_len
    gw = KV_REP * HEAD_DIM

    def qrow(b, g, t):
        return jnp.where(t < lat_q, b * lat_q + t, n_batch * lat_q + b * ctx_q + (t - lat_q))

    return pl.pallas_call(
        functools.partial(_global_attn_kernel, lat_q),
        grid=(n_batch, N_KV_HEADS, lat_q + ctx_q),
        in_specs=[pl.BlockSpec((Q_TILE, gw), lambda b, g, t: (qrow(b, g, t), g)),
                  pl.BlockSpec((seq, HEAD_DIM), lambda b, g, t: (b, g)),
                  pl.BlockSpec((ctx_len, HEAD_DIM), lambda b, g, t: (ctx_blk0 + b, g)),
                  pl.BlockSpec((HEAD_DIM, seq), lambda b, g, t: (g, b)),
                  pl.BlockSpec((HEAD_DIM, ctx_len), lambda b, g, t: (g, ctx_blk0 + b))],
        out_specs=pl.BlockSpec((Q_TILE, gw), lambda b, g, t: (qrow(b, g, t), g)),
        out_shape=jax.ShapeDtypeStruct((n, d), BF16),
        compiler_params=_cparams(("arbitrary", "arbitrary", "arbitrary")),
    )(q, k, k, vt, vt)


def _window_attn_kernel(seq, q_ref, kl_ref, kc_ref, vtl_ref, vtc_ref, sink_ref, o_ref):
    qt = pl.program_id(2)
    span = Q_TILE + 2 * WINDOW
    start = pl.multiple_of(jnp.clip(qt * Q_TILE - WINDOW, 0, seq - span), Q_TILE)
    q4 = _stack_heads(q_ref)
    cols = KV_REP * Q_TILE
    s_w = _nt_dot(kl_ref[pl.ds(start, span), :], q4)
    s_c = _nt_dot(kc_ref[...], q4)
    kpos = start + lax.broadcasted_iota(jnp.int32, (span, 1), 0)
    qpos = qt * Q_TILE + lax.broadcasted_iota(jnp.int32, (1, cols), 1) % Q_TILE
    s_w = jnp.where(jnp.abs(qpos - kpos) <= WINDOW, s_w, NEG_INF)
    sink = sink_ref[0]
    m = jnp.maximum(jnp.maximum(jnp.max(s_w, axis=0, keepdims=True),
                                jnp.max(s_c, axis=0, keepdims=True)), sink)
    p_w = jnp.exp2(s_w - m)
    p_c = jnp.exp2(s_c - m)
    den = (jnp.sum(p_w, axis=0, keepdims=True) + jnp.sum(p_c, axis=0, keepdims=True)
           + jnp.exp2(sink - m))
    ot = (jnp.dot(vtl_ref[:, pl.ds(start, span)], p_w.astype(BF16), preferred_element_type=F32)
          + jnp.dot(vtc_ref[...], p_c.astype(BF16), preferred_element_type=F32))
    _store_heads(ot * (1.0 / den), o_ref)


def _window_attention(q, k, vt, sink_cols, n_batch, seq, ctx_len):
    n, d = q.shape
    lat_q = seq // Q_TILE
    ctx_blk0 = (n_batch * seq) // ctx_len
    gw = KV_REP * HEAD_DIM
    return pl.pallas_call(
        functools.partial(_window_attn_kernel, seq),
        grid=(n_batch, N_KV_HEADS, lat_q),
        in_specs=[pl.BlockSpec((Q_TILE, gw), lambda b, g, t: (b * lat_q + t, g)),
                  pl.BlockSpec((seq, HEAD_DIM), lambda b, g, t: (b, g)),
                  pl.BlockSpec((ctx_len, HEAD_DIM), lambda b, g, t: (ctx_blk0 + b, g)),
                  pl.BlockSpec((HEAD_DIM, seq), lambda b, g, t: (g, b)),
                  pl.BlockSpec((HEAD_DIM, ctx_len), lambda b, g, t: (g, ctx_blk0 + b)),
                  pl.BlockSpec((1, 1, KV_REP * Q_TILE), lambda b, g, t: (g, 0, 0))],
        out_specs=pl.BlockSpec((Q_TILE, gw), lambda b, g, t: (b * lat_q + t, g)),
        out_shape=jax.ShapeDtypeStruct((n, d), BF16),
        compiler_params=_cparams(("arbitrary", "arbitrary", "arbitrary")),
    )(q, k, k, vt, vt, sink_cols)


def _peer_pre_kernel(o_ref, x_ref, wo_ref, g1_ref, n2_ref, sh_ref, sc_ref, wq_ref, keys_ref,
                     xn_ref, h2t_ref, st_ref):
    y = jnp.dot(o_ref[...], wo_ref[...], preferred_element_type=F32)
    xn = x_ref[...] + g1_ref[0] * y
    xn_ref[...] = xn
    h2 = _rms(xn) * n2_ref[...]
    h2 = h2 * (1.0 + sc_ref[0]) + sh_ref[0]
    h2t_ref[...] = h2.T.astype(BF16)
    qp = jnp.dot(h2.astype(BF16), wq_ref[...], preferred_element_type=F32)
    half = PEER_QDIM // 2
    for hs in range(2 * PEER_HEADS):
        qs = qp[:, hs * half:(hs + 1) * half].astype(BF16)
        st_ref[hs] = _nt_dot(keys_ref[hs % 2], qs)


def _peer_pre(o, xs, mods, layer, w_o, n2, w_q, keys, n_tiles, lat_tiles, tiles_per_seq, n_batch):
    n, d = xs.shape
    t = ROW_TILE
    qd = PEER_HEADS * PEER_QDIM

    def mrow(i):
        return layer * MOD_ROWS + jnp.where(i < lat_tiles, i // tiles_per_seq, n_batch)

    return pl.pallas_call(
        _peer_pre_kernel,
        grid=(n_tiles,),
        in_specs=[pl.BlockSpec((t, d), lambda i: (i, 0)),
                  pl.BlockSpec((t, d), lambda i: (i, 0)),
                  pl.BlockSpec((d, d), lambda i: (0, 0)),
                  pl.BlockSpec((1, 1, d), lambda i: (mrow(i), 0, 2)),
                  pl.BlockSpec((1, d), lambda i: (0, 0)),
                  pl.BlockSpec((1, 1, d), lambda i: (mrow(i), 0, 3)),
                  pl.BlockSpec((1, 1, d), lambda i: (mrow(i), 0, 4)),
                  pl.BlockSpec((d, qd), lambda i: (0, 0)),
                  pl.BlockSpec((2, N_KEYS, PEER_QDIM // 2), lambda i: (0, 0, 0))],
        out_specs=[pl.BlockSpec((t, d), lambda i: (i, 0)),
                   pl.BlockSpec((d, t), lambda i: (0, i)),
                   pl.BlockSpec((2 * PEER_HEADS, N_KEYS, t), lambda i: (0, 0, i))],
        out_shape=[jax.ShapeDtypeStruct((n, d), F32),
                   jax.ShapeDtypeStruct((d, n), BF16),
                   jax.ShapeDtypeStruct((2 * PEER_HEADS, N_KEYS, n), F32)],
        compiler_params=_cparams(("arbitrary",)),
    )(o, xs, w_o, mods, n2, mods, mods, w_q, keys)


def _sorting_network(n):
    def merge(lo, hi, r):
        step = r * 2
        if step < hi - lo:
            yield from merge(lo, hi, step)
            yield from merge(lo + r, hi, step)
            yield from [(i, i + r) for i in range(lo + r, hi - r, step)]
        else:
            yield (lo, lo + r)

    def sort(lo, hi):
        if hi - lo >= 1:
            mid = lo + (hi - lo) // 2
            yield from sort(lo, mid)
            yield from sort(mid + 1, hi)
            yield from merge(lo, hi, 1)

    return list(sort(0, n - 1))


def _exchange(x, i, j):
    x[i], x[j] = jnp.maximum(x[i], x[j]), jnp.minimum(x[i], x[j])


def _top_values(s):
    k = PEER_TOPK
    assert s.shape[0] == k * SUBLANES
    x = [s[i * SUBLANES:(i + 1) * SUBLANES] for i in range(k)]
    for i, j in _sorting_network(k):
        _exchange(x, i, j)
    shift = SUBLANES // 2
    while shift:
        x = [jnp.maximum(x[i], pltpu.roll(x[k - 1 - i], shift, 0)) for i in range(k)]
        d = k // 2
        while d:
            for i in range(k):
                if not i & d:
                    _exchange(x, i, i + d)
            d //= 2
        shift //= 2
    return [xi[0:1] for xi in x]


def _kth_largest(s, k):
    for _ in range(k - 1):
        s = jnp.where(s == jnp.max(s, axis=0, keepdims=True), -jnp.inf, s)
    return jnp.max(s, axis=0, keepdims=True)


def _candidate_sums(v1, v2):
    assert PEER_TOPK == 16
    a1 = jnp.concatenate(v1, axis=0)
    a2 = jnp.concatenate(v2, axis=0)
    row = lax.broadcasted_iota(jnp.int32, (8, a1.shape[1]), 0)
    ninf = -jnp.inf
    return jnp.concatenate([
        v1[0] + a2,
        v1[1] + a2[0:8],
        jnp.where(row < 5, v1[2] + a2[0:8], ninf),
        jnp.where(row < 4, v1[3] + a2[0:8], ninf),
        a1[8:16] + v2[0],
        jnp.where(row >= 4, a1[0:8] + v2[0], ninf),
        jnp.where(row >= 4, a1[0:8] + v2[1], ninf),
        jnp.where(row == 4, a1[0:8] + v2[2], ninf),
    ], axis=0)


def _peer_topk_kernel(st_ref, s2_ref, e2_ref, thr_ref, e1_ref):
    def per_head(h, carry):
        s1 = st_ref[2 * h]
        s2 = st_ref[2 * h + 1]
        v1 = _top_values(s1)
        v2 = _top_values(s2)
        cand = _candidate_sums(v1, v2)
        thr = _kth_largest(cand, PEER_TOPK)
        top = v1[0] + v2[0]
        z = jnp.sum(jnp.where(cand >= thr, jnp.exp(cand - top), 0.0), axis=0, keepdims=True)
        guard = thr - jnp.abs(thr) * THRESHOLD_SLACK
        s2_ref[h] = s2
        e2_ref[h] = jnp.exp(s2 - v2[0]) * (1.0 / z)
        thr_ref[h] = guard - s1
        e1_ref[h] = jnp.exp(s1 - v1[0])
        return carry

    lax.fori_loop(0, PEER_HEADS, per_head, 0)


def _peer_topk(st, n_tiles):
    _, _, n = st.shape
    t = ROW_TILE
    spec = pl.BlockSpec((PEER_HEADS, N_KEYS, t), lambda i: (0, 0, i))
    shape = jax.ShapeDtypeStruct((PEER_HEADS, N_KEYS, n), F32)
    return pl.pallas_call(
        _peer_topk_kernel,
        grid=(n_tiles,),
        in_specs=[pl.BlockSpec((2 * PEER_HEADS, N_KEYS, t), lambda i: (0, 0, i))],
        out_specs=[spec, spec, spec, spec],
        out_shape=[shape, shape, shape, shape],
        compiler_params=_cparams(("arbitrary",)),
    )(st)


def _gelu_tanh(x):
    k = -2.0 * math.sqrt(2.0 / math.pi)
    return x / (1.0 + jnp.exp(x * (k + (k * 0.044715) * (x * x))))


def _peer_dense_kernel(h2t_ref, s2_ref, e2_ref, thr_ref, e1_ref, u_ref, v_ref, xn_ref, g2_ref,
                       out_ref, acc_ref, act_ref, slab_ref):
    j = pl.program_id(1)
    lanes = acc_ref.shape[1]
    lane_blocks = lanes // SUB_LANES

    @pl.when(j == 0)
    def _():
        acc_ref[...] = jnp.zeros_like(acc_ref)

    act_ref[...] = jnp.dot(u_ref[0], h2t_ref[...], preferred_element_type=F32)

    def gate_block(blk, carry):
        bc = blk // lane_blocks
        lc = blk % lane_blocks
        rows = pl.ds(pl.multiple_of(bc * GATE_ROWS, GATE_ROWS), GATE_ROWS)
        cols = pl.ds(pl.multiple_of(lc * SUB_LANES, SUB_LANES), SUB_LANES)
        w = [jnp.zeros((GATE_ROWS, SUB_LANES), F32) for _ in range(PEER_CHUNK)]
        for h in range(PEER_HEADS):
            s2 = s2_ref[h, rows, cols]
            e2 = e2_ref[h, rows, cols]
            thr = thr_ref[h, pl.ds(j * PEER_CHUNK, PEER_CHUNK), cols]
            e1 = e1_ref[h, pl.ds(j * PEER_CHUNK, PEER_CHUNK), cols]
            for al in range(PEER_CHUNK):
                w[al] = w[al] + jnp.where(s2 >= thr[al:al + 1], e2, 0.0) * e1[al:al + 1]
        for al in range(PEER_CHUNK):
            arows = pl.ds(pl.multiple_of(al * N_KEYS + bc * GATE_ROWS, GATE_ROWS), GATE_ROWS)
            slab_ref[arows, cols] = (w[al] * _gelu_tanh(act_ref[arows, cols])).astype(BF16)
        return carry

    lax.fori_loop(0, (N_KEYS // GATE_ROWS) * lane_blocks, gate_block, 0)
    acc_ref[...] += lax.dot_general(v_ref[0], slab_ref[...], (((0,), (0,)), ((), ())),
                                    preferred_element_type=F32)

    @pl.when(j == pl.num_programs(1) - 1)
    def _():
        out_ref[...] = xn_ref[...] + g2_ref[0] * acc_ref[...].T


def _peer_dense(h2t, coefs, u, v, xn, mods, layer, n_tiles, lat_tiles, tiles_per_seq, n_batch):
    d = xn.shape[1]
    t = PEER_TILE
    ce = PEER_CHUNK * N_KEYS
    n_exp = u.shape[1]

    def mrow(i):
        return layer * MOD_ROWS + jnp.where(i < lat_tiles, i // tiles_per_seq, n_batch)

    cspec = pl.BlockSpec((PEER_HEADS, N_KEYS, t), lambda i, j: (0, 0, i))
    return pl.pallas_call(
        _peer_dense_kernel,
        grid=(n_tiles, n_exp // ce),
        in_specs=[pl.BlockSpec((d, t), lambda i, j: (0, i)),
                  cspec, cspec, cspec, cspec,
                  pl.BlockSpec((1, ce, d), lambda i, j: (layer, j, 0)),
                  pl.BlockSpec((1, ce, d), lambda i, j: (layer, j, 0)),
                  pl.BlockSpec((t, d), lambda i, j: (i, 0)),
                  pl.BlockSpec((1, 1, d), lambda i, j: (mrow(i), 0, 5))],
        out_specs=pl.BlockSpec((t, d), lambda i, j: (i, 0)),
        out_shape=jax.ShapeDtypeStruct((n_tiles * t, d), F32),
        scratch_shapes=[pltpu.VMEM((d, t), F32),
                        pltpu.VMEM((ce, t), F32),
                        pltpu.VMEM((ce, t), BF16)],
        compiler_params=_cparams(("arbitrary", "arbitrary")),
    )(h2t, *coefs, u, v, xn, mods)


def _rope_tables(seq, ctx_len):
    pos = jnp.arange(seq)
    row = (pos // GRID_W).astype(F32)
    col = (pos % GRID_W).astype(F32)
    half = HEAD_DIM // 2
    inv = ROPE_THETA ** (-jnp.arange(0, half, 2, dtype=F32) / half)
    ar = row[:, None] * inv
    ac = col[:, None] * inv
    cos = jnp.concatenate([jnp.cos(ar), jnp.cos(ar), jnp.cos(ac), jnp.cos(ac)], axis=-1)
    sin = jnp.concatenate([-jnp.sin(ar), jnp.sin(ar), -jnp.sin(ac), jnp.sin(ac)], axis=-1)
    cos = jnp.concatenate([cos, jnp.ones((ctx_len, HEAD_DIM), F32)], axis=0)
    sin = jnp.concatenate([sin, jnp.zeros((ctx_len, HEAD_DIM), F32)], axis=0)
    return cos, sin


def kernel(x, c, ctx, c_ctx, ada_w, ada_b, norm1_gain, norm2_gain, w_qkv, q_norm_gain, k_norm_gain,
           w_o, attn_sinks, peer_w_q, peer_sub_keys, peer_u, peer_v):
    n_batch, seq, d = x.shape
    ctx_len = ctx.shape[1]
    depth = ada_w.shape[0]
    n_lat = n_batch * seq
    n_ctx = n_batch * ctx_len
    assert depth == 2 and d == D_MODEL and n_batch < MOD_ROWS
    assert seq % PEER_TILE == 0 and n_ctx % PEER_TILE == 0 and ctx_len == ROW_TILE
    assert seq >= Q_TILE + 2 * WINDOW and n_lat % ctx_len == 0

    cc = jnp.concatenate([c, c_ctx[None, :], jnp.zeros((MOD_ROWS - n_batch - 1, d), F32)], axis=0)
    mods = _modulation(cc, ada_w, ada_b).reshape(depth * MOD_ROWS, 1, N_MOD * d)
    cos_t, sin_t = _rope_tables(seq, ctx_len)
    u_tab = peer_u.astype(BF16)
    v_tab = peer_v.astype(BF16)
    xs = jnp.concatenate([x.reshape(n_lat, d), ctx.reshape(n_ctx, d)], axis=0)
    scale = HEAD_DIM ** -0.5 * math.log2(math.e)

    for layer in range(depth):
        last = layer == depth - 1
        q, k, vt = _qkv(xs, mods, layer, norm1_gain[layer][None, :], w_qkv[layer].astype(BF16),
                       (q_norm_gain[layer] * scale)[None, :], k_norm_gain[layer][None, :],
                       cos_t, sin_t, n_lat // ROW_TILE, seq // ROW_TILE, n_batch)
        if layer % 2 == 0:
            o = _global_attention(q, k, vt, n_batch, seq, ctx_len)
        else:
            sink = (attn_sinks[layer // 2] * math.log2(math.e)).reshape(N_KV_HEADS, 1, KV_REP, 1)
            sink_cols = jnp.broadcast_to(sink, (N_KV_HEADS, 1, KV_REP, Q_TILE))
            o = _window_attention(q, k, vt, sink_cols.reshape(N_KV_HEADS, 1, KV_REP * Q_TILE),
                                  n_batch, seq, ctx_len)
        n_rows = n_lat if last else n_lat + n_ctx
        xn, h2t, st = _peer_pre(o, xs, mods, layer, w_o[layer].astype(BF16),
                                norm2_gain[layer][None, :], peer_w_q[layer].astype(BF16),
                                peer_sub_keys[layer].astype(BF16), n_rows // ROW_TILE,
                                n_lat // ROW_TILE, seq // ROW_TILE, n_batch)
        coefs = _peer_topk(st, n_rows // ROW_TILE)
        xs = _peer_dense(h2t, coefs, u_tab, v_tab,
                         xn, mods, layer, n_rows // PEER_TILE, n_lat // PEER_TILE,
                         seq // PEER_TILE, n_batch)
    return xs[:n_lat].reshape(n_batch, seq, d)
```

```python
import functools
import math

import jax
import jax.numpy as jnp
from jax import lax
from jax.experimental import pallas as pl
from jax.experimental.pallas import tpu as pltpu

F32 = jnp.float32
BF16 = jnp.bfloat16

D_MODEL = 1024
N_HEADS = 8
N_KV_HEADS = 2
HEAD_DIM = 128
KV_REP = N_HEADS // N_KV_HEADS
QKV_DIM = (N_HEADS + 2 * N_KV_HEADS) * HEAD_DIM
GRID_W = 64
WINDOW = 128
ROPE_THETA = 10000.0
RMS_EPS = 1e-6
NEG_INF = -1e30
PEER_HEADS = 8
PEER_TOPK = 16
N_KEYS = 128
PEER_QDIM = 256
N_MOD = 6
MOD_ROWS = 8

ROW_TILE = 256
Q_TILE = 128
PEER_TILE = 512
PEER_CHUNK = 16
GATE_KEYS = 8
SUB_LANES = 128
GATE_ROWS = 32
SUBLANES = 8
THRESHOLD_SLACK = 2.0 ** -22
VMEM_LIMIT = 56 * 1024 * 1024


def _cparams(sem):
    return pltpu.CompilerParams(dimension_semantics=sem, vmem_limit_bytes=VMEM_LIMIT)


def _nt_dot(a, b):
    return lax.dot_general(a, b, (((1,), (1,)), ((), ())), preferred_element_type=F32)


def _mod_kernel(c_ref, w_ref, b_ref, o_ref):
    c = c_ref[...]
    a = c * (1.0 / (1.0 + jnp.exp(-c)))
    o_ref[0] = jnp.dot(a, w_ref[0], preferred_element_type=F32,
                       precision=lax.Precision.HIGHEST) + b_ref[0]


def _modulation(cc, ada_w, ada_b):
    depth, d, n = ada_w.shape
    tn = 1536
    return pl.pallas_call(
        _mod_kernel,
        grid=(depth, n // tn),
        in_specs=[pl.BlockSpec((MOD_ROWS, d), lambda l, j: (0, 0)),
                  pl.BlockSpec((1, d, tn), lambda l, j: (l, 0, j)),
                  pl.BlockSpec((1, 1, tn), lambda l, j: (l, 0, j))],
        out_specs=pl.BlockSpec((1, MOD_ROWS, tn), lambda l, j: (l, 0, j)),
        out_shape=jax.ShapeDtypeStruct((depth, MOD_ROWS, n), F32),
        compiler_params=_cparams(("arbitrary", "arbitrary")),
    )(cc, ada_w, ada_b.reshape(depth, 1, n))


def _rms(x):
    return x * lax.rsqrt(jnp.mean(x * x, axis=-1, keepdims=True) + RMS_EPS)


def _qkv_kernel(x_ref, sh_ref, sc_ref, g_ref, w_ref, qg_ref, kg_ref, cos_ref, sin_ref,
                q_ref, k_ref, vt_ref):
    h = _rms(x_ref[...]) * g_ref[...]
    h = h * (1.0 + sc_ref[0]) + sh_ref[0]
    y = jnp.dot(h.astype(BF16), w_ref[...], preferred_element_type=F32)
    cos = cos_ref[...]
    sin = sin_ref[...]
    lane = lax.broadcasted_iota(jnp.int32, cos.shape, 1)
    first = (lane % (HEAD_DIM // 2)) < (HEAD_DIM // 4)

    def head(col, gain):
        n = _rms(y[:, col * HEAD_DIM:(col + 1) * HEAD_DIM]) * gain
        sw = jnp.where(first, pltpu.roll(n, HEAD_DIM - HEAD_DIM // 4, 1),
                       pltpu.roll(n, HEAD_DIM // 4, 1))
        return n * cos + sw * sin

    for hh in range(N_HEADS):
        q_ref[:, hh * HEAD_DIM:(hh + 1) * HEAD_DIM] = head(hh, qg_ref[...]).astype(BF16)
    for g in range(N_KV_HEADS):
        k_ref[:, g * HEAD_DIM:(g + 1) * HEAD_DIM] = head(N_HEADS + g, kg_ref[...]).astype(BF16)
    v0 = (N_HEADS + N_KV_HEADS) * HEAD_DIM
    vt_ref[...] = y[:, v0:].T.astype(BF16)


def _qkv(xs, mods, layer, gain, w, qg, kg, cos_t, sin_t, lat_tiles, tiles_per_seq, n_batch):
    n, d = xs.shape
    t = ROW_TILE

    def mrow(i):
        return layer * MOD_ROWS + jnp.where(i < lat_tiles, i // tiles_per_seq, n_batch)

    def trow(i):
        return jnp.where(i < lat_tiles, i % tiles_per_seq, tiles_per_seq)

    kvd = N_KV_HEADS * HEAD_DIM
    return pl.pallas_call(
        _qkv_kernel,
        grid=(n // t,),
        in_specs=[pl.BlockSpec((t, d), lambda i: (i, 0)),
                  pl.BlockSpec((1, 1, d), lambda i: (mrow(i), 0, 0)),
                  pl.BlockSpec((1, 1, d), lambda i: (mrow(i), 0, 1)),
                  pl.BlockSpec((1, d), lambda i: (0, 0)),
                  pl.BlockSpec((d, QKV_DIM), lambda i: (0, 0)),
                  pl.BlockSpec((1, HEAD_DIM), lambda i: (0, 0)),
                  pl.BlockSpec((1, HEAD_DIM), lambda i: (0, 0)),
                  pl.BlockSpec((t, HEAD_DIM), lambda i: (trow(i), 0)),
                  pl.BlockSpec((t, HEAD_DIM), lambda i: (trow(i), 0))],
        out_specs=[pl.BlockSpec((t, d), lambda i: (i, 0)),
                   pl.BlockSpec((t, kvd), lambda i: (i, 0)),
                   pl.BlockSpec((kvd, t), lambda i: (0, i))],
        out_shape=[jax.ShapeDtypeStruct((n, d), BF16),
                   jax.ShapeDtypeStruct((n, kvd), BF16),
                   jax.ShapeDtypeStruct((kvd, n), BF16)],
        compiler_params=_cparams(("arbitrary",)),
    )(xs, mods, mods, gain, w, qg, kg, cos_t, sin_t)


def _stack_heads(q_ref):
    return jnp.concatenate(
        [q_ref[:, r * HEAD_DIM:(r + 1) * HEAD_DIM] for r in range(KV_REP)], axis=0)


def _store_heads(ot, o_ref):
    for r in range(KV_REP):
        blk = ot[:, r * Q_TILE:(r + 1) * Q_TILE]
        o_ref[:, r * HEAD_DIM:(r + 1) * HEAD_DIM] = blk.T.astype(BF16)


def _global_attn_kernel(lat_q_tiles, q_ref, kl_ref, kc_ref, vtl_ref, vtc_ref, o_ref):
    qt = pl.program_id(2)
    q4 = _stack_heads(q_ref)
    sc = _nt_dot(kc_ref[...], q4)

    def finish(ot, den):
        _store_heads(ot * (1.0 / den), o_ref)

    @pl.when(qt < lat_q_tiles)
    def _():
        sl = _nt_dot(kl_ref[...], q4)
        m = jnp.maximum(jnp.max(sl, axis=0, keepdims=True), jnp.max(sc, axis=0, keepdims=True))
        p_l = jnp.exp2(sl - m)
        p_c = jnp.exp2(sc - m)
        den = jnp.sum(p_l, axis=0, keepdims=True) + jnp.sum(p_c, axis=0, keepdims=True)
        finish(jnp.dot(vtl_ref[...], p_l.astype(BF16), preferred_element_type=F32)
               + jnp.dot(vtc_ref[...], p_c.astype(BF16), preferred_element_type=F32), den)

    @pl.when(qt >= lat_q_tiles)
    def _():
        p_c = jnp.exp2(sc - jnp.max(sc, axis=0, keepdims=True))
        finish(jnp.dot(vtc_ref[...], p_c.astype(BF16), preferred_element_type=F32),
               jnp.sum(p_c, axis=0, keepdims=True))


def _global_attention(q, k, vt, n_batch, seq, ctx_len):
    n, d = q.shape
    lat_q = seq // Q_TILE
    ctx_q = ctx_len // Q_TILE
    ctx_blk0 = (n_batch * seq) // ctx_len
    gw = KV_REP * HEAD_DIM

    def qrow(b, g, t):
        return jnp.where(t < lat_q, b * lat_q + t, n_batch * lat_q + b * ctx_q + (t - lat_q))

    return pl.pallas_call(
        functools.partial(_global_attn_kernel, lat_q),
        grid=(n_batch, N_KV_HEADS, lat_q + ctx_q),
        in_specs=[pl.BlockSpec((Q_TILE, gw), lambda b, g, t: (qrow(b, g, t), g)),
                  pl.BlockSpec((seq, HEAD_DIM), lambda b, g, t: (b, g)),
                  pl.BlockSpec((ctx_len, HEAD_DIM), lambda b, g, t: (ctx_blk0 + b, g)),
                  pl.BlockSpec((HEAD_DIM, seq), lambda b, g, t: (g, b)),
                  pl.BlockSpec((HEAD_DIM, ctx_len), lambda b, g, t: (g, ctx_blk0 + b))],
        out_specs=pl.BlockSpec((Q_TILE, gw), lambda b, g, t: (qrow(b, g, t), g)),
        out_shape=jax.ShapeDtypeStruct((n, d), BF16),
        compiler_params=_cparams(("arbitrary", "arbitrary", "arbitrary")),
    )(q, k, k, vt, vt)


def _window_attn_kernel(seq, q_ref, kl_ref, kc_ref, vtl_ref, vtc_ref, sink_ref, o_ref):
    qt = pl.program_id(2)
    span = Q_TILE + 2 * WINDOW
    start = pl.multiple_of(jnp.clip(qt * Q_TILE - WINDOW, 0, seq - span), Q_TILE)
    q4 = _stack_heads(q_ref)
    cols = KV_REP * Q_TILE
    s_w = _nt_dot(kl_ref[pl.ds(start, span), :], q4)
    s_c = _nt_dot(kc_ref[...], q4)
    kpos = start + lax.broadcasted_iota(jnp.int32, (span, 1), 0)
    qpos = qt * Q_TILE + lax.broadcasted_iota(jnp.int32, (1, cols), 1) % Q_TILE
    s_w = jnp.where(jnp.abs(qpos - kpos) <= WINDOW, s_w, NEG_INF)
    sink = sink_ref[0]
    m = jnp.maximum(jnp.maximum(jnp.max(s_w, axis=0, keepdims=True),
                                jnp.max(s_c, axis=0, keepdims=True)), sink)
    p_w = jnp.exp2(s_w - m)
    p_c = jnp.exp2(s_c - m)
    den = (jnp.sum(p_w, axis=0, keepdims=True) + jnp.sum(p_c, axis=0, keepdims=True)
           + jnp.exp2(sink - m))
    ot = (jnp.dot(vtl_ref[:, pl.ds(start, span)], p_w.astype(BF16), preferred_element_type=F32)
          + jnp.dot(vtc_ref[...], p_c.astype(BF16), preferred_element_type=F32))
    _store_heads(ot * (1.0 / den), o_ref)


def _window_attention(q, k, vt, sink_cols, n_batch, seq, ctx_len):
    n, d = q.shape
    lat_q = seq // Q_TILE
    ctx_blk0 = (n_batch * seq) // ctx_len
    gw = KV_REP * HEAD_DIM
    return pl.pallas_call(
        functools.partial(_window_attn_kernel, seq),
        grid=(n_batch, N_KV_HEADS, lat_q),
        in_specs=[pl.BlockSpec((Q_TILE, gw), lambda b, g, t: (b * lat_q + t, g)),
                  pl.BlockSpec((seq, HEAD_DIM), lambda b, g, t: (b, g)),
                  pl.BlockSpec((ctx_len, HEAD_DIM), lambda b, g, t: (ctx_blk0 + b, g)),
                  pl.BlockSpec((HEAD_DIM, seq), lambda b, g, t: (g, b)),
                  pl.BlockSpec((HEAD_DIM, ctx_len), lambda b, g, t: (g, ctx_blk0 + b)),
                  pl.BlockSpec((1, 1, KV_REP * Q_TILE), lambda b, g, t: (g, 0, 0))],
        out_specs=pl.BlockSpec((Q_TILE, gw), lambda b, g, t: (b * lat_q + t, g)),
        out_shape=jax.ShapeDtypeStruct((n, d), BF16),
        compiler_params=_cparams(("arbitrary", "arbitrary", "arbitrary")),
    )(q, k, k, vt, vt, sink_cols)


def _peer_pre_kernel(o_ref, x_ref, wo_ref, g1_ref, n2_ref, sh_ref, sc_ref, wq_ref, keys_ref,
                     xn_ref, h2t_ref, st_ref):
    y = jnp.dot(o_ref[...], wo_ref[...], preferred_element_type=F32)
    xn = x_ref[...] + g1_ref[0] * y
    xn_ref[...] = xn
    h2 = _rms(xn) * n2_ref[...]
    h2 = h2 * (1.0 + sc_ref[0]) + sh_ref[0]
    h2t_ref[...] = h2.T.astype(BF16)
    qp = jnp.dot(h2.astype(BF16), wq_ref[...], preferred_element_type=F32)
    half = PEER_QDIM // 2
    for hs in range(2 * PEER_HEADS):
        qs = qp[:, hs * half:(hs + 1) * half].astype(BF16)
        st_ref[hs] = _nt_dot(keys_ref[hs % 2], qs)


def _peer_pre(o, xs, mods, layer, w_o, n2, w_q, keys, n_tiles, lat_tiles, tiles_per_seq, n_batch):
    n, d = xs.shape
    t = ROW_TILE
    qd = PEER_HEADS * PEER_QDIM

    def mrow(i):
        return layer * MOD_ROWS + jnp.where(i < lat_tiles, i // tiles_per_seq, n_batch)

    return pl.pallas_call(
        _peer_pre_kernel,
        grid=(n_tiles,),
        in_specs=[pl.BlockSpec((t, d), lambda i: (i, 0)),
                  pl.BlockSpec((t, d), lambda i: (i, 0)),
                  pl.BlockSpec((d, d), lambda i: (0, 0)),
                  pl.BlockSpec((1, 1, d), lambda i: (mrow(i), 0, 2)),
                  pl.BlockSpec((1, d), lambda i: (0, 0)),
                  pl.BlockSpec((1, 1, d), lambda i: (mrow(i), 0, 3)),
                  pl.BlockSpec((1, 1, d), lambda i: (mrow(i), 0, 4)),
                  pl.BlockSpec((d, qd), lambda i: (0, 0)),
                  pl.BlockSpec((2, N_KEYS, PEER_QDIM // 2), lambda i: (0, 0, 0))],
        out_specs=[pl.BlockSpec((t, d), lambda i: (i, 0)),
                   pl.BlockSpec((d, t), lambda i: (0, i)),
                   pl.BlockSpec((2 * PEER_HEADS, N_KEYS, t), lambda i: (0, 0, i))],
        out_shape=[jax.ShapeDtypeStruct((n, d), F32),
                   jax.ShapeDtypeStruct((d, n), BF16),
                   jax.ShapeDtypeStruct((2 * PEER_HEADS, N_KEYS, n), F32)],
        compiler_params=_cparams(("arbitrary",)),
    )(o, xs, w_o, mods, n2, mods, mods, w_q, keys)


def _sorting_network(n):
    def merge(lo, hi, r):
        step = r * 2
        if step < hi - lo:
            yield from merge(lo, hi, step)
            yield from merge(lo + r, hi, step)
            yield from [(i, i + r) for i in range(lo + r, hi - r, step)]
        else:
            yield (lo, lo + r)

    def sort(lo, hi):
        if hi - lo >= 1:
            mid = lo + (hi - lo) // 2
            yield from sort(lo, mid)
            yield from sort(mid + 1, hi)
            yield from merge(lo, hi, 1)

    return list(sort(0, n - 1))


def _exchange(x, i, j):
    x[i], x[j] = jnp.maximum(x[i], x[j]), jnp.minimum(x[i], x[j])


def _top_values(s):
    k = PEER_TOPK
    assert s.shape[0] == k * SUBLANES
    x = [s[i * SUBLANES:(i + 1) * SUBLANES] for i in range(k)]
    for i, j in _sorting_network(k):
        _exchange(x, i, j)
    shift = SUBLANES // 2
    while shift:
        x = [jnp.maximum(x[i], pltpu.roll(x[k - 1 - i], shift, 0)) for i in range(k)]
        d = k // 2
        while d:
            for i in range(k):
                if not i & d:
                    _exchange(x, i, i + d)
            d //= 2
        shift //= 2
    return [xi[0:1] for xi in x]


def _kth_largest(s, k):
    for _ in range(k - 1):
        s = jnp.where(s == jnp.max(s, axis=0, keepdims=True), -jnp.inf, s)
    return jnp.max(s, axis=0, keepdims=True)


def _candidate_sums(v1, v2):
    assert PEER_TOPK == 16
    a1 = jnp.concatenate(v1, axis=0)
    a2 = jnp.concatenate(v2, axis=0)
    row = lax.broadcasted_iota(jnp.int32, (8, a1.shape[1]), 0)
    ninf = -jnp.inf
    return jnp.concatenate([
        v1[0] + a2,
        v1[1] + a2[0:8],
        jnp.where(row < 5, v1[2] + a2[0:8], ninf),
        jnp.where(row < 4, v1[3] + a2[0:8], ninf),
        a1[8:16] + v2[0],
        jnp.where(row >= 4, a1[0:8] + v2[0], ninf),
        jnp.where(row >= 4, a1[0:8] + v2[1], ninf),
        jnp.where(row == 4, a1[0:8] + v2[2], ninf),
    ], axis=0)


def _peer_topk_kernel(st_ref, s2_ref, e2_ref, thr_ref, e1_ref):
    def per_head(h, carry):
        s1 = st_ref[2 * h]
        s2 = st_ref[2 * h + 1]
        v1 = _top_values(s1)
        v2 = _top_values(s2)
        cand = _candidate_sums(v1, v2)
        thr = _kth_largest(cand, PEER_TOPK)
        top = v1[0] + v2[0]
        z = jnp.sum(jnp.where(cand >= thr, jnp.exp(cand - top), 0.0), axis=0, keepdims=True)
        guard = thr - jnp.abs(thr) * THRESHOLD_SLACK
        s2_ref[h] = s2
        e2_ref[h] = jnp.exp(s2 - v2[0]) * (1.0 / z)
        thr_ref[h] = guard - s1
        e1_ref[h] = jnp.exp(s1 - v1[0])
        return carry

    lax.fori_loop(0, PEER_HEADS, per_head, 0)


def _peer_topk(st, n_tiles):
    _, _, n = st.shape
    t = ROW_TILE
    spec = pl.BlockSpec((PEER_HEADS, N_KEYS, t), lambda i: (0, 0, i))
    shape = jax.ShapeDtypeStruct((PEER_HEADS, N_KEYS, n), F32)
    return pl.pallas_call(
        _peer_topk_kernel,
        grid=(n_tiles,),
        in_specs=[pl.BlockSpec((2 * PEER_HEADS, N_KEYS, t), lambda i: (0, 0, i))],
        out_specs=[spec, spec, spec, spec],
        out_shape=[shape, shape, shape, shape],
        compiler_params=_cparams(("arbitrary",)),
    )(st)


def _gelu_tanh(x):
    k = -2.0 * math.sqrt(2.0 / math.pi)
    return x / (1.0 + jnp.exp(x * (k + (k * 0.044715) * (x * x))))


def _peer_dense_kernel(h2t_ref, s2_ref, e2_ref, thr_ref, e1_ref, u_ref, v_ref, xn_ref, g2_ref,
                       out_ref, acc_ref, act_ref, slab_ref):
    j = pl.program_id(1)
    lanes = acc_ref.shape[1]
    lane_blocks = lanes // SUB_LANES
    row_blocks = N_KEYS // GATE_ROWS

    @pl.when(j == 0)
    def _():
        acc_ref[...] = jnp.zeros_like(acc_ref)

    act_ref[...] = jnp.dot(u_ref[0], h2t_ref[...], preferred_element_type=F32)

    def gate_block(blk, carry):
        grp = blk // (row_blocks * lane_blocks)
        bc = (blk // lane_blocks) % row_blocks
        lc = blk % lane_blocks
        rows = pl.ds(pl.multiple_of(bc * GATE_ROWS, GATE_ROWS), GATE_ROWS)
        cols = pl.ds(pl.multiple_of(lc * SUB_LANES, SUB_LANES), SUB_LANES)
        a0 = pl.multiple_of(j * PEER_CHUNK + grp * GATE_KEYS, GATE_KEYS)
        w = [jnp.zeros((GATE_ROWS, SUB_LANES), F32) for _ in range(GATE_KEYS)]
        for h in range(PEER_HEADS):
            s2 = s2_ref[h, rows, cols]
            e2 = e2_ref[h, rows, cols]
            thr = thr_ref[h, pl.ds(a0, GATE_KEYS), cols]
            e1 = e1_ref[h, pl.ds(a0, GATE_KEYS), cols]
            for al in range(GATE_KEYS):
                w[al] = w[al] + jnp.where(s2 >= thr[al:al + 1], e2, 0.0) * e1[al:al + 1]
        for al in range(GATE_KEYS):
            r0 = (grp * GATE_KEYS + al) * N_KEYS + bc * GATE_ROWS
            arows = pl.ds(pl.multiple_of(r0, GATE_ROWS), GATE_ROWS)
            slab_ref[arows, cols] = (w[al] * _gelu_tanh(act_ref[arows, cols])).astype(BF16)
        return carry

    lax.fori_loop(0, (PEER_CHUNK // GATE_KEYS) * row_blocks * lane_blocks, gate_block, 0)
    acc_ref[...] += lax.dot_general(v_ref[0], slab_ref[...], (((0,), (0,)), ((), ())),
                                    preferred_element_type=F32)

    @pl.when(j == pl.num_programs(1) - 1)
    def _():
        out_ref[...] = xn_ref[...] + g2_ref[0] * acc_ref[...].T


def _peer_dense(h2t, coefs, u, v, xn, mods, layer, n_tiles, lat_tiles, tiles_per_seq, n_batch):
    d = xn.shape[1]
    t = PEER_TILE
    ce = PEER_CHUNK * N_KEYS
    n_exp = u.shape[1]

    def mrow(i):
        return layer * MOD_ROWS + jnp.where(i < lat_tiles, i // tiles_per_seq, n_batch)

    cspec = pl.BlockSpec((PEER_HEADS, N_KEYS, t), lambda i, j: (0, 0, i))
    return pl.pallas_call(
        _peer_dense_kernel,
        grid=(n_tiles, n_exp // ce),
        in_specs=[pl.BlockSpec((d, t), lambda i, j: (0, i)),
                  cspec, cspec, cspec, cspec,
                  pl.BlockSpec((1, ce, d), lambda i, j: (layer, j, 0)),
                  pl.BlockSpec((1, ce, d), lambda i, j: (layer, j, 0)),
                  pl.BlockSpec((t, d), lambda i, j: (i, 0)),
                  pl.BlockSpec((1, 1, d), lambda i, j: (mrow(i), 0, 5))],
        out_specs=pl.BlockSpec((t, d), lambda i, j: (i, 0)),
        out_shape=jax.ShapeDtypeStruct((n_tiles * t, d), F32),
        scratch_shapes=[pltpu.VMEM((d, t), F32),
                        pltpu.VMEM((ce, t), F32),
                        pltpu.VMEM((ce, t), BF16)],
        compiler_params=_cparams(("arbitrary", "arbitrary")),
    )(h2t, *coefs, u, v, xn, mods)


def _rope_tables(seq, ctx_len):
    pos = jnp.arange(seq)
    row = (pos // GRID_W).astype(F32)
    col = (pos % GRID_W).astype(F32)
    half = HEAD_DIM // 2
    inv = ROPE_THETA ** (-jnp.arange(0, half, 2, dtype=F32) / half)
    ar = row[:, None] * inv
    ac = col[:, None] * inv
    cos = jnp.concatenate([jnp.cos(ar), jnp.cos(ar), jnp.cos(ac), jnp.cos(ac)], axis=-1)
    sin = jnp.concatenate([-jnp.sin(ar), jnp.sin(ar), -jnp.sin(ac), jnp.sin(ac)], axis=-1)
    cos = jnp.concatenate([cos, jnp.ones((ctx_len, HEAD_DIM), F32)], axis=0)
    sin = jnp.concatenate([sin, jnp.zeros((ctx_len, HEAD_DIM), F32)], axis=0)
    return cos, sin


def kernel(x, c, ctx, c_ctx, ada_w, ada_b, norm1_gain, norm2_gain, w_qkv, q_norm_gain, k_norm_gain,
           w_o, attn_sinks, peer_w_q, peer_sub_keys, peer_u, peer_v):
    n_batch, seq, d = x.shape
    ctx_len = ctx.shape[1]
    depth = ada_w.shape[0]
    n_lat = n_batch * seq
    n_ctx = n_batch * ctx_len
    assert depth == 2 and d == D_MODEL and n_batch < MOD_ROWS
    assert seq % PEER_TILE == 0 and n_ctx % PEER_TILE == 0 and ctx_len == ROW_TILE
    assert seq >= Q_TILE + 2 * WINDOW and n_lat % ctx_len == 0

    cc = jnp.concatenate([c, c_ctx[None, :], jnp.zeros((MOD_ROWS - n_batch - 1, d), F32)], axis=0)
    mods = _modulation(cc, ada_w, ada_b).reshape(depth * MOD_ROWS, 1, N_MOD * d)
    cos_t, sin_t = _rope_tables(seq, ctx_len)
    u_tab = peer_u.astype(BF16)
    v_tab = peer_v.astype(BF16)
    xs = jnp.concatenate([x.reshape(n_lat, d), ctx.reshape(n_ctx, d)], axis=0)
    scale = HEAD_DIM ** -0.5 * math.log2(math.e)

    for layer in range(depth):
        last = layer == depth - 1
        q, k, vt = _qkv(xs, mods, layer, norm1_gain[layer][None, :], w_qkv[layer].astype(BF16),
                       (q_norm_gain[layer] * scale)[None, :], k_norm_gain[layer][None, :],
                       cos_t, sin_t, n_lat // ROW_TILE, seq // ROW_TILE, n_batch)
        if layer % 2 == 0:
            o = _global_attention(q, k, vt, n_batch, seq, ctx_len)
        else:
            sink = (attn_sinks[layer // 2] * math.log2(math.e)).reshape(N_KV_HEADS, 1, KV_REP, 1)
            sink_cols = jnp.broadcast_to(sink, (N_KV_HEADS, 1, KV_REP, Q_TILE))
            o = _window_attention(q, k, vt, sink_cols.reshape(N_KV_HEADS, 1, KV_REP * Q_TILE),
                                  n_batch, seq, ctx_len)
        n_rows = n_lat if last else n_lat + n_ctx
        xn, h2t, st = _peer_pre(o, xs, mods, layer, w_o[layer].astype(BF16),
                                norm2_gain[layer][None, :], peer_w_q[layer].astype(BF16),
                                peer_sub_keys[layer].astype(BF16), n_rows // ROW_TILE,
                                n_lat // ROW_TILE, seq // ROW_TILE, n_batch)
        coefs = _peer_topk(st, n_rows // ROW_TILE)
        xs = _peer_dense(h2t, coefs, u_tab, v_tab,
                         xn, mods, layer, n_rows // PEER_TILE, n_lat // PEER_TILE,
                         seq // PEER_TILE, n_batch)
    return xs[:n_lat].reshape(n_batch, seq, d)
```

```python
import functools
import math

import jax
import jax.numpy as jnp
from jax import lax
from jax.experimental import pallas as pl
from jax.experimental.pallas import tpu as pltpu

F32 = jnp.float32
BF16 = jnp.bfloat16

D_MODEL = 1024
N_HEADS = 8
N_KV_HEADS = 2
HEAD_DIM = 128
KV_REP = N_HEADS // N_KV_HEADS
QKV_DIM = (N_HEADS + 2 * N_KV_HEADS) * HEAD_DIM
GRID_W = 64
WINDOW = 128
ROPE_THETA = 10000.0
RMS_EPS = 1e-6
NEG_INF = -1e30
PEER_HEADS = 8
PEER_TOPK = 16
N_KEYS = 128
PEER_QDIM = 256
N_MOD = 6
MOD_ROWS = 8

ROW_TILE = 256
Q_TILE = 256
PEER_TILE = 512
PEER_CHUNK = 16
GATE_KEYS = 8
SUB_LANES = 128
GATE_ROWS = 32
SUBLANES = 8
THRESHOLD_SLACK = 2.0 ** -22
VMEM_LIMIT = 56 * 1024 * 1024


def _cparams(sem):
    return pltpu.CompilerParams(dimension_semantics=sem, vmem_limit_bytes=VMEM_LIMIT)


def _nt_dot(a, b):
    return lax.dot_general(a, b, (((1,), (1,)), ((), ())), preferred_element_type=F32)


def _mod_kernel(c_ref, w_ref, b_ref, o_ref):
    c = c_ref[...]
    a = c * (1.0 / (1.0 + jnp.exp(-c)))
    o_ref[0] = jnp.dot(a, w_ref[0], preferred_element_type=F32,
                       precision=lax.Precision.HIGHEST) + b_ref[0]


def _modulation(cc, ada_w, ada_b):
    depth, d, n = ada_w.shape
    tn = 1536
    return pl.pallas_call(
        _mod_kernel,
        grid=(depth, n // tn),
        in_specs=[pl.BlockSpec((MOD_ROWS, d), lambda l, j: (0, 0)),
                  pl.BlockSpec((1, d, tn), lambda l, j: (l, 0, j)),
                  pl.BlockSpec((1, 1, tn), lambda l, j: (l, 0, j))],
        out_specs=pl.BlockSpec((1, MOD_ROWS, tn), lambda l, j: (l, 0, j)),
        out_shape=jax.ShapeDtypeStruct((depth, MOD_ROWS, n), F32),
        compiler_params=_cparams(("arbitrary", "arbitrary")),
    )(cc, ada_w, ada_b.reshape(depth, 1, n))


def _rms(x):
    return x * lax.rsqrt(jnp.mean(x * x, axis=-1, keepdims=True) + RMS_EPS)


def _qkv_kernel(x_ref, sh_ref, sc_ref, g_ref, w_ref, qg_ref, kg_ref, cos_ref, sin_ref,
                q_ref, k_ref, vt_ref):
    h = _rms(x_ref[...]) * g_ref[...]
    h = h * (1.0 + sc_ref[0]) + sh_ref[0]
    y = jnp.dot(h.astype(BF16), w_ref[...], preferred_element_type=F32)
    cos = cos_ref[...]
    sin = sin_ref[...]
    lane = lax.broadcasted_iota(jnp.int32, cos.shape, 1)
    first = (lane % (HEAD_DIM // 2)) < (HEAD_DIM // 4)

    def head(col, gain):
        n = _rms(y[:, col * HEAD_DIM:(col + 1) * HEAD_DIM]) * gain
        sw = jnp.where(first, pltpu.roll(n, HEAD_DIM - HEAD_DIM // 4, 1),
                       pltpu.roll(n, HEAD_DIM // 4, 1))
        return n * cos + sw * sin

    for hh in range(N_HEADS):
        q_ref[:, hh * HEAD_DIM:(hh + 1) * HEAD_DIM] = head(hh, qg_ref[...]).astype(BF16)
    for g in range(N_KV_HEADS):
        k_ref[:, g * HEAD_DIM:(g + 1) * HEAD_DIM] = head(N_HEADS + g, kg_ref[...]).astype(BF16)
    v0 = (N_HEADS + N_KV_HEADS) * HEAD_DIM
    vt_ref[...] = y[:, v0:].T.astype(BF16)


def _qkv(xs, mods, layer, gain, w, qg, kg, cos_t, sin_t, lat_tiles, tiles_per_seq, n_batch):
    n, d = xs.shape
    t = ROW_TILE

    def mrow(i):
        return layer * MOD_ROWS + jnp.where(i < lat_tiles, i // tiles_per_seq, n_batch)

    def trow(i):
        return jnp.where(i < lat_tiles, i % tiles_per_seq, tiles_per_seq)

    kvd = N_KV_HEADS * HEAD_DIM
    return pl.pallas_call(
        _qkv_kernel,
        grid=(n // t,),
        in_specs=[pl.BlockSpec((t, d), lambda i: (i, 0)),
                  pl.BlockSpec((1, 1, d), lambda i: (mrow(i), 0, 0)),
                  pl.BlockSpec((1, 1, d), lambda i: (mrow(i), 0, 1)),
                  pl.BlockSpec((1, d), lambda i: (0, 0)),
                  pl.BlockSpec((d, QKV_DIM), lambda i: (0, 0)),
                  pl.BlockSpec((1, HEAD_DIM), lambda i: (0, 0)),
                  pl.BlockSpec((1, HEAD_DIM), lambda i: (0, 0)),
                  pl.BlockSpec((t, HEAD_DIM), lambda i: (trow(i), 0)),
                  pl.BlockSpec((t, HEAD_DIM), lambda i: (trow(i), 0))],
        out_specs=[pl.BlockSpec((t, d), lambda i: (i, 0)),
                   pl.BlockSpec((t, kvd), lambda i: (i, 0)),
                   pl.BlockSpec((kvd, t), lambda i: (0, i))],
        out_shape=[jax.ShapeDtypeStruct((n, d), BF16),
                   jax.ShapeDtypeStruct((n, kvd), BF16),
                   jax.ShapeDtypeStruct((kvd, n), BF16)],
        compiler_params=_cparams(("arbitrary",)),
    )(xs, mods, mods, gain, w, qg, kg, cos_t, sin_t)


def _stack_heads(q_ref):
    return jnp.concatenate(
        [q_ref[:, r * HEAD_DIM:(r + 1) * HEAD_DIM] for r in range(KV_REP)], axis=0)


def _store_heads(ot, o_ref):
    for r in range(KV_REP):
        blk = ot[:, r * Q_TILE:(r + 1) * Q_TILE]
        o_ref[:, r * HEAD_DIM:(r + 1) * HEAD_DIM] = blk.T.astype(BF16)


def _global_attn_kernel(lat_q_tiles, q_ref, kl_ref, kc_ref, vtl_ref, vtc_ref, o_ref):
    qt = pl.program_id(2)
    q4 = _stack_heads(q_ref)
    sc = _nt_dot(kc_ref[...], q4)

    def finish(ot, den):
        _store_heads(ot * (1.0 / den), o_ref)

    @pl.when(qt < lat_q_tiles)
    def _():
        sl = _nt_dot(kl_ref[...], q4)
        m = jnp.maximum(jnp.max(sl, axis=0, keepdims=True), jnp.max(sc, axis=0, keepdims=True))
        p_l = jnp.exp2(sl - m)
        p_c = jnp.exp2(sc - m)
        den = jnp.sum(p_l, axis=0, keepdims=True) + jnp.sum(p_c, axis=0, keepdims=True)
        finish(jnp.dot(vtl_ref[...], p_l.astype(BF16), preferred_element_type=F32)
               + jnp.dot(vtc_ref[...], p_c.astype(BF16), preferred_element_type=F32), den)

    @pl.when(qt >= lat_q_tiles)
    def _():
        p_c = jnp.exp2(sc - jnp.max(sc, axis=0, keepdims=True))
        finish(jnp.dot(vtc_ref[...], p_c.astype(BF16), preferred_element_type=F32),
               jnp.sum(p_c, axis=0, keepdims=True))


def _global_attention(q, k, vt, n_batch, seq, ctx_len):
    n, d = q.shape
    lat_q = seq // Q_TILE
    ctx_q = ctx_len // Q_TILE
    ctx_blk0 = (n_batch * seq) // ctx_len
    gw = KV_REP * HEAD_DIM

    def qrow(b, g, t):
        return jnp.where(t < lat_q, b * lat_q + t, n_batch * lat_q + b * ctx_q + (t - lat_q))

    return pl.pallas_call(
        functools.partial(_global_attn_kernel, lat_q),
        grid=(n_batch, N_KV_HEADS, lat_q + ctx_q),
        in_specs=[pl.BlockSpec((Q_TILE, gw), lambda b, g, t: (qrow(b, g, t), g)),
                  pl.BlockSpec((seq, HEAD_DIM), lambda b, g, t: (b, g)),
                  pl.BlockSpec((ctx_len, HEAD_DIM), lambda b, g, t: (ctx_blk0 + b, g)),
                  pl.BlockSpec((HEAD_DIM, seq), lambda b, g, t: (g, b)),
                  pl.BlockSpec((HEAD_DIM, ctx_len), lambda b, g, t: (g, ctx_blk0 + b))],
        out_specs=pl.BlockSpec((Q_TILE, gw), lambda b, g, t: (qrow(b, g, t), g)),
        out_shape=jax.ShapeDtypeStruct((n, d), BF16),
        compiler_params=_cparams(("arbitrary", "arbitrary", "arbitrary")),
    )(q, k, k, vt, vt)


def _window_attn_kernel(seq, q_ref, kl_ref, kc_ref, vtl_ref, vtc_ref, sink_ref, o_ref):
    qt = pl.program_id(2)
    span = Q_TILE + 2 * WINDOW
    start = pl.multiple_of(jnp.clip(qt * Q_TILE - WINDOW, 0, seq - span), math.gcd(Q_TILE, WINDOW))
    q4 = _stack_heads(q_ref)
    cols = KV_REP * Q_TILE
    s_w = _nt_dot(kl_ref[pl.ds(start, span), :], q4)
    s_c = _nt_dot(kc_ref[...], q4)
    kpos = start + lax.broadcasted_iota(jnp.int32, (span, 1), 0)
    qpos = qt * Q_TILE + lax.broadcasted_iota(jnp.int32, (1, cols), 1) % Q_TILE
    s_w = jnp.where(jnp.abs(qpos - kpos) <= WINDOW, s_w, NEG_INF)
    sink = sink_ref[0]
    m = jnp.maximum(jnp.maximum(jnp.max(s_w, axis=0, keepdims=True),
                                jnp.max(s_c, axis=0, keepdims=True)), sink)
    p_w = jnp.exp2(s_w - m)
    p_c = jnp.exp2(s_c - m)
    den = (jnp.sum(p_w, axis=0, keepdims=True) + jnp.sum(p_c, axis=0, keepdims=True)
           + jnp.exp2(sink - m))
    ot = (jnp.dot(vtl_ref[:, pl.ds(start, span)], p_w.astype(BF16), preferred_element_type=F32)
          + jnp.dot(vtc_ref[...], p_c.astype(BF16), preferred_element_type=F32))
    _store_heads(ot * (1.0 / den), o_ref)


def _window_attention(q, k, vt, sink_cols, n_batch, seq, ctx_len):
    n, d = q.shape
    lat_q = seq // Q_TILE
    ctx_blk0 = (n_batch * seq) // ctx_len
    gw = KV_REP * HEAD_DIM
    return pl.pallas_call(
        functools.partial(_window_attn_kernel, seq),
        grid=(n_batch, N_KV_HEADS, lat_q),
        in_specs=[pl.BlockSpec((Q_TILE, gw), lambda b, g, t: (b * lat_q + t, g)),
                  pl.BlockSpec((seq, HEAD_DIM), lambda b, g, t: (b, g)),
                  pl.BlockSpec((ctx_len, HEAD_DIM), lambda b, g, t: (ctx_blk0 + b, g)),
                  pl.BlockSpec((HEAD_DIM, seq), lambda b, g, t: (g, b)),
                  pl.BlockSpec((HEAD_DIM, ctx_len), lambda b, g, t: (g, ctx_blk0 + b)),
                  pl.BlockSpec((1, 1, KV_REP * Q_TILE), lambda b, g, t: (g, 0, 0))],
        out_specs=pl.BlockSpec((Q_TILE, gw), lambda b, g, t: (b * lat_q + t, g)),
        out_shape=jax.ShapeDtypeStruct((n, d), BF16),
        compiler_params=_cparams(("arbitrary", "arbitrary", "arbitrary")),
    )(q, k, k, vt, vt, sink_cols)


def _peer_pre_kernel(o_ref, x_ref, wo_ref, g1_ref, n2_ref, sh_ref, sc_ref, wq_ref, keys_ref,
                     xn_ref, h2t_ref, st_ref):
    y = jnp.dot(o_ref[...], wo_ref[...], preferred_element_type=F32)
    xn = x_ref[...] + g1_ref[0] * y
    xn_ref[...] = xn
    h2 = _rms(xn) * n2_ref[...]
    h2 = h2 * (1.0 + sc_ref[0]) + sh_ref[0]
    h2t_ref[...] = h2.T.astype(BF16)
    qp = jnp.dot(h2.astype(BF16), wq_ref[...], preferred_element_type=F32)
    half = PEER_QDIM // 2
    for hs in range(2 * PEER_HEADS):
        qs = qp[:, hs * half:(hs + 1) * half].astype(BF16)
        st_ref[hs] = _nt_dot(keys_ref[hs % 2], qs)


def _peer_pre(o, xs, mods, layer, w_o, n2, w_q, keys, n_tiles, lat_tiles, tiles_per_seq, n_batch):
    n, d = xs.shape
    t = ROW_TILE
    qd = PEER_HEADS * PEER_QDIM

    def mrow(i):
        return layer * MOD_ROWS + jnp.where(i < lat_tiles, i // tiles_per_seq, n_batch)

    return pl.pallas_call(
        _peer_pre_kernel,
        grid=(n_tiles,),
        in_specs=[pl.BlockSpec((t, d), lambda i: (i, 0)),
                  pl.BlockSpec((t, d), lambda i: (i, 0)),
                  pl.BlockSpec((d, d), lambda i: (0, 0)),
                  pl.BlockSpec((1, 1, d), lambda i: (mrow(i), 0, 2)),
                  pl.BlockSpec((1, d), lambda i: (0, 0)),
                  pl.BlockSpec((1, 1, d), lambda i: (mrow(i), 0, 3)),
                  pl.BlockSpec((1, 1, d), lambda i: (mrow(i), 0, 4)),
                  pl.BlockSpec((d, qd), lambda i: (0, 0)),
                  pl.BlockSpec((2, N_KEYS, PEER_QDIM // 2), lambda i: (0, 0, 0))],
        out_specs=[pl.BlockSpec((t, d), lambda i: (i, 0)),
                   pl.BlockSpec((d, t), lambda i: (0, i)),
                   pl.BlockSpec((2 * PEER_HEADS, N_KEYS, t), lambda i: (0, 0, i))],
        out_shape=[jax.ShapeDtypeStruct((n, d), F32),
                   jax.ShapeDtypeStruct((d, n), BF16),
                   jax.ShapeDtypeStruct((2 * PEER_HEADS, N_KEYS, n), F32)],
        compiler_params=_cparams(("arbitrary",)),
    )(o, xs, w_o, mods, n2, mods, mods, w_q, keys)


def _sorting_network(n):
    def merge(lo, hi, r):
        step = r * 2
        if step < hi - lo:
            yield from merge(lo, hi, step)
            yield from merge(lo + r, hi, step)
            yield from [(i, i + r) for i in range(lo + r, hi - r, step)]
        else:
            yield (lo, lo + r)

    def sort(lo, hi):
        if hi - lo >= 1:
            mid = lo + (hi - lo) // 2
            yield from sort(lo, mid)
            yield from sort(mid + 1, hi)
            yield from merge(lo, hi, 1)

    return list(sort(0, n - 1))


def _exchange(x, i, j):
    x[i], x[j] = jnp.maximum(x[i], x[j]), jnp.minimum(x[i], x[j])


def _top_values(s):
    k = PEER_TOPK
    assert s.shape[0] == k * SUBLANES
    x = [s[i * SUBLANES:(i + 1) * SUBLANES] for i in range(k)]
    for i, j in _sorting_network(k):
        _exchange(x, i, j)
    shift = SUBLANES // 2
    while shift:
        x = [jnp.maximum(x[i], pltpu.roll(x[k - 1 - i], shift, 0)) for i in range(k)]
        d = k // 2
        while d:
            for i in range(k):
                if not i & d:
                    _exchange(x, i, i + d)
            d //= 2
        shift //= 2
    return [xi[0:1] for xi in x]


def _kth_largest(s, k):
    for _ in range(k - 1):
        s = jnp.where(s == jnp.max(s, axis=0, keepdims=True), -jnp.inf, s)
    return jnp.max(s, axis=0, keepdims=True)


def _candidate_sums(v1, v2):
    assert PEER_TOPK == 16
    a1 = jnp.concatenate(v1, axis=0)
    a2 = jnp.concatenate(v2, axis=0)
    row = lax.broadcasted_iota(jnp.int32, (8, a1.shape[1]), 0)
    ninf = -jnp.inf
    return jnp.concatenate([
        v1[0] + a2,
        v1[1] + a2[0:8],
        jnp.where(row < 5, v1[2] + a2[0:8], ninf),
        jnp.where(row < 4, v1[3] + a2[0:8], ninf),
        a1[8:16] + v2[0],
        jnp.where(row >= 4, a1[0:8] + v2[0], ninf),
        jnp.where(row >= 4, a1[0:8] + v2[1], ninf),
        jnp.where(row == 4, a1[0:8] + v2[2], ninf),
    ], axis=0)


def _peer_topk_kernel(st_ref, s2_ref, e2_ref, thr_ref, e1_ref):
    def per_head(h, carry):
        s1 = st_ref[2 * h]
        s2 = st_ref[2 * h + 1]
        v1 = _top_values(s1)
        v2 = _top_values(s2)
        cand = _candidate_sums(v1, v2)
        thr = _kth_largest(cand, PEER_TOPK)
        top = v1[0] + v2[0]
        z = jnp.sum(jnp.where(cand >= thr, jnp.exp(cand - top), 0.0), axis=0, keepdims=True)
        guard = thr - jnp.abs(thr) * THRESHOLD_SLACK
        s2_ref[h] = s2
        e2_ref[h] = jnp.exp(s2 - v2[0]) * (1.0 / z)
        thr_ref[h] = guard - s1
        e1_ref[h] = jnp.exp(s1 - v1[0])
        return carry

    lax.fori_loop(0, PEER_HEADS, per_head, 0)


def _peer_topk(st, n_tiles):
    _, _, n = st.shape
    t = ROW_TILE
    spec = pl.BlockSpec((PEER_HEADS, N_KEYS, t), lambda i: (0, 0, i))
    shape = jax.ShapeDtypeStruct((PEER_HEADS, N_KEYS, n), F32)
    return pl.pallas_call(
        _peer_topk_kernel,
        grid=(n_tiles,),
        in_specs=[pl.BlockSpec((2 * PEER_HEADS, N_KEYS, t), lambda i: (0, 0, i))],
        out_specs=[spec, spec, spec, spec],
        out_shape=[shape, shape, shape, shape],
        compiler_params=_cparams(("arbitrary",)),
    )(st)


def _gelu_tanh(x):
    k = -2.0 * math.sqrt(2.0 / math.pi)
    return x / (1.0 + jnp.exp(x * (k + (k * 0.044715) * (x * x))))


def _peer_dense_kernel(h2t_ref, s2_ref, e2_ref, thr_ref, e1_ref, u_ref, v_ref, xn_ref, g2_ref,
                       out_ref, acc_ref, act_ref, slab_ref):
    j = pl.program_id(1)
    lanes = acc_ref.shape[1]
    lane_blocks = lanes // SUB_LANES
    row_blocks = N_KEYS // GATE_ROWS

    @pl.when(j == 0)
    def _():
        acc_ref[...] = jnp.zeros_like(acc_ref)

    act_ref[...] = jnp.dot(u_ref[0], h2t_ref[...], preferred_element_type=F32)

    def gate_block(blk, carry):
        grp = blk // (row_blocks * lane_blocks)
        bc = (blk // lane_blocks) % row_blocks
        lc = blk % lane_blocks
        rows = pl.ds(pl.multiple_of(bc * GATE_ROWS, GATE_ROWS), GATE_ROWS)
        cols = pl.ds(pl.multiple_of(lc * SUB_LANES, SUB_LANES), SUB_LANES)
        a0 = pl.multiple_of(j * PEER_CHUNK + grp * GATE_KEYS, GATE_KEYS)
        w = [jnp.zeros((GATE_ROWS, SUB_LANES), F32) for _ in range(GATE_KEYS)]
        for h in range(PEER_HEADS):
            s2 = s2_ref[h, rows, cols]
            e2 = e2_ref[h, rows, cols]
            thr = thr_ref[h, pl.ds(a0, GATE_KEYS), cols]
            e1 = e1_ref[h, pl.ds(a0, GATE_KEYS), cols]
            for al in range(GATE_KEYS):
                w[al] = w[al] + jnp.where(s2 >= thr[al:al + 1], e2, 0.0) * e1[al:al + 1]
        for al in range(GATE_KEYS):
            r0 = (grp * GATE_KEYS + al) * N_KEYS + bc * GATE_ROWS
            arows = pl.ds(pl.multiple_of(r0, GATE_ROWS), GATE_ROWS)
            slab_ref[arows, cols] = (w[al] * _gelu_tanh(act_ref[arows, cols])).astype(BF16)
        return carry

    lax.fori_loop(0, (PEER_CHUNK // GATE_KEYS) * row_blocks * lane_blocks, gate_block, 0)
    acc_ref[...] += lax.dot_general(v_ref[0], slab_ref[...], (((0,), (0,)), ((), ())),
                                    preferred_element_type=F32)

    @pl.when(j == pl.num_programs(1) - 1)
    def _():
        out_ref[...] = xn_ref[...] + g2_ref[0] * acc_ref[...].T


def _peer_dense(h2t, coefs, u, v, xn, mods, layer, n_tiles, lat_tiles, tiles_per_seq, n_batch):
    d = xn.shape[1]
    t = PEER_TILE
    ce = PEER_CHUNK * N_KEYS
    n_exp = u.shape[1]

    def mrow(i):
        return layer * MOD_ROWS + jnp.where(i < lat_tiles, i // tiles_per_seq, n_batch)

    cspec = pl.BlockSpec((PEER_HEADS, N_KEYS, t), lambda i, j: (0, 0, i))
    return pl.pallas_call(
        _peer_dense_kernel,
        grid=(n_tiles, n_exp // ce),
        in_specs=[pl.BlockSpec((d, t), lambda i, j: (0, i)),
                  cspec, cspec, cspec, cspec,
                  pl.BlockSpec((1, ce, d), lambda i, j: (layer, j, 0)),
                  pl.BlockSpec((1, ce, d), lambda i, j: (layer, j, 0)),
                  pl.BlockSpec((t, d), lambda i, j: (i, 0)),
                  pl.BlockSpec((1, 1, d), lambda i, j: (mrow(i), 0, 5))],
        out_specs=pl.BlockSpec((t, d), lambda i, j: (i, 0)),
        out_shape=jax.ShapeDtypeStruct((n_tiles * t, d), F32),
        scratch_shapes=[pltpu.VMEM((d, t), F32),
                        pltpu.VMEM((ce, t), F32),
                        pltpu.VMEM((ce, t), BF16)],
        compiler_params=_cparams(("arbitrary", "arbitrary")),
    )(h2t, *coefs, u, v, xn, mods)


def _rope_tables(seq, ctx_len):
    pos = jnp.arange(seq)
    row = (pos // GRID_W).astype(F32)
    col = (pos % GRID_W).astype(F32)
    half = HEAD_DIM // 2
    inv = ROPE_THETA ** (-jnp.arange(0, half, 2, dtype=F32) / half)
    ar = row[:, None] * inv
    ac = col[:, None] * inv
    cos = jnp.concatenate([jnp.cos(ar), jnp.cos(ar), jnp.cos(ac), jnp.cos(ac)], axis=-1)
    sin = jnp.concatenate([-jnp.sin(ar), jnp.sin(ar), -jnp.sin(ac), jnp.sin(ac)], axis=-1)
    cos = jnp.concatenate([cos, jnp.ones((ctx_len, HEAD_DIM), F32)], axis=0)
    sin = jnp.concatenate([sin, jnp.zeros((ctx_len, HEAD_DIM), F32)], axis=0)
    return cos, sin


def kernel(x, c, ctx, c_ctx, ada_w, ada_b, norm1_gain, norm2_gain, w_qkv, q_norm_gain, k_norm_gain,
           w_o, attn_sinks, peer_w_q, peer_sub_keys, peer_u, peer_v):
    n_batch, seq, d = x.shape
    ctx_len = ctx.shape[1]
    depth = ada_w.shape[0]
    n_lat = n_batch * seq
    n_ctx = n_batch * ctx_len
    assert depth == 2 and d == D_MODEL and n_batch < MOD_ROWS
    assert seq % PEER_TILE == 0 and n_ctx % PEER_TILE == 0 and ctx_len == ROW_TILE
    assert seq >= Q_TILE + 2 * WINDOW and n_lat % ctx_len == 0

    cc = jnp.concatenate([c, c_ctx[None, :], jnp.zeros((MOD_ROWS - n_batch - 1, d), F32)], axis=0)
    mods = _modulation(cc, ada_w, ada_b).reshape(depth * MOD_ROWS, 1, N_MOD * d)
    cos_t, sin_t = _rope_tables(seq, ctx_len)
    u_tab = peer_u.astype(BF16)
    v_tab = peer_v.astype(BF16)
    xs = jnp.concatenate([x.reshape(n_lat, d), ctx.reshape(n_ctx, d)], axis=0)
    scale = HEAD_DIM ** -0.5 * math.log2(math.e)

    for layer in range(depth):
        last = layer == depth - 1
        q, k, vt = _qkv(xs, mods, layer, norm1_gain[layer][None, :], w_qkv[layer].astype(BF16),
                       (q_norm_gain[layer] * scale)[None, :], k_norm_gain[layer][None, :],
                       cos_t, sin_t, n_lat // ROW_TILE, seq // ROW_TILE, n_batch)
        if layer % 2 == 0:
            o = _global_attention(q, k, vt, n_batch, seq, ctx_len)
        else:
            sink = (attn_sinks[layer // 2] * math.log2(math.e)).reshape(N_KV_HEADS, 1, KV_REP, 1)
            sink_cols = jnp.broadcast_to(sink, (N_KV_HEADS, 1, KV_REP, Q_TILE))
            o = _window_attention(q, k, vt, sink_cols.reshape(N_KV_HEADS, 1, KV_REP * Q_TILE),
                                  n_batch, seq, ctx_len)
        n_rows = n_lat if last else n_lat + n_ctx
        xn, h2t, st = _peer_pre(o, xs, mods, layer, w_o[layer].astype(BF16),
                                norm2_gain[layer][None, :], peer_w_q[layer].astype(BF16),
                                peer_sub_keys[layer].astype(BF16), n_rows // ROW_TILE,
                                n_lat // ROW_TILE, seq // ROW_TILE, n_batch)
        coefs = _peer_topk(st, n_rows // ROW_TILE)
        xs = _peer_dense(h2t, coefs, u_tab, v_tab,
                         xn, mods, layer, n_rows // PEER_TILE, n_lat // PEER_TILE,
                         seq // PEER_TILE, n_batch)
    return xs[:n_lat].reshape(n_batch, seq, d)
```

```python
import functools
import math

import jax
import jax.numpy as jnp
from jax import lax
from jax.experimental import pallas as pl
from jax.experimental.pallas import tpu as pltpu

F32 = jnp.float32
BF16 = jnp.bfloat16

D_MODEL = 1024
N_HEADS = 8
N_KV_HEADS = 2
HEAD_DIM = 128
KV_REP = N_HEADS // N_KV_HEADS
QKV_DIM = (N_HEADS + 2 * N_KV_HEADS) * HEAD_DIM
GRID_W = 64
WINDOW = 128
ROPE_THETA = 10000.0
RMS_EPS = 1e-6
NEG_INF = -1e30
PEER_HEADS = 8
PEER_TOPK = 16
N_KEYS = 128
PEER_QDIM = 256
N_MOD = 6
MOD_ROWS = 8

ROW_TILE = 256
Q_TILE = 256
PEER_TILE = 512
PEER_CHUNK = 16
GATE_KEYS = 8
SUB_LANES = 128
GATE_ROWS = 32
SUBLANES = 8
THRESHOLD_SLACK = 2.0 ** -22
VMEM_LIMIT = 56 * 1024 * 1024


def _cparams(sem):
    return pltpu.CompilerParams(dimension_semantics=sem, vmem_limit_bytes=VMEM_LIMIT)


def _nt_dot(a, b):
    return lax.dot_general(a, b, (((1,), (1,)), ((), ())), preferred_element_type=F32)


def _mod_kernel(c_ref, w_ref, b_ref, o_ref):
    c = c_ref[...]
    a = c * (1.0 / (1.0 + jnp.exp(-c)))
    o_ref[0] = jnp.dot(a, w_ref[0], preferred_element_type=F32,
                       precision=lax.Precision.HIGHEST) + b_ref[0]


def _modulation(cc, ada_w, ada_b):
    depth, d, n = ada_w.shape
    tn = 1536
    return pl.pallas_call(
        _mod_kernel,
        grid=(depth, n // tn),
        in_specs=[pl.BlockSpec((MOD_ROWS, d), lambda l, j: (0, 0)),
                  pl.BlockSpec((1, d, tn), lambda l, j: (l, 0, j)),
                  pl.BlockSpec((1, 1, tn), lambda l, j: (l, 0, j))],
        out_specs=pl.BlockSpec((1, MOD_ROWS, tn), lambda l, j: (l, 0, j)),
        out_shape=jax.ShapeDtypeStruct((depth, MOD_ROWS, n), F32),
        compiler_params=_cparams(("arbitrary", "arbitrary")),
    )(cc, ada_w, ada_b.reshape(depth, 1, n))


def _rms(x):
    return x * lax.rsqrt(jnp.mean(x * x, axis=-1, keepdims=True) + RMS_EPS)


def _qkv_kernel(x_ref, sh_ref, sc_ref, g_ref, w_ref, qg_ref, kg_ref, cos_ref, sin_ref,
                q_ref, k_ref, vt_ref):
    h = _rms(x_ref[...]) * g_ref[...]
    h = h * (1.0 + sc_ref[0]) + sh_ref[0]
    y = jnp.dot(h.astype(BF16), w_ref[...], preferred_element_type=F32)
    cos = cos_ref[...]
    sin = sin_ref[...]
    lane = lax.broadcasted_iota(jnp.int32, cos.shape, 1)
    first = (lane % (HEAD_DIM // 2)) < (HEAD_DIM // 4)

    def head(col, gain):
        n = _rms(y[:, col * HEAD_DIM:(col + 1) * HEAD_DIM]) * gain
        sw = jnp.where(first, pltpu.roll(n, HEAD_DIM - HEAD_DIM // 4, 1),
                       pltpu.roll(n, HEAD_DIM // 4, 1))
        return n * cos + sw * sin

    for hh in range(N_HEADS):
        q_ref[:, hh * HEAD_DIM:(hh + 1) * HEAD_DIM] = head(hh, qg_ref[...]).astype(BF16)
    for g in range(N_KV_HEADS):
        k_ref[:, g * HEAD_DIM:(g + 1) * HEAD_DIM] = head(N_HEADS + g, kg_ref[...]).astype(BF16)
    v0 = (N_HEADS + N_KV_HEADS) * HEAD_DIM
    vt_ref[...] = y[:, v0:].T.astype(BF16)


def _qkv(xs, mods, layer, gain, w, qg, kg, cos_t, sin_t, lat_tiles, tiles_per_seq, n_batch):
    n, d = xs.shape
    t = ROW_TILE

    def mrow(i):
        return layer * MOD_ROWS + jnp.where(i < lat_tiles, i // tiles_per_seq, n_batch)

    def trow(i):
        return jnp.where(i < lat_tiles, i % tiles_per_seq, tiles_per_seq)

    kvd = N_KV_HEADS * HEAD_DIM
    return pl.pallas_call(
        _qkv_kernel,
        grid=(n // t,),
        in_specs=[pl.BlockSpec((t, d), lambda i: (i, 0)),
                  pl.BlockSpec((1, 1, d), lambda i: (mrow(i), 0, 0)),
                  pl.BlockSpec((1, 1, d), lambda i: (mrow(i), 0, 1)),
                  pl.BlockSpec((1, d), lambda i: (0, 0)),
                  pl.BlockSpec((d, QKV_DIM), lambda i: (0, 0)),
                  pl.BlockSpec((1, HEAD_DIM), lambda i: (0, 0)),
                  pl.BlockSpec((1, HEAD_DIM), lambda i: (0, 0)),
                  pl.BlockSpec((t, HEAD_DIM), lambda i: (trow(i), 0)),
                  pl.BlockSpec((t, HEAD_DIM), lambda i: (trow(i), 0))],
        out_specs=[pl.BlockSpec((t, d), lambda i: (i, 0)),
                   pl.BlockSpec((t, kvd), lambda i: (i, 0)),
                   pl.BlockSpec((kvd, t), lambda i: (0, i))],
        out_shape=[jax.ShapeDtypeStruct((n, d), BF16),
                   jax.ShapeDtypeStruct((n, kvd), BF16),
                   jax.ShapeDtypeStruct((kvd, n), BF16)],
        compiler_params=_cparams(("arbitrary",)),
    )(xs, mods, mods, gain, w, qg, kg, cos_t, sin_t)


def _stack_heads(q_ref):
    return jnp.concatenate(
        [q_ref[:, r * HEAD_DIM:(r + 1) * HEAD_DIM] for r in range(KV_REP)], axis=0)


def _store_heads(ot, o_ref):
    for r in range(KV_REP):
        blk = ot[:, r * Q_TILE:(r + 1) * Q_TILE]
        o_ref[:, r * HEAD_DIM:(r + 1) * HEAD_DIM] = blk.T.astype(BF16)


def _global_attn_kernel(lat_q_tiles, q_ref, kl_ref, kc_ref, vtl_ref, vtc_ref, o_ref):
    qt = pl.program_id(2)
    q4 = _stack_heads(q_ref)
    sc = _nt_dot(kc_ref[...], q4)

    def finish(ot, den):
        _store_heads(ot * (1.0 / den), o_ref)

    @pl.when(qt < lat_q_tiles)
    def _():
        sl = _nt_dot(kl_ref[...], q4)
        m = jnp.maximum(jnp.max(sl, axis=0, keepdims=True), jnp.max(sc, axis=0, keepdims=True))
        p_l = jnp.exp2(sl - m)
        p_c = jnp.exp2(sc - m)
        den = jnp.sum(p_l, axis=0, keepdims=True) + jnp.sum(p_c, axis=0, keepdims=True)
        finish(jnp.dot(vtl_ref[...], p_l.astype(BF16), preferred_element_type=F32)
               + jnp.dot(vtc_ref[...], p_c.astype(BF16), preferred_element_type=F32), den)

    @pl.when(qt >= lat_q_tiles)
    def _():
        p_c = jnp.exp2(sc - jnp.max(sc, axis=0, keepdims=True))
        finish(jnp.dot(vtc_ref[...], p_c.astype(BF16), preferred_element_type=F32),
               jnp.sum(p_c, axis=0, keepdims=True))


def _global_attention(q, k, vt, n_batch, seq, ctx_len):
    n, d = q.shape
    lat_q = seq // Q_TILE
    ctx_q = ctx_len // Q_TILE
    ctx_blk0 = (n_batch * seq) // ctx_len
    gw = KV_REP * HEAD_DIM

    def qrow(b, g, t):
        return jnp.where(t < lat_q, b * lat_q + t, n_batch * lat_q + b * ctx_q + (t - lat_q))

    return pl.pallas_call(
        functools.partial(_global_attn_kernel, lat_q),
        grid=(n_batch, N_KV_HEADS, lat_q + ctx_q),
        in_specs=[pl.BlockSpec((Q_TILE, gw), lambda b, g, t: (qrow(b, g, t), g)),
                  pl.BlockSpec((seq, HEAD_DIM), lambda b, g, t: (b, g)),
                  pl.BlockSpec((ctx_len, HEAD_DIM), lambda b, g, t: (ctx_blk0 + b, g)),
                  pl.BlockSpec((HEAD_DIM, seq), lambda b, g, t: (g, b)),
                  pl.BlockSpec((HEAD_DIM, ctx_len), lambda b, g, t: (g, ctx_blk0 + b))],
        out_specs=pl.BlockSpec((Q_TILE, gw), lambda b, g, t: (qrow(b, g, t), g)),
        out_shape=jax.ShapeDtypeStruct((n, d), BF16),
        compiler_params=_cparams(("arbitrary", "arbitrary", "arbitrary")),
    )(q, k, k, vt, vt)


def _window_attn_kernel(seq, q_ref, kl_ref, kc_ref, vtl_ref, vtc_ref, sink_ref, o_ref):
    qt = pl.program_id(2)
    span = Q_TILE + 2 * WINDOW
    start = pl.multiple_of(jnp.clip(qt * Q_TILE - WINDOW, 0, seq - span), math.gcd(Q_TILE, WINDOW))
    q4 = _stack_heads(q_ref)
    cols = KV_REP * Q_TILE
    s_w = _nt_dot(kl_ref[pl.ds(start, span), :], q4)
    s_c = _nt_dot(kc_ref[...], q4)
    kpos = start + lax.broadcasted_iota(jnp.int32, (span, 1), 0)
    qpos = qt * Q_TILE + lax.broadcasted_iota(jnp.int32, (1, cols), 1) % Q_TILE
    s_w = jnp.where(jnp.abs(qpos - kpos) <= WINDOW, s_w, NEG_INF)
    sink = sink_ref[0]
    m = jnp.maximum(jnp.maximum(jnp.max(s_w, axis=0, keepdims=True),
                                jnp.max(s_c, axis=0, keepdims=True)), sink)
    p_w = jnp.exp2(s_w - m)
    p_c = jnp.exp2(s_c - m)
    den = (jnp.sum(p_w, axis=0, keepdims=True) + jnp.sum(p_c, axis=0, keepdims=True)
           + jnp.exp2(sink - m))
    ot = (jnp.dot(vtl_ref[:, pl.ds(start, span)], p_w.astype(BF16), preferred_element_type=F32)
          + jnp.dot(vtc_ref[...], p_c.astype(BF16), preferred_element_type=F32))
    _store_heads(ot * (1.0 / den), o_ref)


def _window_attention(q, k, vt, sink_cols, n_batch, seq, ctx_len):
    n, d = q.shape
    lat_q = seq // Q_TILE
    ctx_blk0 = (n_batch * seq) // ctx_len
    gw = KV_REP * HEAD_DIM
    return pl.pallas_call(
        functools.partial(_window_attn_kernel, seq),
        grid=(n_batch, N_KV_HEADS, lat_q),
        in_specs=[pl.BlockSpec((Q_TILE, gw), lambda b, g, t: (b * lat_q + t, g)),
                  pl.BlockSpec((seq, HEAD_DIM), lambda b, g, t: (b, g)),
                  pl.BlockSpec((ctx_len, HEAD_DIM), lambda b, g, t: (ctx_blk0 + b, g)),
                  pl.BlockSpec((HEAD_DIM, seq), lambda b, g, t: (g, b)),
                  pl.BlockSpec((HEAD_DIM, ctx_len), lambda b, g, t: (g, ctx_blk0 + b)),
                  pl.BlockSpec((1, 1, KV_REP * Q_TILE), lambda b, g, t: (g, 0, 0))],
        out_specs=pl.BlockSpec((Q_TILE, gw), lambda b, g, t: (b * lat_q + t, g)),
        out_shape=jax.ShapeDtypeStruct((n, d), BF16),
        compiler_params=_cparams(("arbitrary", "arbitrary", "arbitrary")),
    )(q, k, k, vt, vt, sink_cols)


def _peer_pre_kernel(o_ref, x_ref, wo_ref, g1_ref, n2_ref, sh_ref, sc_ref, wq_ref, keys_ref,
                     xn_ref, h2t_ref, st_ref):
    y = jnp.dot(o_ref[...], wo_ref[...], preferred_element_type=F32)
    xn = x_ref[...] + g1_ref[0] * y
    xn_ref[...] = xn
    h2 = _rms(xn) * n2_ref[...]
    h2 = h2 * (1.0 + sc_ref[0]) + sh_ref[0]
    h2t_ref[...] = h2.T.astype(BF16)
    qp = jnp.dot(h2.astype(BF16), wq_ref[...], preferred_element_type=F32)
    half = PEER_QDIM // 2
    for hs in range(2 * PEER_HEADS):
        qs = qp[:, hs * half:(hs + 1) * half].astype(BF16)
        st_ref[hs] = _nt_dot(keys_ref[hs % 2], qs)


def _peer_pre(o, xs, mods, layer, w_o, n2, w_q, keys, n_tiles, lat_tiles, tiles_per_seq, n_batch):
    n, d = xs.shape
    t = ROW_TILE
    qd = PEER_HEADS * PEER_QDIM

    def mrow(i):
        return layer * MOD_ROWS + jnp.where(i < lat_tiles, i // tiles_per_seq, n_batch)

    return pl.pallas_call(
        _peer_pre_kernel,
        grid=(n_tiles,),
        in_specs=[pl.BlockSpec((t, d), lambda i: (i, 0)),
                  pl.BlockSpec((t, d), lambda i: (i, 0)),
                  pl.BlockSpec((d, d), lambda i: (0, 0)),
                  pl.BlockSpec((1, 1, d), lambda i: (mrow(i), 0, 2)),
                  pl.BlockSpec((1, d), lambda i: (0, 0)),
                  pl.BlockSpec((1, 1, d), lambda i: (mrow(i), 0, 3)),
                  pl.BlockSpec((1, 1, d), lambda i: (mrow(i), 0, 4)),
                  pl.BlockSpec((d, qd), lambda i: (0, 0)),
                  pl.BlockSpec((2, N_KEYS, PEER_QDIM // 2), lambda i: (0, 0, 0))],
        out_specs=[pl.BlockSpec((t, d), lambda i: (i, 0)),
                   pl.BlockSpec((d, t), lambda i: (0, i)),
                   pl.BlockSpec((2 * PEER_HEADS, N_KEYS, t), lambda i: (0, 0, i))],
        out_shape=[jax.ShapeDtypeStruct((n, d), F32),
                   jax.ShapeDtypeStruct((d, n), BF16),
                   jax.ShapeDtypeStruct((2 * PEER_HEADS, N_KEYS, n), F32)],
        compiler_params=_cparams(("arbitrary",)),
    )(o, xs, w_o, mods, n2, mods, mods, w_q, keys)


def _sorting_network(n):
    def merge(lo, hi, r):
        step = r * 2
        if step < hi - lo:
            yield from merge(lo, hi, step)
            yield from merge(lo + r, hi, step)
            yield from [(i, i + r) for i in range(lo + r, hi - r, step)]
        else:
            yield (lo, lo + r)

    def sort(lo, hi):
        if hi - lo >= 1:
            mid = lo + (hi - lo) // 2
            yield from sort(lo, mid)
            yield from sort(mid + 1, hi)
            yield from merge(lo, hi, 1)

    return list(sort(0, n - 1))


def _exchange(x, i, j):
    x[i], x[j] = jnp.maximum(x[i], x[j]), jnp.minimum(x[i], x[j])


def _top_values(s):
    k = PEER_TOPK
    assert s.shape[0] == k * SUBLANES
    x = [s[i * SUBLANES:(i + 1) * SUBLANES] for i in range(k)]
    for i, j in _sorting_network(k):
        _exchange(x, i, j)
    shift = SUBLANES // 2
    while shift:
        x = [jnp.maximum(x[i], pltpu.roll(x[k - 1 - i], shift, 0)) for i in range(k)]
        d = k // 2
        while d:
            for i in range(k):
                if not i & d:
                    _exchange(x, i, i + d)
            d //= 2
        shift //= 2
    return [xi[0:1] for xi in x]


def _kth_largest(s, k):
    for _ in range(k - 1):
        s = jnp.where(s == jnp.max(s, axis=0, keepdims=True), -jnp.inf, s)
    return jnp.max(s, axis=0, keepdims=True)


def _candidate_sums(v1, v2):
    assert PEER_TOPK == 16
    a1 = jnp.concatenate(v1, axis=0)
    a2 = jnp.concatenate(v2, axis=0)
    row = lax.broadcasted_iota(jnp.int32, (8, a1.shape[1]), 0)
    ninf = -jnp.inf
    return jnp.concatenate([
        v1[0] + a2,
        v1[1] + a2[0:8],
        jnp.where(row < 5, v1[2] + a2[0:8], ninf),
        jnp.where(row < 4, v1[3] + a2[0:8], ninf),
        a1[8:16] + v2[0],
        jnp.where(row >= 4, a1[0:8] + v2[0], ninf),
        jnp.where(row >= 4, a1[0:8] + v2[1], ninf),
        jnp.where(row == 4, a1[0:8] + v2[2], ninf),
    ], axis=0)


def _peer_topk_kernel(st_ref, s2_ref, e2_ref, thr_ref, e1_ref):
    def per_head(h, carry):
        s1 = st_ref[2 * h]
        s2 = st_ref[2 * h + 1]
        v1 = _top_values(s1)
        v2 = _top_values(s2)
        cand = _candidate_sums(v1, v2)
        thr = _kth_largest(cand, PEER_TOPK)
        top = v1[0] + v2[0]
        z = jnp.sum(jnp.where(cand >= thr, jnp.exp(cand - top), 0.0), axis=0, keepdims=True)
        guard = thr - jnp.abs(thr) * THRESHOLD_SLACK
        outs = ((s2_ref, s2), (e2_ref, jnp.exp(s2 - v2[0]) * (1.0 / z)),
                (thr_ref, guard - s1), (e1_ref, jnp.exp(s1 - v1[0])))
        for ref, val in outs:
            for lb in range(val.shape[1] // SUB_LANES):
                ref[lb, h] = val[:, lb * SUB_LANES:(lb + 1) * SUB_LANES]
        return carry

    lax.fori_loop(0, PEER_HEADS, per_head, 0)


def _peer_topk(st, n_tiles):
    _, _, n = st.shape
    t = ROW_TILE
    spec = pl.BlockSpec((t // SUB_LANES, PEER_HEADS, N_KEYS, SUB_LANES), lambda i: (i, 0, 0, 0))
    shape = jax.ShapeDtypeStruct((n // SUB_LANES, PEER_HEADS, N_KEYS, SUB_LANES), F32)
    return pl.pallas_call(
        _peer_topk_kernel,
        grid=(n_tiles,),
        in_specs=[pl.BlockSpec((2 * PEER_HEADS, N_KEYS, t), lambda i: (0, 0, i))],
        out_specs=[spec, spec, spec, spec],
        out_shape=[shape, shape, shape, shape],
        compiler_params=_cparams(("arbitrary",)),
    )(st)


def _gelu_tanh(x):
    k = -2.0 * math.sqrt(2.0 / math.pi)
    return x / (1.0 + jnp.exp(x * (k + (k * 0.044715) * (x * x))))


def _peer_dense_kernel(h2t_ref, s2_ref, e2_ref, thr_ref, e1_ref, u_ref, v_ref, xn_ref, g2_ref,
                       out_ref, acc_ref, act_ref, slab_ref):
    j = pl.program_id(1)
    lanes = acc_ref.shape[1]
    lane_blocks = lanes // SUB_LANES
    row_blocks = N_KEYS // GATE_ROWS

    @pl.when(j == 0)
    def _():
        acc_ref[...] = jnp.zeros_like(acc_ref)

    act_ref[...] = jnp.dot(u_ref[0], h2t_ref[...], preferred_element_type=F32)

    def gate_block(blk, carry):
        grp = blk // (row_blocks * lane_blocks)
        bc = (blk // lane_blocks) % row_blocks
        lc = blk % lane_blocks
        rows = pl.ds(pl.multiple_of(bc * GATE_ROWS, GATE_ROWS), GATE_ROWS)
        cols = pl.ds(pl.multiple_of(lc * SUB_LANES, SUB_LANES), SUB_LANES)
        a0 = j * PEER_CHUNK + grp * GATE_KEYS
        w = [jnp.zeros((GATE_ROWS, SUB_LANES), F32) for _ in range(GATE_KEYS)]
        for h in range(PEER_HEADS):
            s2 = s2_ref[lc, h, rows, :]
            e2 = e2_ref[lc, h, rows, :]
            for al in range(GATE_KEYS):
                thr = thr_ref[lc, h, pl.ds(a0 + al, GATE_ROWS, stride=0), :]
                e1 = e1_ref[lc, h, pl.ds(a0 + al, GATE_ROWS, stride=0), :]
                w[al] = w[al] + jnp.where(s2 >= thr, e2, 0.0) * e1
        for al in range(GATE_KEYS):
            r0 = (grp * GATE_KEYS + al) * N_KEYS + bc * GATE_ROWS
            arows = pl.ds(pl.multiple_of(r0, GATE_ROWS), GATE_ROWS)
            slab_ref[arows, cols] = (w[al] * _gelu_tanh(act_ref[arows, cols])).astype(BF16)
        return carry

    lax.fori_loop(0, (PEER_CHUNK // GATE_KEYS) * row_blocks * lane_blocks, gate_block, 0)
    acc_ref[...] += lax.dot_general(v_ref[0], slab_ref[...], (((0,), (0,)), ((), ())),
                                    preferred_element_type=F32)

    @pl.when(j == pl.num_programs(1) - 1)
    def _():
        out_ref[...] = xn_ref[...] + g2_ref[0] * acc_ref[...].T


def _peer_dense(h2t, coefs, u, v, xn, mods, layer, n_tiles, lat_tiles, tiles_per_seq, n_batch):
    d = xn.shape[1]
    t = PEER_TILE
    ce = PEER_CHUNK * N_KEYS
    n_exp = u.shape[1]

    def mrow(i):
        return layer * MOD_ROWS + jnp.where(i < lat_tiles, i // tiles_per_seq, n_batch)

    cspec = pl.BlockSpec((t // SUB_LANES, PEER_HEADS, N_KEYS, SUB_LANES), lambda i, j: (i, 0, 0, 0))
    return pl.pallas_call(
        _peer_dense_kernel,
        grid=(n_tiles, n_exp // ce),
        in_specs=[pl.BlockSpec((d, t), lambda i, j: (0, i)),
                  cspec, cspec, cspec, cspec,
                  pl.BlockSpec((1, ce, d), lambda i, j: (layer, j, 0)),
                  pl.BlockSpec((1, ce, d), lambda i, j: (layer, j, 0)),
                  pl.BlockSpec((t, d), lambda i, j: (i, 0)),
                  pl.BlockSpec((1, 1, d), lambda i, j: (mrow(i), 0, 5))],
        out_specs=pl.BlockSpec((t, d), lambda i, j: (i, 0)),
        out_shape=jax.ShapeDtypeStruct((n_tiles * t, d), F32),
        scratch_shapes=[pltpu.VMEM((d, t), F32),
                        pltpu.VMEM((ce, t), F32),
                        pltpu.VMEM((ce, t), BF16)],
        compiler_params=_cparams(("arbitrary", "arbitrary")),
    )(h2t, *coefs, u, v, xn, mods)


def _rope_tables(seq, ctx_len):
    pos = jnp.arange(seq)
    row = (pos // GRID_W).astype(F32)
    col = (pos % GRID_W).astype(F32)
    half = HEAD_DIM // 2
    inv = ROPE_THETA ** (-jnp.arange(0, half, 2, dtype=F32) / half)
    ar = row[:, None] * inv
    ac = col[:, None] * inv
    cos = jnp.concatenate([jnp.cos(ar), jnp.cos(ar), jnp.cos(ac), jnp.cos(ac)], axis=-1)
    sin = jnp.concatenate([-jnp.sin(ar), jnp.sin(ar), -jnp.sin(ac), jnp.sin(ac)], axis=-1)
    cos = jnp.concatenate([cos, jnp.ones((ctx_len, HEAD_DIM), F32)], axis=0)
    sin = jnp.concatenate([sin, jnp.zeros((ctx_len, HEAD_DIM), F32)], axis=0)
    return cos, sin


def kernel(x, c, ctx, c_ctx, ada_w, ada_b, norm1_gain, norm2_gain, w_qkv, q_norm_gain, k_norm_gain,
           w_o, attn_sinks, peer_w_q, peer_sub_keys, peer_u, peer_v):
    n_batch, seq, d = x.shape
    ctx_len = ctx.shape[1]
    depth = ada_w.shape[0]
    n_lat = n_batch * seq
    n_ctx = n_batch * ctx_len
    assert depth == 2 and d == D_MODEL and n_batch < MOD_ROWS
    assert seq % PEER_TILE == 0 and n_ctx % PEER_TILE == 0 and ctx_len == ROW_TILE
    assert seq >= Q_TILE + 2 * WINDOW and n_lat % ctx_len == 0

    cc = jnp.concatenate([c, c_ctx[None, :], jnp.zeros((MOD_ROWS - n_batch - 1, d), F32)], axis=0)
    mods = _modulation(cc, ada_w, ada_b).reshape(depth * MOD_ROWS, 1, N_MOD * d)
    cos_t, sin_t = _rope_tables(seq, ctx_len)
    u_tab = peer_u.astype(BF16)
    v_tab = peer_v.astype(BF16)
    xs = jnp.concatenate([x.reshape(n_lat, d), ctx.reshape(n_ctx, d)], axis=0)
    scale = HEAD_DIM ** -0.5 * math.log2(math.e)

    for layer in range(depth):
        last = layer == depth - 1
        q, k, vt = _qkv(xs, mods, layer, norm1_gain[layer][None, :], w_qkv[layer].astype(BF16),
                       (q_norm_gain[layer] * scale)[None, :], k_norm_gain[layer][None, :],
                       cos_t, sin_t, n_lat // ROW_TILE, seq // ROW_TILE, n_batch)
        if layer % 2 == 0:
            o = _global_attention(q, k, vt, n_batch, seq, ctx_len)
        else:
            sink = (attn_sinks[layer // 2] * math.log2(math.e)).reshape(N_KV_HEADS, 1, KV_REP, 1)
            sink_cols = jnp.broadcast_to(sink, (N_KV_HEADS, 1, KV_REP, Q_TILE))
            o = _window_attention(q, k, vt, sink_cols.reshape(N_KV_HEADS, 1, KV_REP * Q_TILE),
                                  n_batch, seq, ctx_len)
        n_rows = n_lat if last else n_lat + n_ctx
        xn, h2t, st = _peer_pre(o, xs, mods, layer, w_o[layer].astype(BF16),
                                norm2_gain[layer][None, :], peer_w_q[layer].astype(BF16),
                                peer_sub_keys[layer].astype(BF16), n_rows // ROW_TILE,
                                n_lat // ROW_TILE, seq // ROW_TILE, n_batch)
        coefs = _peer_topk(st, n_rows // ROW_TILE)
        xs = _peer_dense(h2t, coefs, u_tab, v_tab,
                         xn, mods, layer, n_rows // PEER_TILE, n_lat // PEER_TILE,
                         seq // PEER_TILE, n_batch)
    return xs[:n_lat].reshape(n_batch, seq, d)
```

```python
import functools
import math

import jax
import jax.numpy as jnp
from jax import lax
from jax.experimental import pallas as pl
from jax.experimental.pallas import tpu as pltpu

F32 = jnp.float32
BF16 = jnp.bfloat16

D_MODEL = 1024
N_HEADS = 8
N_KV_HEADS = 2
HEAD_DIM = 128
KV_REP = N_HEADS // N_KV_HEADS
QKV_DIM = (N_HEADS + 2 * N_KV_HEADS) * HEAD_DIM
GRID_W = 64
WINDOW = 128
ROPE_THETA = 10000.0
RMS_EPS = 1e-6
NEG_INF = -1e30
PEER_HEADS = 8
PEER_TOPK = 16
N_KEYS = 128
PEER_QDIM = 256
N_MOD = 6
MOD_ROWS = 8

ROW_TILE = 256
Q_TILE = 256
PEER_TILE = 512
PEER_CHUNK = 16
GATE_KEYS = 8
SUB_LANES = 128
GATE_ROWS = 32
SUBLANES = 8
THRESHOLD_SLACK = 2.0 ** -22
VMEM_LIMIT = 56 * 1024 * 1024


def _cparams(sem):
    return pltpu.CompilerParams(dimension_semantics=sem, vmem_limit_bytes=VMEM_LIMIT)


def _nt_dot(a, b):
    return lax.dot_general(a, b, (((1,), (1,)), ((), ())), preferred_element_type=F32)


def _mod_kernel(c_ref, w_ref, b_ref, o_ref):
    c = c_ref[...]
    a = c * (1.0 / (1.0 + jnp.exp(-c)))
    o_ref[0] = jnp.dot(a, w_ref[0], preferred_element_type=F32,
                       precision=lax.Precision.HIGHEST) + b_ref[0]


def _modulation(cc, ada_w, ada_b):
    depth, d, n = ada_w.shape
    tn = 1536
    return pl.pallas_call(
        _mod_kernel,
        grid=(depth, n // tn),
        in_specs=[pl.BlockSpec((MOD_ROWS, d), lambda l, j: (0, 0)),
                  pl.BlockSpec((1, d, tn), lambda l, j: (l, 0, j)),
                  pl.BlockSpec((1, 1, tn), lambda l, j: (l, 0, j))],
        out_specs=pl.BlockSpec((1, MOD_ROWS, tn), lambda l, j: (l, 0, j)),
        out_shape=jax.ShapeDtypeStruct((depth, MOD_ROWS, n), F32),
        compiler_params=_cparams(("arbitrary", "arbitrary")),
    )(cc, ada_w, ada_b.reshape(depth, 1, n))


def _rms(x):
    return x * lax.rsqrt(jnp.mean(x * x, axis=-1, keepdims=True) + RMS_EPS)


def _qkv_kernel(x_ref, sh_ref, sc_ref, g_ref, w_ref, qg_ref, kg_ref, cos_ref, sin_ref,
                q_ref, k_ref, vt_ref):
    h = _rms(x_ref[...]) * g_ref[...]
    h = h * (1.0 + sc_ref[0]) + sh_ref[0]
    y = jnp.dot(h.astype(BF16), w_ref[...], preferred_element_type=F32)
    cos = cos_ref[...]
    sin = sin_ref[...]
    lane = lax.broadcasted_iota(jnp.int32, cos.shape, 1)
    first = (lane % (HEAD_DIM // 2)) < (HEAD_DIM // 4)

    def head(col, gain):
        n = _rms(y[:, col * HEAD_DIM:(col + 1) * HEAD_DIM]) * gain
        sw = jnp.where(first, pltpu.roll(n, HEAD_DIM - HEAD_DIM // 4, 1),
                       pltpu.roll(n, HEAD_DIM // 4, 1))
        return n * cos + sw * sin

    for hh in range(N_HEADS):
        q_ref[:, hh * HEAD_DIM:(hh + 1) * HEAD_DIM] = head(hh, qg_ref[...]).astype(BF16)
    for g in range(N_KV_HEADS):
        k_ref[:, g * HEAD_DIM:(g + 1) * HEAD_DIM] = head(N_HEADS + g, kg_ref[...]).astype(BF16)
    v0 = (N_HEADS + N_KV_HEADS) * HEAD_DIM
    vt_ref[...] = y[:, v0:].T.astype(BF16)


def _qkv(xs, mods, layer, gain, w, qg, kg, cos_t, sin_t, lat_tiles, tiles_per_seq, n_batch):
    n, d = xs.shape
    t = ROW_TILE

    def mrow(i):
        return layer * MOD_ROWS + jnp.where(i < lat_tiles, i // tiles_per_seq, n_batch)

    def trow(i):
        return jnp.where(i < lat_tiles, i % tiles_per_seq, tiles_per_seq)

    kvd = N_KV_HEADS * HEAD_DIM
    return pl.pallas_call(
        _qkv_kernel,
        grid=(n // t,),
        in_specs=[pl.BlockSpec((t, d), lambda i: (i, 0)),
                  pl.BlockSpec((1, 1, d), lambda i: (mrow(i), 0, 0)),
                  pl.BlockSpec((1, 1, d), lambda i: (mrow(i), 0, 1)),
                  pl.BlockSpec((1, d), lambda i: (0, 0)),
                  pl.BlockSpec((d, QKV_DIM), lambda i: (0, 0)),
                  pl.BlockSpec((1, HEAD_DIM), lambda i: (0, 0)),
                  pl.BlockSpec((1, HEAD_DIM), lambda i: (0, 0)),
                  pl.BlockSpec((t, HEAD_DIM), lambda i: (trow(i), 0)),
                  pl.BlockSpec((t, HEAD_DIM), lambda i: (trow(i), 0))],
        out_specs=[pl.BlockSpec((t, d), lambda i: (i, 0)),
                   pl.BlockSpec((t, kvd), lambda i: (i, 0)),
                   pl.BlockSpec((kvd, t), lambda i: (0, i))],
        out_shape=[jax.ShapeDtypeStruct((n, d), BF16),
                   jax.ShapeDtypeStruct((n, kvd), BF16),
                   jax.ShapeDtypeStruct((kvd, n), BF16)],
        compiler_params=_cparams(("arbitrary",)),
    )(xs, mods, mods, gain, w, qg, kg, cos_t, sin_t)


def _stack_heads(q_ref):
    return jnp.concatenate(
        [q_ref[:, r * HEAD_DIM:(r + 1) * HEAD_DIM] for r in range(KV_REP)], axis=0)


def _store_heads(ot, o_ref):
    for r in range(KV_REP):
        blk = ot[:, r * Q_TILE:(r + 1) * Q_TILE]
        o_ref[:, r * HEAD_DIM:(r + 1) * HEAD_DIM] = blk.T.astype(BF16)


def _global_attn_kernel(lat_q_tiles, q_ref, kl_ref, kc_ref, vtl_ref, vtc_ref, o_ref):
    qt = pl.program_id(2)
    q4 = _stack_heads(q_ref)
    sc = _nt_dot(kc_ref[...], q4)

    def finish(ot, den):
        _store_heads(ot * (1.0 / den), o_ref)

    @pl.when(qt < lat_q_tiles)
    def _():
        sl = _nt_dot(kl_ref[...], q4)
        m = jnp.maximum(jnp.max(sl, axis=0, keepdims=True), jnp.max(sc, axis=0, keepdims=True))
        p_l = jnp.exp2(sl - m)
        p_c = jnp.exp2(sc - m)
        den = jnp.sum(p_l, axis=0, keepdims=True) + jnp.sum(p_c, axis=0, keepdims=True)
        finish(jnp.dot(vtl_ref[...], p_l.astype(BF16), preferred_element_type=F32)
               + jnp.dot(vtc_ref[...], p_c.astype(BF16), preferred_element_type=F32), den)

    @pl.when(qt >= lat_q_tiles)
    def _():
        p_c = jnp.exp2(sc - jnp.max(sc, axis=0, keepdims=True))
        finish(jnp.dot(vtc_ref[...], p_c.astype(BF16), preferred_element_type=F32),
               jnp.sum(p_c, axis=0, keepdims=True))


def _global_attention(q, k, vt, n_batch, seq, ctx_len):
    n, d = q.shape
    lat_q = seq // Q_TILE
    ctx_q = ctx_len // Q_TILE
    ctx_blk0 = (n_batch * seq) // ctx_len
    gw = KV_REP * HEAD_DIM

    def qrow(b, g, t):
        return jnp.where(t < lat_q, b * lat_q + t, n_batch * lat_q + b * ctx_q + (t - lat_q))

    return pl.pallas_call(
        functools.partial(_global_attn_kernel, lat_q),
        grid=(n_batch, N_KV_HEADS, lat_q + ctx_q),
        in_specs=[pl.BlockSpec((Q_TILE, gw), lambda b, g, t: (qrow(b, g, t), g)),
                  pl.BlockSpec((seq, HEAD_DIM), lambda b, g, t: (b, g)),
                  pl.BlockSpec((ctx_len, HEAD_DIM), lambda b, g, t: (ctx_blk0 + b, g)),
                  pl.BlockSpec((HEAD_DIM, seq), lambda b, g, t: (g, b)),
                  pl.BlockSpec((HEAD_DIM, ctx_len), lambda b, g, t: (g, ctx_blk0 + b))],
        out_specs=pl.BlockSpec((Q_TILE, gw), lambda b, g, t: (qrow(b, g, t), g)),
        out_shape=jax.ShapeDtypeStruct((n, d), BF16),
        compiler_params=_cparams(("arbitrary", "arbitrary", "arbitrary")),
    )(q, k, k, vt, vt)


def _window_attn_kernel(seq, q_ref, kl_ref, kc_ref, vtl_ref, vtc_ref, sink_ref, o_ref):
    qt = pl.program_id(2)
    span = Q_TILE + 2 * WINDOW
    start = pl.multiple_of(jnp.clip(qt * Q_TILE - WINDOW, 0, seq - span), math.gcd(Q_TILE, WINDOW))
    q4 = _stack_heads(q_ref)
    cols = KV_REP * Q_TILE
    s_w = _nt_dot(kl_ref[pl.ds(start, span), :], q4)
    s_c = _nt_dot(kc_ref[...], q4)
    kpos = start + lax.broadcasted_iota(jnp.int32, (span, 1), 0)
    qpos = qt * Q_TILE + lax.broadcasted_iota(jnp.int32, (1, cols), 1) % Q_TILE
    s_w = jnp.where(jnp.abs(qpos - kpos) <= WINDOW, s_w, NEG_INF)
    sink = sink_ref[0]
    m = jnp.maximum(jnp.maximum(jnp.max(s_w, axis=0, keepdims=True),
                                jnp.max(s_c, axis=0, keepdims=True)), sink)
    p_w = jnp.exp2(s_w - m)
    p_c = jnp.exp2(s_c - m)
    den = (jnp.sum(p_w, axis=0, keepdims=True) + jnp.sum(p_c, axis=0, keepdims=True)
           + jnp.exp2(sink - m))
    ot = (jnp.dot(vtl_ref[:, pl.ds(start, span)], p_w.astype(BF16), preferred_element_type=F32)
          + jnp.dot(vtc_ref[...], p_c.astype(BF16), preferred_element_type=F32))
    _store_heads(ot * (1.0 / den), o_ref)


def _window_attention(q, k, vt, sink_cols, n_batch, seq, ctx_len):
    n, d = q.shape
    lat_q = seq // Q_TILE
    ctx_blk0 = (n_batch * seq) // ctx_len
    gw = KV_REP * HEAD_DIM
    return pl.pallas_call(
        functools.partial(_window_attn_kernel, seq),
        grid=(n_batch, N_KV_HEADS, lat_q),
        in_specs=[pl.BlockSpec((Q_TILE, gw), lambda b, g, t: (b * lat_q + t, g)),
                  pl.BlockSpec((seq, HEAD_DIM), lambda b, g, t: (b, g)),
                  pl.BlockSpec((ctx_len, HEAD_DIM), lambda b, g, t: (ctx_blk0 + b, g)),
                  pl.BlockSpec((HEAD_DIM, seq), lambda b, g, t: (g, b)),
                  pl.BlockSpec((HEAD_DIM, ctx_len), lambda b, g, t: (g, ctx_blk0 + b)),
                  pl.BlockSpec((1, 1, KV_REP * Q_TILE), lambda b, g, t: (g, 0, 0))],
        out_specs=pl.BlockSpec((Q_TILE, gw), lambda b, g, t: (b * lat_q + t, g)),
        out_shape=jax.ShapeDtypeStruct((n, d), BF16),
        compiler_params=_cparams(("arbitrary", "arbitrary", "arbitrary")),
    )(q, k, k, vt, vt, sink_cols)


def _peer_pre_kernel(o_ref, x_ref, wo_ref, g1_ref, n2_ref, sh_ref, sc_ref, wq_ref, keys_ref,
                     xn_ref, h2t_ref, st_ref):
    y = jnp.dot(o_ref[...], wo_ref[...], preferred_element_type=F32)
    xn = x_ref[...] + g1_ref[0] * y
    xn_ref[...] = xn
    h2 = _rms(xn) * n2_ref[...]
    h2 = h2 * (1.0 + sc_ref[0]) + sh_ref[0]
    h2t_ref[...] = h2.T.astype(BF16)
    qp = jnp.dot(h2.astype(BF16), wq_ref[...], preferred_element_type=F32)
    half = PEER_QDIM // 2
    for hs in range(2 * PEER_HEADS):
        qs = qp[:, hs * half:(hs + 1) * half].astype(BF16)
        st_ref[hs] = _nt_dot(keys_ref[hs % 2], qs)


def _peer_pre(o, xs, mods, layer, w_o, n2, w_q, keys, n_tiles, lat_tiles, tiles_per_seq, n_batch):
    n, d = xs.shape
    t = ROW_TILE
    qd = PEER_HEADS * PEER_QDIM

    def mrow(i):
        return layer * MOD_ROWS + jnp.where(i < lat_tiles, i // tiles_per_seq, n_batch)

    return pl.pallas_call(
        _peer_pre_kernel,
        grid=(n_tiles,),
        in_specs=[pl.BlockSpec((t, d), lambda i: (i, 0)),
                  pl.BlockSpec((t, d), lambda i: (i, 0)),
                  pl.BlockSpec((d, d), lambda i: (0, 0)),
                  pl.BlockSpec((1, 1, d), lambda i: (mrow(i), 0, 2)),
                  pl.BlockSpec((1, d), lambda i: (0, 0)),
                  pl.BlockSpec((1, 1, d), lambda i: (mrow(i), 0, 3)),
                  pl.BlockSpec((1, 1, d), lambda i: (mrow(i), 0, 4)),
                  pl.BlockSpec((d, qd), lambda i: (0, 0)),
                  pl.BlockSpec((2, N_KEYS, PEER_QDIM // 2), lambda i: (0, 0, 0))],
        out_specs=[pl.BlockSpec((t, d), lambda i: (i, 0)),
                   pl.BlockSpec((d, t), lambda i: (0, i)),
                   pl.BlockSpec((2 * PEER_HEADS, N_KEYS, t), lambda i: (0, 0, i))],
        out_shape=[jax.ShapeDtypeStruct((n, d), F32),
                   jax.ShapeDtypeStruct((d, n), BF16),
                   jax.ShapeDtypeStruct((2 * PEER_HEADS, N_KEYS, n), F32)],
        compiler_params=_cparams(("arbitrary",)),
    )(o, xs, w_o, mods, n2, mods, mods, w_q, keys)


def _sorting_network(n):
    def merge(lo, hi, r):
        step = r * 2
        if step < hi - lo:
            yield from merge(lo, hi, step)
            yield from merge(lo + r, hi, step)
            yield from [(i, i + r) for i in range(lo + r, hi - r, step)]
        else:
            yield (lo, lo + r)

    def sort(lo, hi):
        if hi - lo >= 1:
            mid = lo + (hi - lo) // 2
            yield from sort(lo, mid)
            yield from sort(mid + 1, hi)
            yield from merge(lo, hi, 1)

    return list(sort(0, n - 1))


def _exchange(x, i, j):
    x[i], x[j] = jnp.maximum(x[i], x[j]), jnp.minimum(x[i], x[j])


def _top_values(s):
    k = PEER_TOPK
    assert s.shape[0] == k * SUBLANES
    x = [s[i * SUBLANES:(i + 1) * SUBLANES] for i in range(k)]
    for i, j in _sorting_network(k):
        _exchange(x, i, j)
    shift = SUBLANES // 2
    while shift:
        x = [jnp.maximum(x[i], pltpu.roll(x[k - 1 - i], shift, 0)) for i in range(k)]
        d = k // 2
        while d:
            for i in range(k):
                if not i & d:
                    _exchange(x, i, i + d)
            d //= 2
        shift //= 2
    return [xi[0:1] for xi in x]


def _kth_largest(s, k):
    for _ in range(k - 1):
        s = jnp.where(s == jnp.max(s, axis=0, keepdims=True), -jnp.inf, s)
    return jnp.max(s, axis=0, keepdims=True)


def _candidate_sums(v1, v2):
    assert PEER_TOPK == 16
    a1 = jnp.concatenate(v1, axis=0)
    a2 = jnp.concatenate(v2, axis=0)
    row = lax.broadcasted_iota(jnp.int32, (8, a1.shape[1]), 0)
    ninf = -jnp.inf
    return jnp.concatenate([
        v1[0] + a2,
        v1[1] + a2[0:8],
        jnp.where(row < 5, v1[2] + a2[0:8], ninf),
        jnp.where(row < 4, v1[3] + a2[0:8], ninf),
        a1[8:16] + v2[0],
        jnp.where(row >= 4, a1[0:8] + v2[0], ninf),
        jnp.where(row >= 4, a1[0:8] + v2[1], ninf),
        jnp.where(row == 4, a1[0:8] + v2[2], ninf),
    ], axis=0)


def _peer_topk_kernel(st_ref, s2_ref, e2_ref, thr_ref, e1_ref):
    def per_head(h, carry):
        s1 = st_ref[2 * h]
        s2 = st_ref[2 * h + 1]
        v1 = _top_values(s1)
        v2 = _top_values(s2)
        cand = _candidate_sums(v1, v2)
        thr = _kth_largest(cand, PEER_TOPK)
        top = v1[0] + v2[0]
        z = jnp.sum(jnp.where(cand >= thr, jnp.exp(cand - top), 0.0), axis=0, keepdims=True)
        guard = thr - jnp.abs(thr) * THRESHOLD_SLACK
        outs = ((s2_ref, s2), (e2_ref, jnp.exp(s2 - v2[0]) * (1.0 / z)),
                (thr_ref, guard - s1), (e1_ref, jnp.exp(s1 - v1[0])))
        for ref, val in outs:
            for lb in range(val.shape[1] // SUB_LANES):
                ref[lb, h] = val[:, lb * SUB_LANES:(lb + 1) * SUB_LANES]
        return carry

    lax.fori_loop(0, PEER_HEADS, per_head, 0)


def _peer_topk(st, n_tiles):
    _, _, n = st.shape
    t = ROW_TILE
    spec = pl.BlockSpec((t // SUB_LANES, PEER_HEADS, N_KEYS, SUB_LANES), lambda i: (i, 0, 0, 0))
    shape = jax.ShapeDtypeStruct((n // SUB_LANES, PEER_HEADS, N_KEYS, SUB_LANES), F32)
    return pl.pallas_call(
        _peer_topk_kernel,
        grid=(n_tiles,),
        in_specs=[pl.BlockSpec((2 * PEER_HEADS, N_KEYS, t), lambda i: (0, 0, i))],
        out_specs=[spec, spec, spec, spec],
        out_shape=[shape, shape, shape, shape],
        compiler_params=_cparams(("arbitrary",)),
    )(st)


def _gelu_tanh(x):
    k = -2.0 * math.log2(math.e) * math.sqrt(2.0 / math.pi)
    return x / (1.0 + jnp.exp2(x * (k + (k * 0.044715) * (x * x))))


def _peer_dense_kernel(h2t_ref, s2_ref, e2_ref, thr_ref, e1_ref, u_ref, v_ref, xn_ref, g2_ref,
                       out_ref, acc_ref, act_ref, slab_ref):
    j = pl.program_id(1)
    lanes = acc_ref.shape[1]
    lane_blocks = lanes // SUB_LANES
    row_blocks = N_KEYS // GATE_ROWS

    @pl.when(j == 0)
    def _():
        acc_ref[...] = jnp.zeros_like(acc_ref)

    act_ref[...] = jnp.dot(u_ref[0], h2t_ref[...], preferred_element_type=F32)

    def gate_block(blk, carry):
        grp = blk // (row_blocks * lane_blocks)
        bc = (blk // lane_blocks) % row_blocks
        lc = blk % lane_blocks
        rows = pl.ds(pl.multiple_of(bc * GATE_ROWS, GATE_ROWS), GATE_ROWS)
        cols = pl.ds(pl.multiple_of(lc * SUB_LANES, SUB_LANES), SUB_LANES)
        a0 = j * PEER_CHUNK + grp * GATE_KEYS
        w = [None] * GATE_KEYS
        for h in range(PEER_HEADS):
            s2 = s2_ref[lc, h, rows, :]
            e2 = e2_ref[lc, h, rows, :]
            for al in range(GATE_KEYS):
                thr = thr_ref[lc, h, pl.ds(a0 + al, GATE_ROWS, stride=0), :]
                e1 = e1_ref[lc, h, pl.ds(a0 + al, GATE_ROWS, stride=0), :]
                term = jnp.where(s2 >= thr, e2, 0.0) * e1
                w[al] = term if h == 0 else w[al] + term
        for al in range(GATE_KEYS):
            r0 = (grp * GATE_KEYS + al) * N_KEYS + bc * GATE_ROWS
            arows = pl.ds(pl.multiple_of(r0, GATE_ROWS), GATE_ROWS)
            slab_ref[arows, cols] = (w[al] * _gelu_tanh(act_ref[arows, cols])).astype(BF16)
        return carry

    lax.fori_loop(0, (PEER_CHUNK // GATE_KEYS) * row_blocks * lane_blocks, gate_block, 0)
    acc_ref[...] += lax.dot_general(v_ref[0], slab_ref[...], (((0,), (0,)), ((), ())),
                                    preferred_element_type=F32)

    @pl.when(j == pl.num_programs(1) - 1)
    def _():
        out_ref[...] = xn_ref[...] + g2_ref[0] * acc_ref[...].T


def _peer_dense(h2t, coefs, u, v, xn, mods, layer, n_tiles, lat_tiles, tiles_per_seq, n_batch):
    d = xn.shape[1]
    t = PEER_TILE
    ce = PEER_CHUNK * N_KEYS
    n_exp = u.shape[1]

    def mrow(i):
        return layer * MOD_ROWS + jnp.where(i < lat_tiles, i // tiles_per_seq, n_batch)

    cspec = pl.BlockSpec((t // SUB_LANES, PEER_HEADS, N_KEYS, SUB_LANES), lambda i, j: (i, 0, 0, 0))
    return pl.pallas_call(
        _peer_dense_kernel,
        grid=(n_tiles, n_exp // ce),
        in_specs=[pl.BlockSpec((d, t), lambda i, j: (0, i)),
                  cspec, cspec, cspec, cspec,
                  pl.BlockSpec((1, ce, d), lambda i, j: (layer, j, 0)),
                  pl.BlockSpec((1, ce, d), lambda i, j: (layer, j, 0)),
                  pl.BlockSpec((t, d), lambda i, j: (i, 0)),
                  pl.BlockSpec((1, 1, d), lambda i, j: (mrow(i), 0, 5))],
        out_specs=pl.BlockSpec((t, d), lambda i, j: (i, 0)),
        out_shape=jax.ShapeDtypeStruct((n_tiles * t, d), F32),
        scratch_shapes=[pltpu.VMEM((d, t), F32),
                        pltpu.VMEM((ce, t), F32),
                        pltpu.VMEM((ce, t), BF16)],
        compiler_params=_cparams(("arbitrary", "arbitrary")),
    )(h2t, *coefs, u, v, xn, mods)


def _rope_tables(seq, ctx_len):
    pos = jnp.arange(seq)
    row = (pos // GRID_W).astype(F32)
    col = (pos % GRID_W).astype(F32)
    half = HEAD_DIM // 2
    inv = ROPE_THETA ** (-jnp.arange(0, half, 2, dtype=F32) / half)
    ar = row[:, None] * inv
    ac = col[:, None] * inv
    cos = jnp.concatenate([jnp.cos(ar), jnp.cos(ar), jnp.cos(ac), jnp.cos(ac)], axis=-1)
    sin = jnp.concatenate([-jnp.sin(ar), jnp.sin(ar), -jnp.sin(ac), jnp.sin(ac)], axis=-1)
    cos = jnp.concatenate([cos, jnp.ones((ctx_len, HEAD_DIM), F32)], axis=0)
    sin = jnp.concatenate([sin, jnp.zeros((ctx_len, HEAD_DIM), F32)], axis=0)
    return cos, sin


def kernel(x, c, ctx, c_ctx, ada_w, ada_b, norm1_gain, norm2_gain, w_qkv, q_norm_gain, k_norm_gain,
           w_o, attn_sinks, peer_w_q, peer_sub_keys, peer_u, peer_v):
    n_batch, seq, d = x.shape
    ctx_len = ctx.shape[1]
    depth = ada_w.shape[0]
    n_lat = n_batch * seq
    n_ctx = n_batch * ctx_len
    assert depth == 2 and d == D_MODEL and n_batch < MOD_ROWS
    assert seq % PEER_TILE == 0 and n_ctx % PEER_TILE == 0 and ctx_len == ROW_TILE
    assert seq >= Q_TILE + 2 * WINDOW and n_lat % ctx_len == 0

    cc = jnp.concatenate([c, c_ctx[None, :], jnp.zeros((MOD_ROWS - n_batch - 1, d), F32)], axis=0)
    mods = _modulation(cc, ada_w, ada_b).reshape(depth * MOD_ROWS, 1, N_MOD * d)
    cos_t, sin_t = _rope_tables(seq, ctx_len)
    u_tab = peer_u.astype(BF16)
    v_tab = peer_v.astype(BF16)
    xs = jnp.concatenate([x.reshape(n_lat, d), ctx.reshape(n_ctx, d)], axis=0)
    scale = HEAD_DIM ** -0.5 * math.log2(math.e)

    for layer in range(depth):
        last = layer == depth - 1
        q, k, vt = _qkv(xs, mods, layer, norm1_gain[layer][None, :], w_qkv[layer].astype(BF16),
                       (q_norm_gain[layer] * scale)[None, :], k_norm_gain[layer][None, :],
                       cos_t, sin_t, n_lat // ROW_TILE, seq // ROW_TILE, n_batch)
        if layer % 2 == 0:
            o = _global_attention(q, k, vt, n_batch, seq, ctx_len)
        else:
            sink = (attn_sinks[layer // 2] * math.log2(math.e)).reshape(N_KV_HEADS, 1, KV_REP, 1)
            sink_cols = jnp.broadcast_to(sink, (N_KV_HEADS, 1, KV_REP, Q_TILE))
            o = _window_attention(q, k, vt, sink_cols.reshape(N_KV_HEADS, 1, KV_REP * Q_TILE),
                                  n_batch, seq, ctx_len)
        n_rows = n_lat if last else n_lat + n_ctx
        xn, h2t, st = _peer_pre(o, xs, mods, layer, w_o[layer].astype(BF16),
                                norm2_gain[layer][None, :], peer_w_q[layer].astype(BF16),
                                peer_sub_keys[layer].astype(BF16), n_rows // ROW_TILE,
                                n_lat // ROW_TILE, seq // ROW_TILE, n_batch)
        coefs = _peer_topk(st, n_rows // ROW_TILE)
        xs = _peer_dense(h2t, coefs, u_tab, v_tab,
                         xn, mods, layer, n_rows // PEER_TILE, n_lat // PEER_TILE,
                         seq // PEER_TILE, n_batch)
    return xs[:n_lat].reshape(n_batch, seq, d)
```

```python
import functools
import math

import jax
import jax.numpy as jnp
from jax import lax
from jax.experimental import pallas as pl
from jax.experimental.pallas import tpu as pltpu

F32 = jnp.float32
BF16 = jnp.bfloat16

D_MODEL = 1024
N_HEADS = 8
N_KV_HEADS = 2
HEAD_DIM = 128
KV_REP = N_HEADS // N_KV_HEADS
QKV_DIM = (N_HEADS + 2 * N_KV_HEADS) * HEAD_DIM
GRID_W = 64
WINDOW = 128
ROPE_THETA = 10000.0
RMS_EPS = 1e-6
NEG_INF = -1e30
PEER_HEADS = 8
PEER_TOPK = 16
N_KEYS = 128
PEER_QDIM = 256
N_MOD = 6
MOD_ROWS = 8

ROW_TILE = 256
Q_TILE = 256
PEER_TILE = 512
PEER_CHUNK = 16
GATE_KEYS = 8
SUB_LANES = 128
GATE_ROWS = 32
SUBLANES = 8
THRESHOLD_SLACK = 2.0 ** -22
VMEM_LIMIT = 56 * 1024 * 1024


def _cparams(sem):
    return pltpu.CompilerParams(dimension_semantics=sem, vmem_limit_bytes=VMEM_LIMIT)


def _nt_dot(a, b):
    return lax.dot_general(a, b, (((1,), (1,)), ((), ())), preferred_element_type=F32)


def _mod_kernel(c_ref, w_ref, b_ref, o_ref):
    c = c_ref[...]
    a = c * (1.0 / (1.0 + jnp.exp(-c)))
    o_ref[0] = jnp.dot(a, w_ref[0], preferred_element_type=F32,
                       precision=lax.Precision.HIGHEST) + b_ref[0]


def _modulation(cc, ada_w, ada_b):
    depth, d, n = ada_w.shape
    tn = 1536
    return pl.pallas_call(
        _mod_kernel,
        grid=(depth, n // tn),
        in_specs=[pl.BlockSpec((MOD_ROWS, d), lambda l, j: (0, 0)),
                  pl.BlockSpec((1, d, tn), lambda l, j: (l, 0, j)),
                  pl.BlockSpec((1, 1, tn), lambda l, j: (l, 0, j))],
        out_specs=pl.BlockSpec((1, MOD_ROWS, tn), lambda l, j: (l, 0, j)),
        out_shape=jax.ShapeDtypeStruct((depth, MOD_ROWS, n), F32),
        compiler_params=_cparams(("arbitrary", "arbitrary")),
    )(cc, ada_w, ada_b.reshape(depth, 1, n))


def _rms(x):
    return x * lax.rsqrt(jnp.mean(x * x, axis=-1, keepdims=True) + RMS_EPS)


def _qkv_kernel(x_ref, sh_ref, sc_ref, g_ref, w_ref, qg_ref, kg_ref, cos_ref, sin_ref,
                q_ref, k_ref, vt_ref):
    h = _rms(x_ref[...]) * g_ref[...]
    h = h * (1.0 + sc_ref[0]) + sh_ref[0]
    y = jnp.dot(h.astype(BF16), w_ref[...], preferred_element_type=F32)
    cos = cos_ref[...]
    sin = sin_ref[...]
    lane = lax.broadcasted_iota(jnp.int32, cos.shape, 1)
    first = (lane % (HEAD_DIM // 2)) < (HEAD_DIM // 4)

    def head(col, gain):
        n = _rms(y[:, col * HEAD_DIM:(col + 1) * HEAD_DIM]) * gain
        sw = jnp.where(first, pltpu.roll(n, HEAD_DIM - HEAD_DIM // 4, 1),
                       pltpu.roll(n, HEAD_DIM // 4, 1))
        return n * cos + sw * sin

    for hh in range(N_HEADS):
        q_ref[:, hh * HEAD_DIM:(hh + 1) * HEAD_DIM] = head(hh, qg_ref[...]).astype(BF16)
    for g in range(N_KV_HEADS):
        k_ref[:, g * HEAD_DIM:(g + 1) * HEAD_DIM] = head(N_HEADS + g, kg_ref[...]).astype(BF16)
    v0 = (N_HEADS + N_KV_HEADS) * HEAD_DIM
    vt_ref[...] = y[:, v0:].T.astype(BF16)


def _qkv(xs, mods, layer, gain, w, qg, kg, cos_t, sin_t, lat_tiles, tiles_per_seq, n_batch):
    n, d = xs.shape
    t = ROW_TILE

    def mrow(i):
        return layer * MOD_ROWS + jnp.where(i < lat_tiles, i // tiles_per_seq, n_batch)

    def trow(i):
        return jnp.where(i < lat_tiles, i % tiles_per_seq, tiles_per_seq)

    kvd = N_KV_HEADS * HEAD_DIM
    return pl.pallas_call(
        _qkv_kernel,
        grid=(n // t,),
        in_specs=[pl.BlockSpec((t, d), lambda i: (i, 0)),
                  pl.BlockSpec((1, 1, d), lambda i: (mrow(i), 0, 0)),
                  pl.BlockSpec((1, 1, d), lambda i: (mrow(i), 0, 1)),
                  pl.BlockSpec((1, d), lambda i: (0, 0)),
                  pl.BlockSpec((d, QKV_DIM), lambda i: (0, 0)),
                  pl.BlockSpec((1, HEAD_DIM), lambda i: (0, 0)),
                  pl.BlockSpec((1, HEAD_DIM), lambda i: (0, 0)),
                  pl.BlockSpec((t, HEAD_DIM), lambda i: (trow(i), 0)),
                  pl.BlockSpec((t, HEAD_DIM), lambda i: (trow(i), 0))],
        out_specs=[pl.BlockSpec((t, d), lambda i: (i, 0)),
                   pl.BlockSpec((t, kvd), lambda i: (i, 0)),
                   pl.BlockSpec((kvd, t), lambda i: (0, i))],
        out_shape=[jax.ShapeDtypeStruct((n, d), BF16),
                   jax.ShapeDtypeStruct((n, kvd), BF16),
                   jax.ShapeDtypeStruct((kvd, n), BF16)],
        compiler_params=_cparams(("arbitrary",)),
    )(xs, mods, mods, gain, w, qg, kg, cos_t, sin_t)


def _stack_heads(q_ref):
    return jnp.concatenate(
        [q_ref[:, r * HEAD_DIM:(r + 1) * HEAD_DIM] for r in range(KV_REP)], axis=0)


def _store_heads(ot, o_ref):
    for r in range(KV_REP):
        blk = ot[:, r * Q_TILE:(r + 1) * Q_TILE]
        o_ref[:, r * HEAD_DIM:(r + 1) * HEAD_DIM] = blk.T.astype(BF16)


def _global_attn_kernel(lat_q_tiles, q_ref, kl_ref, kc_ref, vtl_ref, vtc_ref, o_ref):
    qt = pl.program_id(2)
    q4 = _stack_heads(q_ref)
    sc = _nt_dot(kc_ref[...], q4)

    def finish(ot, den):
        _store_heads(ot * (1.0 / den), o_ref)

    @pl.when(qt < lat_q_tiles)
    def _():
        sl = _nt_dot(kl_ref[...], q4)
        m = jnp.maximum(jnp.max(sl, axis=0, keepdims=True), jnp.max(sc, axis=0, keepdims=True))
        p_l = jnp.exp2(sl - m)
        p_c = jnp.exp2(sc - m)
        den = jnp.sum(p_l, axis=0, keepdims=True) + jnp.sum(p_c, axis=0, keepdims=True)
        finish(jnp.dot(vtl_ref[...], p_l.astype(BF16), preferred_element_type=F32)
               + jnp.dot(vtc_ref[...], p_c.astype(BF16), preferred_element_type=F32), den)

    @pl.when(qt >= lat_q_tiles)
    def _():
        p_c = jnp.exp2(sc - jnp.max(sc, axis=0, keepdims=True))
        finish(jnp.dot(vtc_ref[...], p_c.astype(BF16), preferred_element_type=F32),
               jnp.sum(p_c, axis=0, keepdims=True))


def _global_attention(q, k, vt, n_batch, seq, ctx_len):
    n, d = q.shape
    lat_q = seq // Q_TILE
    ctx_q = ctx_len // Q_TILE
    ctx_blk0 = (n_batch * seq) // ctx_len
    gw = KV_REP * HEAD_DIM

    def qrow(b, g, t):
        return jnp.where(t < lat_q, b * lat_q + t, n_batch * lat_q + b * ctx_q + (t - lat_q))

    return pl.pallas_call(
        functools.partial(_global_attn_kernel, lat_q),
        grid=(n_batch, N_KV_HEADS, lat_q + ctx_q),
        in_specs=[pl.BlockSpec((Q_TILE, gw), lambda b, g, t: (qrow(b, g, t), g)),
                  pl.BlockSpec((seq, HEAD_DIM), lambda b, g, t: (b, g)),
                  pl.BlockSpec((ctx_len, HEAD_DIM), lambda b, g, t: (ctx_blk0 + b, g)),
                  pl.BlockSpec((HEAD_DIM, seq), lambda b, g, t: (g, b)),
                  pl.BlockSpec((HEAD_DIM, ctx_len), lambda b, g, t: (g, ctx_blk0 + b))],
        out_specs=pl.BlockSpec((Q_TILE, gw), lambda b, g, t: (qrow(b, g, t), g)),
        out_shape=jax.ShapeDtypeStruct((n, d), BF16),
        compiler_params=_cparams(("arbitrary", "arbitrary", "arbitrary")),
    )(q, k, k, vt, vt)


def _window_attn_kernel(seq, q_ref, kl_ref, kc_ref, vtl_ref, vtc_ref, sink_ref, o_ref):
    qt = pl.program_id(2)
    span = Q_TILE + 2 * WINDOW
    start = pl.multiple_of(jnp.clip(qt * Q_TILE - WINDOW, 0, seq - span), math.gcd(Q_TILE, WINDOW))
    q4 = _stack_heads(q_ref)
    cols = KV_REP * Q_TILE
    s_w = _nt_dot(kl_ref[pl.ds(start, span), :], q4)
    s_c = _nt_dot(kc_ref[...], q4)
    kpos = start + lax.broadcasted_iota(jnp.int32, (span, 1), 0)
    qpos = qt * Q_TILE + lax.broadcasted_iota(jnp.int32, (1, cols), 1) % Q_TILE
    s_w = jnp.where(jnp.abs(qpos - kpos) <= WINDOW, s_w, NEG_INF)
    sink = sink_ref[0]
    m = jnp.maximum(jnp.maximum(jnp.max(s_w, axis=0, keepdims=True),
                                jnp.max(s_c, axis=0, keepdims=True)), sink)
    p_w = jnp.exp2(s_w - m)
    p_c = jnp.exp2(s_c - m)
    den = (jnp.sum(p_w, axis=0, keepdims=True) + jnp.sum(p_c, axis=0, keepdims=True)
           + jnp.exp2(sink - m))
    ot = (jnp.dot(vtl_ref[:, pl.ds(start, span)], p_w.astype(BF16), preferred_element_type=F32)
          + jnp.dot(vtc_ref[...], p_c.astype(BF16), preferred_element_type=F32))
    _store_heads(ot * (1.0 / den), o_ref)


def _window_attention(q, k, vt, sink_cols, n_batch, seq, ctx_len):
    n, d = q.shape
    lat_q = seq // Q_TILE
    ctx_blk0 = (n_batch * seq) // ctx_len
    gw = KV_REP * HEAD_DIM
    return pl.pallas_call(
        functools.partial(_window_attn_kernel, seq),
        grid=(n_batch, N_KV_HEADS, lat_q),
        in_specs=[pl.BlockSpec((Q_TILE, gw), lambda b, g, t: (b * lat_q + t, g)),
                  pl.BlockSpec((seq, HEAD_DIM), lambda b, g, t: (b, g)),
                  pl.BlockSpec((ctx_len, HEAD_DIM), lambda b, g, t: (ctx_blk0 + b, g)),
                  pl.BlockSpec((HEAD_DIM, seq), lambda b, g, t: (g, b)),
                  pl.BlockSpec((HEAD_DIM, ctx_len), lambda b, g, t: (g, ctx_blk0 + b)),
                  pl.BlockSpec((1, 1, KV_REP * Q_TILE), lambda b, g, t: (g, 0, 0))],
        out_specs=pl.BlockSpec((Q_TILE, gw), lambda b, g, t: (b * lat_q + t, g)),
        out_shape=jax.ShapeDtypeStruct((n, d), BF16),
        compiler_params=_cparams(("arbitrary", "arbitrary", "arbitrary")),
    )(q, k, k, vt, vt, sink_cols)


def _peer_pre_kernel(o_ref, x_ref, wo_ref, g1_ref, n2_ref, sh_ref, sc_ref, wq_ref, keys_ref,
                     xn_ref, h2t_ref, s2_ref, e2_ref, thr_ref, e1_ref, st_ref):
    y = jnp.dot(o_ref[...], wo_ref[...], preferred_element_type=F32)
    xn = x_ref[...] + g1_ref[0] * y
    xn_ref[...] = xn
    h2 = _rms(xn) * n2_ref[...]
    h2 = h2 * (1.0 + sc_ref[0]) + sh_ref[0]
    h2t_ref[...] = h2.T.astype(BF16)
    qp = jnp.dot(h2.astype(BF16), wq_ref[...], preferred_element_type=F32)
    half = PEER_QDIM // 2
    for hs in range(2 * PEER_HEADS):
        qs = qp[:, hs * half:(hs + 1) * half].astype(BF16)
        st_ref[hs] = _nt_dot(keys_ref[hs % 2], qs)
    _peer_topk(st_ref, s2_ref, e2_ref, thr_ref, e1_ref)


def _peer_pre(o, xs, mods, layer, w_o, n2, w_q, keys, n_tiles, lat_tiles, tiles_per_seq, n_batch):
    n, d = xs.shape
    t = ROW_TILE
    qd = PEER_HEADS * PEER_QDIM

    def mrow(i):
        return layer * MOD_ROWS + jnp.where(i < lat_tiles, i // tiles_per_seq, n_batch)

    cspec = pl.BlockSpec((t // SUB_LANES, PEER_HEADS, N_KEYS, SUB_LANES), lambda i: (i, 0, 0, 0))
    cshape = jax.ShapeDtypeStruct((n // SUB_LANES, PEER_HEADS, N_KEYS, SUB_LANES), F32)
    outs = pl.pallas_call(
        _peer_pre_kernel,
        grid=(n_tiles,),
        in_specs=[pl.BlockSpec((t, d), lambda i: (i, 0)),
                  pl.BlockSpec((t, d), lambda i: (i, 0)),
                  pl.BlockSpec((d, d), lambda i: (0, 0)),
                  pl.BlockSpec((1, 1, d), lambda i: (mrow(i), 0, 2)),
                  pl.BlockSpec((1, d), lambda i: (0, 0)),
                  pl.BlockSpec((1, 1, d), lambda i: (mrow(i), 0, 3)),
                  pl.BlockSpec((1, 1, d), lambda i: (mrow(i), 0, 4)),
                  pl.BlockSpec((d, qd), lambda i: (0, 0)),
                  pl.BlockSpec((2, N_KEYS, PEER_QDIM // 2), lambda i: (0, 0, 0))],
        out_specs=[pl.BlockSpec((t, d), lambda i: (i, 0)),
                   pl.BlockSpec((d, t), lambda i: (0, i)),
                   cspec, cspec, cspec, cspec],
        out_shape=[jax.ShapeDtypeStruct((n, d), F32),
                   jax.ShapeDtypeStruct((d, n), BF16),
                   cshape, cshape, cshape, cshape],
        scratch_shapes=[pltpu.VMEM((2 * PEER_HEADS, N_KEYS, t), F32)],
        compiler_params=_cparams(("arbitrary",)),
    )(o, xs, w_o, mods, n2, mods, mods, w_q, keys)
    return outs[0], outs[1], outs[2:]


def _sorting_network(n):
    def merge(lo, hi, r):
        step = r * 2
        if step < hi - lo:
            yield from merge(lo, hi, step)
            yield from merge(lo + r, hi, step)
            yield from [(i, i + r) for i in range(lo + r, hi - r, step)]
        else:
            yield (lo, lo + r)

    def sort(lo, hi):
        if hi - lo >= 1:
            mid = lo + (hi - lo) // 2
            yield from sort(lo, mid)
            yield from sort(mid + 1, hi)
            yield from merge(lo, hi, 1)

    return list(sort(0, n - 1))


def _exchange(x, i, j):
    x[i], x[j] = jnp.maximum(x[i], x[j]), jnp.minimum(x[i], x[j])


def _top_values(s):
    k = PEER_TOPK
    assert s.shape[0] == k * SUBLANES
    x = [s[i * SUBLANES:(i + 1) * SUBLANES] for i in range(k)]
    for i, j in _sorting_network(k):
        _exchange(x, i, j)
    shift = SUBLANES // 2
    while shift:
        x = [jnp.maximum(x[i], pltpu.roll(x[k - 1 - i], shift, 0)) for i in range(k)]
        d = k // 2
        while d:
            for i in range(k):
                if not i & d:
                    _exchange(x, i, i + d)
            d //= 2
        shift //= 2
    return [xi[0:1] for xi in x]


def _kth_largest(s, k):
    for _ in range(k - 1):
        s = jnp.where(s == jnp.max(s, axis=0, keepdims=True), -jnp.inf, s)
    return jnp.max(s, axis=0, keepdims=True)


def _candidate_sums(v1, v2):
    assert PEER_TOPK == 16
    a1 = jnp.concatenate(v1, axis=0)
    a2 = jnp.concatenate(v2, axis=0)
    row = lax.broadcasted_iota(jnp.int32, (8, a1.shape[1]), 0)
    ninf = -jnp.inf
    return jnp.concatenate([
        v1[0] + a2,
        v1[1] + a2[0:8],
        jnp.where(row < 5, v1[2] + a2[0:8], ninf),
        jnp.where(row < 4, v1[3] + a2[0:8], ninf),
        a1[8:16] + v2[0],
        jnp.where(row >= 4, a1[0:8] + v2[0], ninf),
        jnp.where(row >= 4, a1[0:8] + v2[1], ninf),
        jnp.where(row == 4, a1[0:8] + v2[2], ninf),
    ], axis=0)


def _peer_topk(st_ref, s2_ref, e2_ref, thr_ref, e1_ref):
    def per_head(h, carry):
        s1 = st_ref[2 * h]
        s2 = st_ref[2 * h + 1]
        v1 = _top_values(s1)
        v2 = _top_values(s2)
        cand = _candidate_sums(v1, v2)
        thr = _kth_largest(cand, PEER_TOPK)
        top = v1[0] + v2[0]
        z = jnp.sum(jnp.where(cand >= thr, jnp.exp(cand - top), 0.0), axis=0, keepdims=True)
        guard = thr - jnp.abs(thr) * THRESHOLD_SLACK
        outs = ((s2_ref, s2), (e2_ref, jnp.exp(s2 - v2[0]) * (1.0 / z)),
                (thr_ref, guard - s1), (e1_ref, jnp.exp(s1 - v1[0])))
        for ref, val in outs:
            for lb in range(val.shape[1] // SUB_LANES):
                ref[lb, h] = val[:, lb * SUB_LANES:(lb + 1) * SUB_LANES]
        return carry

    lax.fori_loop(0, PEER_HEADS, per_head, 0)


def _gelu_tanh(x):
    k = -2.0 * math.log2(math.e) * math.sqrt(2.0 / math.pi)
    return x / (1.0 + jnp.exp2(x * (k + (k * 0.044715) * (x * x))))


def _peer_dense_kernel(h2t_ref, s2_ref, e2_ref, thr_ref, e1_ref, u_ref, v_ref, xn_ref, g2_ref,
                       out_ref, acc_ref, act_ref, slab_ref):
    j = pl.program_id(1)
    lanes = acc_ref.shape[1]
    lane_blocks = lanes // SUB_LANES
    row_blocks = N_KEYS // GATE_ROWS

    @pl.when(j == 0)
    def _():
        acc_ref[...] = jnp.zeros_like(acc_ref)

    act_ref[...] = jnp.dot(u_ref[0], h2t_ref[...], preferred_element_type=F32)

    def gate_block(blk, carry):
        grp = blk // (row_blocks * lane_blocks)
        bc = (blk // lane_blocks) % row_blocks
        lc = blk % lane_blocks
        rows = pl.ds(pl.multiple_of(bc * GATE_ROWS, GATE_ROWS), GATE_ROWS)
        cols = pl.ds(pl.multiple_of(lc * SUB_LANES, SUB_LANES), SUB_LANES)
        a0 = j * PEER_CHUNK + grp * GATE_KEYS
        w = [None] * GATE_KEYS
        for h in range(PEER_HEADS):
            s2 = s2_ref[lc, h, rows, :]
            e2 = e2_ref[lc, h, rows, :]
            for al in range(GATE_KEYS):
                thr = thr_ref[lc, h, pl.ds(a0 + al, GATE_ROWS, stride=0), :]
                e1 = e1_ref[lc, h, pl.ds(a0 + al, GATE_ROWS, stride=0), :]
                term = jnp.where(s2 >= thr, e2, 0.0) * e1
                w[al] = term if h == 0 else w[al] + term
        for al in range(GATE_KEYS):
            r0 = (grp * GATE_KEYS + al) * N_KEYS + bc * GATE_ROWS
            arows = pl.ds(pl.multiple_of(r0, GATE_ROWS), GATE_ROWS)
            slab_ref[arows, cols] = (w[al] * _gelu_tanh(act_ref[arows, cols])).astype(BF16)
        return carry

    lax.fori_loop(0, (PEER_CHUNK // GATE_KEYS) * row_blocks * lane_blocks, gate_block, 0)
    acc_ref[...] += lax.dot_general(v_ref[0], slab_ref[...], (((0,), (0,)), ((), ())),
                                    preferred_element_type=F32)

    @pl.when(j == pl.num_programs(1) - 1)
    def _():
        out_ref[...] = xn_ref[...] + g2_ref[0] * acc_ref[...].T


def _peer_dense(h2t, coefs, u, v, xn, mods, layer, n_tiles, lat_tiles, tiles_per_seq, n_batch):
    d = xn.shape[1]
    t = PEER_TILE
    ce = PEER_CHUNK * N_KEYS
    n_exp = u.shape[1]

    def mrow(i):
        return layer * MOD_ROWS + jnp.where(i < lat_tiles, i // tiles_per_seq, n_batch)

    cspec = pl.BlockSpec((t // SUB_LANES, PEER_HEADS, N_KEYS, SUB_LANES), lambda i, j: (i, 0, 0, 0))
    return pl.pallas_call(
        _peer_dense_kernel,
        grid=(n_tiles, n_exp // ce),
        in_specs=[pl.BlockSpec((d, t), lambda i, j: (0, i)),
                  cspec, cspec, cspec, cspec,
                  pl.BlockSpec((1, ce, d), lambda i, j: (layer, j, 0)),
                  pl.BlockSpec((1, ce, d), lambda i, j: (layer, j, 0)),
                  pl.BlockSpec((t, d), lambda i, j: (i, 0)),
                  pl.BlockSpec((1, 1, d), lambda i, j: (mrow(i), 0, 5))],
        out_specs=pl.BlockSpec((t, d), lambda i, j: (i, 0)),
        out_shape=jax.ShapeDtypeStruct((n_tiles * t, d), F32),
        scratch_shapes=[pltpu.VMEM((d, t), F32),
                        pltpu.VMEM((ce, t), F32),
                        pltpu.VMEM((ce, t), BF16)],
        compiler_params=_cparams(("arbitrary", "arbitrary")),
    )(h2t, *coefs, u, v, xn, mods)


def _rope_tables(seq, ctx_len):
    pos = jnp.arange(seq)
    row = (pos // GRID_W).astype(F32)
    col = (pos % GRID_W).astype(F32)
    half = HEAD_DIM // 2
    inv = ROPE_THETA ** (-jnp.arange(0, half, 2, dtype=F32) / half)
    ar = row[:, None] * inv
    ac = col[:, None] * inv
    cos = jnp.concatenate([jnp.cos(ar), jnp.cos(ar), jnp.cos(ac), jnp.cos(ac)], axis=-1)
    sin = jnp.concatenate([-jnp.sin(ar), jnp.sin(ar), -jnp.sin(ac), jnp.sin(ac)], axis=-1)
    cos = jnp.concatenate([cos, jnp.ones((ctx_len, HEAD_DIM), F32)], axis=0)
    sin = jnp.concatenate([sin, jnp.zeros((ctx_len, HEAD_DIM), F32)], axis=0)
    return cos, sin


def kernel(x, c, ctx, c_ctx, ada_w, ada_b, norm1_gain, norm2_gain, w_qkv, q_norm_gain, k_norm_gain,
           w_o, attn_sinks, peer_w_q, peer_sub_keys, peer_u, peer_v):
    n_batch, seq, d = x.shape
    ctx_len = ctx.shape[1]
    depth = ada_w.shape[0]
    n_lat = n_batch * seq
    n_ctx = n_batch * ctx_len
    assert depth == 2 and d == D_MODEL and n_batch < MOD_ROWS
    assert seq % PEER_TILE == 0 and n_ctx % PEER_TILE == 0 and ctx_len == ROW_TILE
    assert seq >= Q_TILE + 2 * WINDOW and n_lat % ctx_len == 0

    cc = jnp.concatenate([c, c_ctx[None, :], jnp.zeros((MOD_ROWS - n_batch - 1, d), F32)], axis=0)
    mods = _modulation(cc, ada_w, ada_b).reshape(depth * MOD_ROWS, 1, N_MOD * d)
    cos_t, sin_t = _rope_tables(seq, ctx_len)
    u_tab = peer_u.astype(BF16)
    v_tab = peer_v.astype(BF16)
    xs = jnp.concatenate([x.reshape(n_lat, d), ctx.reshape(n_ctx, d)], axis=0)
    scale = HEAD_DIM ** -0.5 * math.log2(math.e)

    for layer in range(depth):
        last = layer == depth - 1
        q, k, vt = _qkv(xs, mods, layer, norm1_gain[layer][None, :], w_qkv[layer].astype(BF16),
                       (q_norm_gain[layer] * scale)[None, :], k_norm_gain[layer][None, :],
                       cos_t, sin_t, n_lat // ROW_TILE, seq // ROW_TILE, n_batch)
        if layer % 2 == 0:
            o = _global_attention(q, k, vt, n_batch, seq, ctx_len)
        else:
            sink = (attn_sinks[layer // 2] * math.log2(math.e)).reshape(N_KV_HEADS, 1, KV_REP, 1)
            sink_cols = jnp.broadcast_to(sink, (N_KV_HEADS, 1, KV_REP, Q_TILE))
            o = _window_attention(q, k, vt, sink_cols.reshape(N_KV_HEADS, 1, KV_REP * Q_TILE),
                                  n_batch, seq, ctx_len)
        n_rows = n_lat if last else n_lat + n_ctx
        xn, h2t, coefs = _peer_pre(o, xs, mods, layer, w_o[layer].astype(BF16),
                                norm2_gain[layer][None, :], peer_w_q[layer].astype(BF16),
                                peer_sub_keys[layer].astype(BF16), n_rows // ROW_TILE,
                                n_lat // ROW_TILE, seq // ROW_TILE, n_batch)
        xs = _peer_dense(h2t, coefs, u_tab, v_tab,
                         xn, mods, layer, n_rows // PEER_TILE, n_lat // PEER_TILE,
                         seq // PEER_TILE, n_batch)
    return xs[:n_lat].reshape(n_batch, seq, d)
```

```python
import functools
import math

import jax
import jax.numpy as jnp
from jax import lax
from jax.experimental import pallas as pl
from jax.experimental.pallas import tpu as pltpu

F32 = jnp.float32
BF16 = jnp.bfloat16

D_MODEL = 1024
N_HEADS = 8
N_KV_HEADS = 2
HEAD_DIM = 128
KV_REP = N_HEADS // N_KV_HEADS
QKV_DIM = (N_HEADS + 2 * N_KV_HEADS) * HEAD_DIM
GRID_W = 64
WINDOW = 128
ROPE_THETA = 10000.0
RMS_EPS = 1e-6
NEG_INF = -1e30
PEER_HEADS = 8
PEER_TOPK = 16
N_KEYS = 128
PEER_QDIM = 256
N_MOD = 6
MOD_ROWS = 8

ROW_TILE = 256
Q_TILE = 256
PEER_TILE = 512
PEER_CHUNK = 16
GATE_KEYS = 8
SUB_LANES = 128
GATE_ROWS = 32
SUBLANES = 8
THRESHOLD_SLACK = 2.0 ** -22
VMEM_LIMIT = 56 * 1024 * 1024


def _cparams(sem):
    return pltpu.CompilerParams(dimension_semantics=sem, vmem_limit_bytes=VMEM_LIMIT)


def _nt_dot(a, b):
    return lax.dot_general(a, b, (((1,), (1,)), ((), ())), preferred_element_type=F32)


def _mod_kernel(c_ref, w_ref, b_ref, o_ref):
    c = c_ref[...]
    a = c * (1.0 / (1.0 + jnp.exp(-c)))
    o_ref[0] = jnp.dot(a, w_ref[0], preferred_element_type=F32,
                       precision=lax.Precision.HIGHEST) + b_ref[0]


def _modulation(cc, ada_w, ada_b):
    depth, d, n = ada_w.shape
    tn = 1536
    return pl.pallas_call(
        _mod_kernel,
        grid=(depth, n // tn),
        in_specs=[pl.BlockSpec((MOD_ROWS, d), lambda l, j: (0, 0)),
                  pl.BlockSpec((1, d, tn), lambda l, j: (l, 0, j)),
                  pl.BlockSpec((1, 1, tn), lambda l, j: (l, 0, j))],
        out_specs=pl.BlockSpec((1, MOD_ROWS, tn), lambda l, j: (l, 0, j)),
        out_shape=jax.ShapeDtypeStruct((depth, MOD_ROWS, n), F32),
        compiler_params=_cparams(("arbitrary", "arbitrary")),
    )(cc, ada_w, ada_b.reshape(depth, 1, n))


def _rms(x):
    return x * lax.rsqrt(jnp.mean(x * x, axis=-1, keepdims=True) + RMS_EPS)


def _select_rows(lat_tiles, lat_ref, ctx_ref):
    return jnp.where(pl.program_id(0) < lat_tiles, lat_ref[...], ctx_ref[...])


def _row_specs(t, d, lat_tiles, ctx_off):
    return [pl.BlockSpec((t, d), lambda i: (jnp.minimum(i, lat_tiles - 1), 0)),
            pl.BlockSpec((t, d), lambda i: (ctx_off + jnp.maximum(i - lat_tiles, 0), 0))]


def _qkv_kernel(lat_tiles, xl_ref, xc_ref, sh_ref, sc_ref, g_ref, w_ref, qg_ref, kg_ref, cos_ref,
                sin_ref, q_ref, k_ref, vt_ref):
    h = _rms(_select_rows(lat_tiles, xl_ref, xc_ref)) * g_ref[...]
    h = h * (1.0 + sc_ref[0]) + sh_ref[0]
    y = jnp.dot(h.astype(BF16), w_ref[...], preferred_element_type=F32)
    cos = cos_ref[...]
    sin = sin_ref[...]
    lane = lax.broadcasted_iota(jnp.int32, cos.shape, 1)
    first = (lane % (HEAD_DIM // 2)) < (HEAD_DIM // 4)

    def head(col, gain):
        n = _rms(y[:, col * HEAD_DIM:(col + 1) * HEAD_DIM]) * gain
        sw = jnp.where(first, pltpu.roll(n, HEAD_DIM - HEAD_DIM // 4, 1),
                       pltpu.roll(n, HEAD_DIM // 4, 1))
        return n * cos + sw * sin

    for hh in range(N_HEADS):
        q_ref[:, hh * HEAD_DIM:(hh + 1) * HEAD_DIM] = head(hh, qg_ref[...]).astype(BF16)
    for g in range(N_KV_HEADS):
        k_ref[:, g * HEAD_DIM:(g + 1) * HEAD_DIM] = head(N_HEADS + g, kg_ref[...]).astype(BF16)
    v0 = (N_HEADS + N_KV_HEADS) * HEAD_DIM
    vt_ref[...] = y[:, v0:].T.astype(BF16)


def _qkv(rows, n, mods, layer, gain, w, qg, kg, cos_t, sin_t, lat_tiles, tiles_per_seq, n_batch):
    lat_src, ctx_src, ctx_off = rows
    d = lat_src.shape[1]
    t = ROW_TILE

    def mrow(i):
        return layer * MOD_ROWS + jnp.where(i < lat_tiles, i // tiles_per_seq, n_batch)

    def trow(i):
        return jnp.where(i < lat_tiles, i % tiles_per_seq, tiles_per_seq)

    kvd = N_KV_HEADS * HEAD_DIM
    return pl.pallas_call(
        functools.partial(_qkv_kernel, lat_tiles),
        grid=(n // t,),
        in_specs=_row_specs(t, d, lat_tiles, ctx_off) + [
                  pl.BlockSpec((1, 1, d), lambda i: (mrow(i), 0, 0)),
                  pl.BlockSpec((1, 1, d), lambda i: (mrow(i), 0, 1)),
                  pl.BlockSpec((1, d), lambda i: (0, 0)),
                  pl.BlockSpec((d, QKV_DIM), lambda i: (0, 0)),
                  pl.BlockSpec((1, HEAD_DIM), lambda i: (0, 0)),
                  pl.BlockSpec((1, HEAD_DIM), lambda i: (0, 0)),
                  pl.BlockSpec((t, HEAD_DIM), lambda i: (trow(i), 0)),
                  pl.BlockSpec((t, HEAD_DIM), lambda i: (trow(i), 0))],
        out_specs=[pl.BlockSpec((t, d), lambda i: (i, 0)),
                   pl.BlockSpec((t, kvd), lambda i: (i, 0)),
                   pl.BlockSpec((kvd, t), lambda i: (0, i))],
        out_shape=[jax.ShapeDtypeStruct((n, d), BF16),
                   jax.ShapeDtypeStruct((n, kvd), BF16),
                   jax.ShapeDtypeStruct((kvd, n), BF16)],
        compiler_params=_cparams(("arbitrary",)),
    )(lat_src, ctx_src, mods, mods, gain, w, qg, kg, cos_t, sin_t)


def _stack_heads(q_ref):
    return jnp.concatenate(
        [q_ref[:, r * HEAD_DIM:(r + 1) * HEAD_DIM] for r in range(KV_REP)], axis=0)


def _store_heads(ot, o_ref):
    for r in range(KV_REP):
        blk = ot[:, r * Q_TILE:(r + 1) * Q_TILE]
        o_ref[:, r * HEAD_DIM:(r + 1) * HEAD_DIM] = blk.T.astype(BF16)


def _global_attn_kernel(lat_q_tiles, q_ref, kl_ref, kc_ref, vtl_ref, vtc_ref, o_ref):
    qt = pl.program_id(2)
    q4 = _stack_heads(q_ref)
    sc = _nt_dot(kc_ref[...], q4)

    def finish(ot, den):
        _store_heads(ot * (1.0 / den), o_ref)

    @pl.when(qt < lat_q_tiles)
    def _():
        sl = _nt_dot(kl_ref[...], q4)
        m = jnp.maximum(jnp.max(sl, axis=0, keepdims=True), jnp.max(sc, axis=0, keepdims=True))
        p_l = jnp.exp2(sl - m)
        p_c = jnp.exp2(sc - m)
        den = jnp.sum(p_l, axis=0, keepdims=True) + jnp.sum(p_c, axis=0, keepdims=True)
        finish(jnp.dot(vtl_ref[...], p_l.astype(BF16), preferred_element_type=F32)
               + jnp.dot(vtc_ref[...], p_c.astype(BF16), preferred_element_type=F32), den)

    @pl.when(qt >= lat_q_tiles)
    def _():
        p_c = jnp.exp2(sc - jnp.max(sc, axis=0, keepdims=True))
        finish(jnp.dot(vtc_ref[...], p_c.astype(BF16), preferred_element_type=F32),
               jnp.sum(p_c, axis=0, keepdims=True))


def _global_attention(q, k, vt, n_batch, seq, ctx_len):
    n, d = q.shape
    lat_q = seq // Q_TILE
    ctx_q = ctx_len // Q_TILE
    ctx_blk0 = (n_batch * seq) // ctx_len
    gw = KV_REP * HEAD_DIM

    def qrow(b, g, t):
        return jnp.where(t < lat_q, b * lat_q + t, n_batch * lat_q + b * ctx_q + (t - lat_q))

    return pl.pallas_call(
        functools.partial(_global_attn_kernel, lat_q),
        grid=(n_batch, N_KV_HEADS, lat_q + ctx_q),
        in_specs=[pl.BlockSpec((Q_TILE, gw), lambda b, g, t: (qrow(b, g, t), g)),
                  pl.BlockSpec((seq, HEAD_DIM), lambda b, g, t: (b, g)),
                  pl.BlockSpec((ctx_len, HEAD_DIM), lambda b, g, t: (ctx_blk0 + b, g)),
                  pl.BlockSpec((HEAD_DIM, seq), lambda b, g, t: (g, b)),
                  pl.BlockSpec((HEAD_DIM, ctx_len), lambda b, g, t: (g, ctx_blk0 + b))],
        out_specs=pl.BlockSpec((Q_TILE, gw), lambda b, g, t: (qrow(b, g, t), g)),
        out_shape=jax.ShapeDtypeStruct((n, d), BF16),
        compiler_params=_cparams(("arbitrary", "arbitrary", "arbitrary")),
    )(q, k, k, vt, vt)


def _window_attn_kernel(seq, q_ref, kl_ref, kc_ref, vtl_ref, vtc_ref, sink_ref, o_ref):
    qt = pl.program_id(2)
    span = Q_TILE + 2 * WINDOW
    start = pl.multiple_of(jnp.clip(qt * Q_TILE - WINDOW, 0, seq - span), math.gcd(Q_TILE, WINDOW))
    q4 = _stack_heads(q_ref)
    cols = KV_REP * Q_TILE
    s_w = _nt_dot(kl_ref[pl.ds(start, span), :], q4)
    s_c = _nt_dot(kc_ref[...], q4)
    kpos = start + lax.broadcasted_iota(jnp.int32, (span, 1), 0)
    qpos = qt * Q_TILE + lax.broadcasted_iota(jnp.int32, (1, cols), 1) % Q_TILE
    s_w = jnp.where(jnp.abs(qpos - kpos) <= WINDOW, s_w, NEG_INF)
    sink = sink_ref[0]
    m = jnp.maximum(jnp.maximum(jnp.max(s_w, axis=0, keepdims=True),
                                jnp.max(s_c, axis=0, keepdims=True)), sink)
    p_w = jnp.exp2(s_w - m)
    p_c = jnp.exp2(s_c - m)
    den = (jnp.sum(p_w, axis=0, keepdims=True) + jnp.sum(p_c, axis=0, keepdims=True)
           + jnp.exp2(sink - m))
    ot = (jnp.dot(vtl_ref[:, pl.ds(start, span)], p_w.astype(BF16), preferred_element_type=F32)
          + jnp.dot(vtc_ref[...], p_c.astype(BF16), preferred_element_type=F32))
    _store_heads(ot * (1.0 / den), o_ref)


def _window_attention(q, k, vt, sink_cols, n_batch, seq, ctx_len):
    n, d = q.shape
    lat_q = seq // Q_TILE
    ctx_blk0 = (n_batch * seq) // ctx_len
    gw = KV_REP * HEAD_DIM
    return pl.pallas_call(
        functools.partial(_window_attn_kernel, seq),
        grid=(n_batch, N_KV_HEADS, lat_q),
        in_specs=[pl.BlockSpec((Q_TILE, gw), lambda b, g, t: (b * lat_q + t, g)),
                  pl.BlockSpec((seq, HEAD_DIM), lambda b, g, t: (b, g)),
                  pl.BlockSpec((ctx_len, HEAD_DIM), lambda b, g, t: (ctx_blk0 + b, g)),
                  pl.BlockSpec((HEAD_DIM, seq), lambda b, g, t: (g, b)),
                  pl.BlockSpec((HEAD_DIM, ctx_len), lambda b, g, t: (g, ctx_blk0 + b)),
                  pl.BlockSpec((1, 1, KV_REP * Q_TILE), lambda b, g, t: (g, 0, 0))],
        out_specs=pl.BlockSpec((Q_TILE, gw), lambda b, g, t: (b * lat_q + t, g)),
        out_shape=jax.ShapeDtypeStruct((n, d), BF16),
        compiler_params=_cparams(("arbitrary", "arbitrary", "arbitrary")),
    )(q, k, k, vt, vt, sink_cols)


def _peer_pre_kernel(lat_tiles, o_ref, xl_ref, xc_ref, wo_ref, g1_ref, n2_ref, sh_ref, sc_ref, wq_ref,
                     keys_ref, xn_ref, h2t_ref, s2_ref, e2_ref, thr_ref, e1_ref, st_ref):
    y = jnp.dot(o_ref[...], wo_ref[...], preferred_element_type=F32)
    xn = _select_rows(lat_tiles, xl_ref, xc_ref) + g1_ref[0] * y
    xn_ref[...] = xn
    h2 = _rms(xn) * n2_ref[...]
    h2 = h2 * (1.0 + sc_ref[0]) + sh_ref[0]
    h2t_ref[...] = h2.T.astype(BF16)
    qp = jnp.dot(h2.astype(BF16), wq_ref[...], preferred_element_type=F32)
    half = PEER_QDIM // 2
    for hs in range(2 * PEER_HEADS):
        qs = qp[:, hs * half:(hs + 1) * half].astype(BF16)
        st_ref[hs] = _nt_dot(keys_ref[hs % 2], qs)
    _peer_topk(st_ref, s2_ref, e2_ref, thr_ref, e1_ref)


def _peer_pre(o, rows, mods, layer, w_o, n2, w_q, keys, n_tiles, lat_tiles, tiles_per_seq, n_batch):
    lat_src, ctx_src, ctx_off = rows
    n, d = o.shape
    t = ROW_TILE
    qd = PEER_HEADS * PEER_QDIM

    def mrow(i):
        return layer * MOD_ROWS + jnp.where(i < lat_tiles, i // tiles_per_seq, n_batch)

    cspec = pl.BlockSpec((t // SUB_LANES, PEER_HEADS, N_KEYS, SUB_LANES), lambda i: (i, 0, 0, 0))
    cshape = jax.ShapeDtypeStruct((n // SUB_LANES, PEER_HEADS, N_KEYS, SUB_LANES), F32)
    outs = pl.pallas_call(
        functools.partial(_peer_pre_kernel, lat_tiles),
        grid=(n_tiles,),
        in_specs=[pl.BlockSpec((t, d), lambda i: (i, 0))] + _row_specs(t, d, lat_tiles, ctx_off) + [
                  pl.BlockSpec((d, d), lambda i: (0, 0)),
                  pl.BlockSpec((1, 1, d), lambda i: (mrow(i), 0, 2)),
                  pl.BlockSpec((1, d), lambda i: (0, 0)),
                  pl.BlockSpec((1, 1, d), lambda i: (mrow(i), 0, 3)),
                  pl.BlockSpec((1, 1, d), lambda i: (mrow(i), 0, 4)),
                  pl.BlockSpec((d, qd), lambda i: (0, 0)),
                  pl.BlockSpec((2, N_KEYS, PEER_QDIM // 2), lambda i: (0, 0, 0))],
        out_specs=[pl.BlockSpec((t, d), lambda i: (i, 0)),
                   pl.BlockSpec((d, t), lambda i: (0, i)),
                   cspec, cspec, cspec, cspec],
        out_shape=[jax.ShapeDtypeStruct((n, d), F32),
                   jax.ShapeDtypeStruct((d, n), BF16),
                   cshape, cshape, cshape, cshape],
        scratch_shapes=[pltpu.VMEM((2 * PEER_HEADS, N_KEYS, t), F32)],
        compiler_params=_cparams(("arbitrary",)),
    )(o, lat_src, ctx_src, w_o, mods, n2, mods, mods, w_q, keys)
    return outs[0], outs[1], outs[2:]


def _sorting_network(n):
    def merge(lo, hi, r):
        step = r * 2
        if step < hi - lo:
            yield from merge(lo, hi, step)
            yield from merge(lo + r, hi, step)
            yield from [(i, i + r) for i in range(lo + r, hi - r, step)]
        else:
            yield (lo, lo + r)

    def sort(lo, hi):
        if hi - lo >= 1:
            mid = lo + (hi - lo) // 2
            yield from sort(lo, mid)
            yield from sort(mid + 1, hi)
            yield from merge(lo, hi, 1)

    return list(sort(0, n - 1))


def _exchange(x, i, j):
    x[i], x[j] = jnp.maximum(x[i], x[j]), jnp.minimum(x[i], x[j])


def _top_values(s):
    k = PEER_TOPK
    assert s.shape[0] == k * SUBLANES
    x = [s[i * SUBLANES:(i + 1) * SUBLANES] for i in range(k)]
    for i, j in _sorting_network(k):
        _exchange(x, i, j)
    shift = SUBLANES // 2
    while shift:
        x = [jnp.maximum(x[i], pltpu.roll(x[k - 1 - i], shift, 0)) for i in range(k)]
        d = k // 2
        while d:
            for i in range(k):
                if not i & d:
                    _exchange(x, i, i + d)
            d //= 2
        shift //= 2
    return [xi[0:1] for xi in x]


def _kth_largest(s, k):
    for _ in range(k - 1):
        s = jnp.where(s == jnp.max(s, axis=0, keepdims=True), -jnp.inf, s)
    return jnp.max(s, axis=0, keepdims=True)


def _candidate_sums(v1, v2):
    assert PEER_TOPK == 16
    a1 = jnp.concatenate(v1, axis=0)
    a2 = jnp.concatenate(v2, axis=0)
    row = lax.broadcasted_iota(jnp.int32, (8, a1.shape[1]), 0)
    ninf = -jnp.inf
    return jnp.concatenate([
        v1[0] + a2,
        v1[1] + a2[0:8],
        jnp.where(row < 5, v1[2] + a2[0:8], ninf),
        jnp.where(row < 4, v1[3] + a2[0:8], ninf),
        a1[8:16] + v2[0],
        jnp.where(row >= 4, a1[0:8] + v2[0], ninf),
        jnp.where(row >= 4, a1[0:8] + v2[1], ninf),
        jnp.where(row == 4, a1[0:8] + v2[2], ninf),
    ], axis=0)


def _peer_topk(st_ref, s2_ref, e2_ref, thr_ref, e1_ref):
    def per_head(h, carry):
        s1 = st_ref[2 * h]
        s2 = st_ref[2 * h + 1]
        v1 = _top_values(s1)
        v2 = _top_values(s2)
        cand = _candidate_sums(v1, v2)
        thr = _kth_largest(cand, PEER_TOPK)
        top = v1[0] + v2[0]
        z = jnp.sum(jnp.where(cand >= thr, jnp.exp(cand - top), 0.0), axis=0, keepdims=True)
        guard = thr - jnp.abs(thr) * THRESHOLD_SLACK
        outs = ((s2_ref, s2), (e2_ref, jnp.exp(s2 - v2[0]) * (1.0 / z)),
                (thr_ref, guard - s1), (e1_ref, jnp.exp(s1 - v1[0])))
        for ref, val in outs:
            for lb in range(val.shape[1] // SUB_LANES):
                ref[lb, h] = val[:, lb * SUB_LANES:(lb + 1) * SUB_LANES]
        return carry

    lax.fori_loop(0, PEER_HEADS, per_head, 0)


def _gelu_tanh(x):
    k = -2.0 * math.log2(math.e) * math.sqrt(2.0 / math.pi)
    return x / (1.0 + jnp.exp2(x * (k + (k * 0.044715) * (x * x))))


def _peer_dense_kernel(h2t_ref, s2_ref, e2_ref, thr_ref, e1_ref, u_ref, v_ref, xn_ref, g2_ref,
                       out_ref, acc_ref, act_ref, slab_ref):
    j = pl.program_id(1)
    lanes = acc_ref.shape[1]
    lane_blocks = lanes // SUB_LANES
    row_blocks = N_KEYS // GATE_ROWS

    @pl.when(j == 0)
    def _():
        acc_ref[...] = jnp.zeros_like(acc_ref)

    act_ref[...] = jnp.dot(u_ref[0], h2t_ref[...], preferred_element_type=F32)

    def gate_block(blk, carry):
        grp = blk // (row_blocks * lane_blocks)
        bc = (blk // lane_blocks) % row_blocks
        lc = blk % lane_blocks
        rows = pl.ds(pl.multiple_of(bc * GATE_ROWS, GATE_ROWS), GATE_ROWS)
        cols = pl.ds(pl.multiple_of(lc * SUB_LANES, SUB_LANES), SUB_LANES)
        a0 = j * PEER_CHUNK + grp * GATE_KEYS
        w = [None] * GATE_KEYS
        for h in range(PEER_HEADS):
            s2 = s2_ref[lc, h, rows, :]
            e2 = e2_ref[lc, h, rows, :]
            for al in range(GATE_KEYS):
                thr = thr_ref[lc, h, pl.ds(a0 + al, GATE_ROWS, stride=0), :]
                e1 = e1_ref[lc, h, pl.ds(a0 + al, GATE_ROWS, stride=0), :]
                term = jnp.where(s2 >= thr, e2, 0.0) * e1
                w[al] = term if h == 0 else w[al] + term
        for al in range(GATE_KEYS):
            r0 = (grp * GATE_KEYS + al) * N_KEYS + bc * GATE_ROWS
            arows = pl.ds(pl.multiple_of(r0, GATE_ROWS), GATE_ROWS)
            slab_ref[arows, cols] = (w[al] * _gelu_tanh(act_ref[arows, cols])).astype(BF16)
        return carry

    lax.fori_loop(0, (PEER_CHUNK // GATE_KEYS) * row_blocks * lane_blocks, gate_block, 0)
    acc_ref[...] += lax.dot_general(v_ref[0], slab_ref[...], (((0,), (0,)), ((), ())),
                                    preferred_element_type=F32)

    @pl.when(j == pl.num_programs(1) - 1)
    def _():
        out_ref[...] = xn_ref[...] + g2_ref[0] * acc_ref[...].T


def _peer_dense(h2t, coefs, u, v, xn, mods, layer, n_tiles, lat_tiles, tiles_per_seq, n_batch):
    d = xn.shape[1]
    t = PEER_TILE
    ce = PEER_CHUNK * N_KEYS
    n_exp = u.shape[1]

    def mrow(i):
        return layer * MOD_ROWS + jnp.where(i < lat_tiles, i // tiles_per_seq, n_batch)

    cspec = pl.BlockSpec((t // SUB_LANES, PEER_HEADS, N_KEYS, SUB_LANES), lambda i, j: (i, 0, 0, 0))
    return pl.pallas_call(
        _peer_dense_kernel,
        grid=(n_tiles, n_exp // ce),
        in_specs=[pl.BlockSpec((d, t), lambda i, j: (0, i)),
                  cspec, cspec, cspec, cspec,
                  pl.BlockSpec((1, ce, d), lambda i, j: (layer, j, 0)),
                  pl.BlockSpec((1, ce, d), lambda i, j: (layer, j, 0)),
                  pl.BlockSpec((t, d), lambda i, j: (i, 0)),
                  pl.BlockSpec((1, 1, d), lambda i, j: (mrow(i), 0, 5))],
        out_specs=pl.BlockSpec((t, d), lambda i, j: (i, 0)),
        out_shape=jax.ShapeDtypeStruct((n_tiles * t, d), F32),
        scratch_shapes=[pltpu.VMEM((d, t), F32),
                        pltpu.VMEM((ce, t), F32),
                        pltpu.VMEM((ce, t), BF16)],
        compiler_params=_cparams(("arbitrary", "arbitrary")),
    )(h2t, *coefs, u, v, xn, mods)


def _rope_tables(seq, ctx_len):
    pos = jnp.arange(seq)
    row = (pos // GRID_W).astype(F32)
    col = (pos % GRID_W).astype(F32)
    half = HEAD_DIM // 2
    inv = ROPE_THETA ** (-jnp.arange(0, half, 2, dtype=F32) / half)
    ar = row[:, None] * inv
    ac = col[:, None] * inv
    cos = jnp.concatenate([jnp.cos(ar), jnp.cos(ar), jnp.cos(ac), jnp.cos(ac)], axis=-1)
    sin = jnp.concatenate([-jnp.sin(ar), jnp.sin(ar), -jnp.sin(ac), jnp.sin(ac)], axis=-1)
    cos = jnp.concatenate([cos, jnp.ones((ctx_len, HEAD_DIM), F32)], axis=0)
    sin = jnp.concatenate([sin, jnp.zeros((ctx_len, HEAD_DIM), F32)], axis=0)
    return cos, sin


def kernel(x, c, ctx, c_ctx, ada_w, ada_b, norm1_gain, norm2_gain, w_qkv, q_norm_gain, k_norm_gain,
           w_o, attn_sinks, peer_w_q, peer_sub_keys, peer_u, peer_v):
    n_batch, seq, d = x.shape
    ctx_len = ctx.shape[1]
    depth = ada_w.shape[0]
    n_lat = n_batch * seq
    n_ctx = n_batch * ctx_len
    assert depth == 2 and d == D_MODEL and n_batch < MOD_ROWS
    assert seq % PEER_TILE == 0 and n_ctx % PEER_TILE == 0 and ctx_len == ROW_TILE
    assert seq >= Q_TILE + 2 * WINDOW and n_lat % ctx_len == 0

    cc = jnp.concatenate([c, c_ctx[None, :], jnp.zeros((MOD_ROWS - n_batch - 1, d), F32)], axis=0)
    mods = _modulation(cc, ada_w, ada_b).reshape(depth * MOD_ROWS, 1, N_MOD * d)
    cos_t, sin_t = _rope_tables(seq, ctx_len)
    u_tab = peer_u.astype(BF16)
    v_tab = peer_v.astype(BF16)
    rows = (x.reshape(n_lat, d), ctx.reshape(n_ctx, d), 0)
    scale = HEAD_DIM ** -0.5 * math.log2(math.e)

    for layer in range(depth):
        last = layer == depth - 1
        q, k, vt = _qkv(rows, n_lat + n_ctx, mods, layer, norm1_gain[layer][None, :], w_qkv[layer].astype(BF16),
                       (q_norm_gain[layer] * scale)[None, :], k_norm_gain[layer][None, :],
                       cos_t, sin_t, n_lat // ROW_TILE, seq // ROW_TILE, n_batch)
        if layer % 2 == 0:
            o = _global_attention(q, k, vt, n_batch, seq, ctx_len)
        else:
            sink = (attn_sinks[layer // 2] * math.log2(math.e)).reshape(N_KV_HEADS, 1, KV_REP, 1)
            sink_cols = jnp.broadcast_to(sink, (N_KV_HEADS, 1, KV_REP, Q_TILE))
            o = _window_attention(q, k, vt, sink_cols.reshape(N_KV_HEADS, 1, KV_REP * Q_TILE),
                                  n_batch, seq, ctx_len)
        n_rows = n_lat if last else n_lat + n_ctx
        xn, h2t, coefs = _peer_pre(o, rows, mods, layer, w_o[layer].astype(BF16),
                                norm2_gain[layer][None, :], peer_w_q[layer].astype(BF16),
                                peer_sub_keys[layer].astype(BF16), n_rows // ROW_TILE,
                                n_lat // ROW_TILE, seq // ROW_TILE, n_batch)
        xs = _peer_dense(h2t, coefs, u_tab, v_tab,
                         xn, mods, layer, n_rows // PEER_TILE, n_lat // PEER_TILE,
                         seq // PEER_TILE, n_batch)
        rows = (xs, xs, n_lat // ROW_TILE)
    return xs[:n_lat].reshape(n_batch, seq, d)
```

```python
import functools
import math

import jax
import jax.numpy as jnp
from jax import lax
from jax.experimental import pallas as pl
from jax.experimental.pallas import tpu as pltpu

F32 = jnp.float32
BF16 = jnp.bfloat16

D_MODEL = 1024
N_HEADS = 8
N_KV_HEADS = 2
HEAD_DIM = 128
KV_REP = N_HEADS // N_KV_HEADS
QKV_DIM = (N_HEADS + 2 * N_KV_HEADS) * HEAD_DIM
GRID_W = 64
WINDOW = 128
ROPE_THETA = 10000.0
RMS_EPS = 1e-6
NEG_INF = -1e30
PEER_HEADS = 8
PEER_TOPK = 16
N_KEYS = 128
PEER_QDIM = 256
N_MOD = 6
MOD_ROWS = 8

ROW_TILE = 256
Q_TILE = 256
PEER_TILE = 512
PEER_CHUNK = 16
GATE_KEYS = 8
SUB_LANES = 128
GATE_ROWS = 32
SUBLANES = 8
MOD_COLS = 1536
THRESHOLD_SLACK = 2.0 ** -22
VMEM_LIMIT = 56 * 1024 * 1024


def _cparams(sem):
    return pltpu.CompilerParams(dimension_semantics=sem, vmem_limit_bytes=VMEM_LIMIT)


def _nt_dot(a, b):
    return lax.dot_general(a, b, (((1,), (1,)), ((), ())), preferred_element_type=F32)


def _mod_kernel(c_ref, w_ref, b_ref, o_ref):
    c = c_ref[...]
    a = c * (1.0 / (1.0 + jnp.exp(-c)))
    o_ref[0] = jnp.dot(a, w_ref[0], preferred_element_type=F32,
                       precision=lax.Precision.HIGHEST) + b_ref[0]


def _modulation(cc, ada_w, ada_b):
    depth, d, n = ada_w.shape
    tn = MOD_COLS
    return pl.pallas_call(
        _mod_kernel,
        grid=(depth, n // tn),
        in_specs=[pl.BlockSpec((MOD_ROWS, d), lambda l, j: (0, 0)),
                  pl.BlockSpec((1, d, tn), lambda l, j: (l, 0, j)),
                  pl.BlockSpec((1, 1, tn), lambda l, j: (l, 0, j))],
        out_specs=pl.BlockSpec((1, MOD_ROWS, tn), lambda l, j: (l, 0, j)),
        out_shape=jax.ShapeDtypeStruct((depth, MOD_ROWS, n), F32),
        compiler_params=_cparams(("arbitrary", "arbitrary")),
    )(cc, ada_w, ada_b.reshape(depth, 1, n))


def _rms(x):
    return x * lax.rsqrt(jnp.mean(x * x, axis=-1, keepdims=True) + RMS_EPS)


def _select_rows(lat_tiles, lat_ref, ctx_ref):
    return jnp.where(pl.program_id(0) < lat_tiles, lat_ref[...], ctx_ref[...])


def _row_specs(t, d, lat_tiles, ctx_off):
    return [pl.BlockSpec((t, d), lambda i: (jnp.minimum(i, lat_tiles - 1), 0)),
            pl.BlockSpec((t, d), lambda i: (ctx_off + jnp.maximum(i - lat_tiles, 0), 0))]


def _qkv_kernel(lat_tiles, xl_ref, xc_ref, sh_ref, sc_ref, g_ref, w_ref, qg_ref, kg_ref, cos_ref,
                sin_ref, q_ref, k_ref, vt_ref):
    h = _rms(_select_rows(lat_tiles, xl_ref, xc_ref)) * g_ref[...]
    h = h * (1.0 + sc_ref[0]) + sh_ref[0]
    y = jnp.dot(h.astype(BF16), w_ref[...], preferred_element_type=F32)
    cos = cos_ref[...]
    sin = sin_ref[...]
    lane = lax.broadcasted_iota(jnp.int32, cos.shape, 1)
    first = (lane % (HEAD_DIM // 2)) < (HEAD_DIM // 4)

    def head(col, gain):
        n = _rms(y[:, col * HEAD_DIM:(col + 1) * HEAD_DIM]) * gain
        sw = jnp.where(first, pltpu.roll(n, HEAD_DIM - HEAD_DIM // 4, 1),
                       pltpu.roll(n, HEAD_DIM // 4, 1))
        return n * cos + sw * sin

    for hh in range(N_HEADS):
        q_ref[:, hh * HEAD_DIM:(hh + 1) * HEAD_DIM] = head(hh, qg_ref[...]).astype(BF16)
    for g in range(N_KV_HEADS):
        k_ref[:, g * HEAD_DIM:(g + 1) * HEAD_DIM] = head(N_HEADS + g, kg_ref[...]).astype(BF16)
    v0 = (N_HEADS + N_KV_HEADS) * HEAD_DIM
    vt_ref[...] = y[:, v0:].T.astype(BF16)


def _qkv(rows, n, mods, layer, gain, w, qg, kg, cos_t, sin_t, lat_tiles, tiles_per_seq, n_batch):
    lat_src, ctx_src, ctx_off = rows
    d = lat_src.shape[1]
    t = ROW_TILE

    def mrow(i):
        return layer * MOD_ROWS + jnp.where(i < lat_tiles, i // tiles_per_seq, n_batch)

    def trow(i):
        return jnp.where(i < lat_tiles, i % tiles_per_seq, tiles_per_seq)

    kvd = N_KV_HEADS * HEAD_DIM
    return pl.pallas_call(
        functools.partial(_qkv_kernel, lat_tiles),
        grid=(n // t,),
        in_specs=_row_specs(t, d, lat_tiles, ctx_off) + [
                  pl.BlockSpec((1, 1, d), lambda i: (mrow(i), 0, 0)),
                  pl.BlockSpec((1, 1, d), lambda i: (mrow(i), 0, 1)),
                  pl.BlockSpec((1, d), lambda i: (0, 0)),
                  pl.BlockSpec((d, QKV_DIM), lambda i: (0, 0)),
                  pl.BlockSpec((1, HEAD_DIM), lambda i: (0, 0)),
                  pl.BlockSpec((1, HEAD_DIM), lambda i: (0, 0)),
                  pl.BlockSpec((t, HEAD_DIM), lambda i: (trow(i), 0)),
                  pl.BlockSpec((t, HEAD_DIM), lambda i: (trow(i), 0))],
        out_specs=[pl.BlockSpec((t, d), lambda i: (i, 0)),
                   pl.BlockSpec((t, kvd), lambda i: (i, 0)),
                   pl.BlockSpec((kvd, t), lambda i: (0, i))],
        out_shape=[jax.ShapeDtypeStruct((n, d), BF16),
                   jax.ShapeDtypeStruct((n, kvd), BF16),
                   jax.ShapeDtypeStruct((kvd, n), BF16)],
        compiler_params=_cparams(("arbitrary",)),
    )(lat_src, ctx_src, mods, mods, gain, w, qg, kg, cos_t, sin_t)


def _stack_heads(q_ref):
    return jnp.concatenate(
        [q_ref[:, r * HEAD_DIM:(r + 1) * HEAD_DIM] for r in range(KV_REP)], axis=0)


def _store_heads(ot, o_ref):
    for r in range(KV_REP):
        blk = ot[:, r * Q_TILE:(r + 1) * Q_TILE]
        o_ref[:, r * HEAD_DIM:(r + 1) * HEAD_DIM] = blk.T.astype(BF16)


def _global_attn_kernel(lat_q_tiles, q_ref, kl_ref, kc_ref, vtl_ref, vtc_ref, o_ref):
    qt = pl.program_id(2)
    q4 = _stack_heads(q_ref)
    sc = _nt_dot(kc_ref[...], q4)

    def finish(ot, den):
        _store_heads(ot * (1.0 / den), o_ref)

    @pl.when(qt < lat_q_tiles)
    def _():
        sl = _nt_dot(kl_ref[...], q4)
        m = jnp.maximum(jnp.max(sl, axis=0, keepdims=True), jnp.max(sc, axis=0, keepdims=True))
        p_l = jnp.exp2(sl - m)
        p_c = jnp.exp2(sc - m)
        den = jnp.sum(p_l, axis=0, keepdims=True) + jnp.sum(p_c, axis=0, keepdims=True)
        finish(jnp.dot(vtl_ref[...], p_l.astype(BF16), preferred_element_type=F32)
               + jnp.dot(vtc_ref[...], p_c.astype(BF16), preferred_element_type=F32), den)

    @pl.when(qt >= lat_q_tiles)
    def _():
        p_c = jnp.exp2(sc - jnp.max(sc, axis=0, keepdims=True))
        finish(jnp.dot(vtc_ref[...], p_c.astype(BF16), preferred_element_type=F32),
               jnp.sum(p_c, axis=0, keepdims=True))


def _global_attention(q, k, vt, n_batch, seq, ctx_len):
    n, d = q.shape
    lat_q = seq // Q_TILE
    ctx_q = ctx_len // Q_TILE
    ctx_blk0 = (n_batch * seq) // ctx_len
    gw = KV_REP * HEAD_DIM

    def qrow(b, g, t):
        return jnp.where(t < lat_q, b * lat_q + t, n_batch * lat_q + b * ctx_q + (t - lat_q))

    return pl.pallas_call(
        functools.partial(_global_attn_kernel, lat_q),
        grid=(n_batch, N_KV_HEADS, lat_q + ctx_q),
        in_specs=[pl.BlockSpec((Q_TILE, gw), lambda b, g, t: (qrow(b, g, t), g)),
                  pl.BlockSpec((seq, HEAD_DIM), lambda b, g, t: (b, g)),
                  pl.BlockSpec((ctx_len, HEAD_DIM), lambda b, g, t: (ctx_blk0 + b, g)),
                  pl.BlockSpec((HEAD_DIM, seq), lambda b, g, t: (g, b)),
                  pl.BlockSpec((HEAD_DIM, ctx_len), lambda b, g, t: (g, ctx_blk0 + b))],
        out_specs=pl.BlockSpec((Q_TILE, gw), lambda b, g, t: (qrow(b, g, t), g)),
        out_shape=jax.ShapeDtypeStruct((n, d), BF16),
        compiler_params=_cparams(("arbitrary", "arbitrary", "arbitrary")),
    )(q, k, k, vt, vt)


def _window_attn_kernel(seq, q_ref, kl_ref, kc_ref, vtl_ref, vtc_ref, sink_ref, o_ref):
    qt = pl.program_id(2)
    span = Q_TILE + 2 * WINDOW
    start = pl.multiple_of(jnp.clip(qt * Q_TILE - WINDOW, 0, seq - span), math.gcd(Q_TILE, WINDOW))
    q4 = _stack_heads(q_ref)
    cols = KV_REP * Q_TILE
    s_w = _nt_dot(kl_ref[pl.ds(start, span), :], q4)
    s_c = _nt_dot(kc_ref[...], q4)
    kpos = start + lax.broadcasted_iota(jnp.int32, (span, 1), 0)
    qpos = qt * Q_TILE + lax.broadcasted_iota(jnp.int32, (1, cols), 1) % Q_TILE
    s_w = jnp.where(jnp.abs(qpos - kpos) <= WINDOW, s_w, NEG_INF)
    sink = sink_ref[0]
    m = jnp.maximum(jnp.maximum(jnp.max(s_w, axis=0, keepdims=True),
                                jnp.max(s_c, axis=0, keepdims=True)), sink)
    p_w = jnp.exp2(s_w - m)
    p_c = jnp.exp2(s_c - m)
    den = (jnp.sum(p_w, axis=0, keepdims=True) + jnp.sum(p_c, axis=0, keepdims=True)
           + jnp.exp2(sink - m))
    ot = (jnp.dot(vtl_ref[:, pl.ds(start, span)], p_w.astype(BF16), preferred_element_type=F32)
          + jnp.dot(vtc_ref[...], p_c.astype(BF16), preferred_element_type=F32))
    _store_heads(ot * (1.0 / den), o_ref)


def _window_attention(q, k, vt, sink_cols, n_batch, seq, ctx_len):
    n, d = q.shape
    lat_q = seq // Q_TILE
    ctx_blk0 = (n_batch * seq) // ctx_len
    gw = KV_REP * HEAD_DIM
    return pl.pallas_call(
        functools.partial(_window_attn_kernel, seq),
        grid=(n_batch, N_KV_HEADS, lat_q),
        in_specs=[pl.BlockSpec((Q_TILE, gw), lambda b, g, t: (b * lat_q + t, g)),
                  pl.BlockSpec((seq, HEAD_DIM), lambda b, g, t: (b, g)),
                  pl.BlockSpec((ctx_len, HEAD_DIM), lambda b, g, t: (ctx_blk0 + b, g)),
                  pl.BlockSpec((HEAD_DIM, seq), lambda b, g, t: (g, b)),
                  pl.BlockSpec((HEAD_DIM, ctx_len), lambda b, g, t: (g, ctx_blk0 + b)),
                  pl.BlockSpec((1, 1, KV_REP * Q_TILE), lambda b, g, t: (g, 0, 0))],
        out_specs=pl.BlockSpec((Q_TILE, gw), lambda b, g, t: (b * lat_q + t, g)),
        out_shape=jax.ShapeDtypeStruct((n, d), BF16),
        compiler_params=_cparams(("arbitrary", "arbitrary", "arbitrary")),
    )(q, k, k, vt, vt, sink_cols)


def _peer_pre_kernel(lat_tiles, o_ref, xl_ref, xc_ref, wo_ref, g1_ref, n2_ref, sh_ref, sc_ref, wq_ref,
                     keys_ref, xn_ref, h2t_ref, s2_ref, e2_ref, thr_ref, e1_ref, st_ref):
    y = jnp.dot(o_ref[...], wo_ref[...], preferred_element_type=F32)
    xn = _select_rows(lat_tiles, xl_ref, xc_ref) + g1_ref[0] * y
    xn_ref[...] = xn
    h2 = _rms(xn) * n2_ref[...]
    h2 = h2 * (1.0 + sc_ref[0]) + sh_ref[0]
    h2t_ref[...] = h2.T.astype(BF16)
    qp = jnp.dot(h2.astype(BF16), wq_ref[...], preferred_element_type=F32)
    half = PEER_QDIM // 2
    for hs in range(2 * PEER_HEADS):
        qs = qp[:, hs * half:(hs + 1) * half].astype(BF16)
        st_ref[hs] = _nt_dot(keys_ref[hs % 2], qs)
    _peer_topk(st_ref, s2_ref, e2_ref, thr_ref, e1_ref)


def _peer_pre(o, rows, mods, layer, w_o, n2, w_q, keys, n_tiles, lat_tiles, tiles_per_seq, n_batch):
    lat_src, ctx_src, ctx_off = rows
    n, d = o.shape
    t = ROW_TILE
    qd = PEER_HEADS * PEER_QDIM

    def mrow(i):
        return layer * MOD_ROWS + jnp.where(i < lat_tiles, i // tiles_per_seq, n_batch)

    cspec = pl.BlockSpec((t // SUB_LANES, PEER_HEADS, N_KEYS, SUB_LANES), lambda i: (i, 0, 0, 0))
    cshape = jax.ShapeDtypeStruct((n // SUB_LANES, PEER_HEADS, N_KEYS, SUB_LANES), F32)
    outs = pl.pallas_call(
        functools.partial(_peer_pre_kernel, lat_tiles),
        grid=(n_tiles,),
        in_specs=[pl.BlockSpec((t, d), lambda i: (i, 0))] + _row_specs(t, d, lat_tiles, ctx_off) + [
                  pl.BlockSpec((d, d), lambda i: (0, 0)),
                  pl.BlockSpec((1, 1, d), lambda i: (mrow(i), 0, 2)),
                  pl.BlockSpec((1, d), lambda i: (0, 0)),
                  pl.BlockSpec((1, 1, d), lambda i: (mrow(i), 0, 3)),
                  pl.BlockSpec((1, 1, d), lambda i: (mrow(i), 0, 4)),
                  pl.BlockSpec((d, qd), lambda i: (0, 0)),
                  pl.BlockSpec((2, N_KEYS, PEER_QDIM // 2), lambda i: (0, 0, 0))],
        out_specs=[pl.BlockSpec((t, d), lambda i: (i, 0)),
                   pl.BlockSpec((d, t), lambda i: (0, i)),
                   cspec, cspec, cspec, cspec],
        out_shape=[jax.ShapeDtypeStruct((n, d), F32),
                   jax.ShapeDtypeStruct((d, n), BF16),
                   cshape, cshape, cshape, cshape],
        scratch_shapes=[pltpu.VMEM((2 * PEER_HEADS, N_KEYS, t), F32)],
        compiler_params=_cparams(("arbitrary",)),
    )(o, lat_src, ctx_src, w_o, mods, n2, mods, mods, w_q, keys)
    return outs[0], outs[1], outs[2:]


def _sorting_network(n):
    def merge(lo, hi, r):
        step = r * 2
        if step < hi - lo:
            yield from merge(lo, hi, step)
            yield from merge(lo + r, hi, step)
            yield from [(i, i + r) for i in range(lo + r, hi - r, step)]
        else:
            yield (lo, lo + r)

    def sort(lo, hi):
        if hi - lo >= 1:
            mid = lo + (hi - lo) // 2
            yield from sort(lo, mid)
            yield from sort(mid + 1, hi)
            yield from merge(lo, hi, 1)

    return list(sort(0, n - 1))


def _exchange(x, i, j):
    x[i], x[j] = jnp.maximum(x[i], x[j]), jnp.minimum(x[i], x[j])


def _top_values(s):
    k = PEER_TOPK
    assert s.shape[0] == k * SUBLANES
    x = [s[i * SUBLANES:(i + 1) * SUBLANES] for i in range(k)]
    for i, j in _sorting_network(k):
        _exchange(x, i, j)
    shift = SUBLANES // 2
    while shift:
        x = [jnp.maximum(x[i], pltpu.roll(x[k - 1 - i], shift, 0)) for i in range(k)]
        d = k // 2
        while d:
            for i in range(k):
                if not i & d:
                    _exchange(x, i, i + d)
            d //= 2
        shift //= 2
    return [xi[0:1] for xi in x]


def _kth_largest(s, k):
    for _ in range(k - 1):
        s = jnp.where(s == jnp.max(s, axis=0, keepdims=True), -jnp.inf, s)
    return jnp.max(s, axis=0, keepdims=True)


def _candidate_sums(v1, v2):
    assert PEER_TOPK == 16
    a1 = jnp.concatenate(v1, axis=0)
    a2 = jnp.concatenate(v2, axis=0)
    row = lax.broadcasted_iota(jnp.int32, (8, a1.shape[1]), 0)
    ninf = -jnp.inf
    return jnp.concatenate([
        v1[0] + a2,
        v1[1] + a2[0:8],
        jnp.where(row < 5, v1[2] + a2[0:8], ninf),
        jnp.where(row < 4, v1[3] + a2[0:8], ninf),
        a1[8:16] + v2[0],
        jnp.where(row >= 4, a1[0:8] + v2[0], ninf),
        jnp.where(row >= 4, a1[0:8] + v2[1], ninf),
        jnp.where(row == 4, a1[0:8] + v2[2], ninf),
    ], axis=0)


def _peer_topk(st_ref, s2_ref, e2_ref, thr_ref, e1_ref):
    def per_head(h, carry):
        s1 = st_ref[2 * h]
        s2 = st_ref[2 * h + 1]
        v1 = _top_values(s1)
        v2 = _top_values(s2)
        cand = _candidate_sums(v1, v2)
        thr = _kth_largest(cand, PEER_TOPK)
        top = v1[0] + v2[0]
        z = jnp.sum(jnp.where(cand >= thr, jnp.exp(cand - top), 0.0), axis=0, keepdims=True)
        guard = thr - jnp.abs(thr) * THRESHOLD_SLACK
        outs = ((s2_ref, s2), (e2_ref, jnp.exp(s2 - v2[0]) * (1.0 / z)),
                (thr_ref, guard - s1), (e1_ref, jnp.exp(s1 - v1[0])))
        for ref, val in outs:
            for lb in range(val.shape[1] // SUB_LANES):
                ref[lb, h] = val[:, lb * SUB_LANES:(lb + 1) * SUB_LANES]
        return carry

    lax.fori_loop(0, PEER_HEADS, per_head, 0)


def _gelu_tanh(x):
    k = -2.0 * math.log2(math.e) * math.sqrt(2.0 / math.pi)
    return x / (1.0 + jnp.exp2(x * (k + (k * 0.044715) * (x * x))))


def _peer_dense_kernel(h2t_ref, s2_ref, e2_ref, thr_ref, e1_ref, u_ref, v_ref, xn_ref, g2_ref,
                       out_ref, acc_ref, act_ref, slab_ref):
    j = pl.program_id(1)
    lanes = acc_ref.shape[1]
    lane_blocks = lanes // SUB_LANES
    row_blocks = N_KEYS // GATE_ROWS

    @pl.when(j == 0)
    def _():
        acc_ref[...] = jnp.zeros_like(acc_ref)

    act_ref[...] = jnp.dot(u_ref[0], h2t_ref[...], preferred_element_type=F32)

    def gate_block(blk, carry):
        grp = blk // (row_blocks * lane_blocks)
        bc = (blk // lane_blocks) % row_blocks
        lc = blk % lane_blocks
        rows = pl.ds(pl.multiple_of(bc * GATE_ROWS, GATE_ROWS), GATE_ROWS)
        cols = pl.ds(pl.multiple_of(lc * SUB_LANES, SUB_LANES), SUB_LANES)
        a0 = j * PEER_CHUNK + grp * GATE_KEYS
        w = [None] * GATE_KEYS
        for h in range(PEER_HEADS):
            s2 = s2_ref[lc, h, rows, :]
            e2 = e2_ref[lc, h, rows, :]
            for al in range(GATE_KEYS):
                thr = thr_ref[lc, h, pl.ds(a0 + al, GATE_ROWS, stride=0), :]
                e1 = e1_ref[lc, h, pl.ds(a0 + al, GATE_ROWS, stride=0), :]
                term = jnp.where(s2 >= thr, e2, 0.0) * e1
                w[al] = term if h == 0 else w[al] + term
        for al in range(GATE_KEYS):
            r0 = (grp * GATE_KEYS + al) * N_KEYS + bc * GATE_ROWS
            arows = pl.ds(pl.multiple_of(r0, GATE_ROWS), GATE_ROWS)
            slab_ref[arows, cols] = (w[al] * _gelu_tanh(act_ref[arows, cols])).astype(BF16)
        return carry

    lax.fori_loop(0, (PEER_CHUNK // GATE_KEYS) * row_blocks * lane_blocks, gate_block, 0)
    acc_ref[...] += lax.dot_general(v_ref[0], slab_ref[...], (((0,), (0,)), ((), ())),
                                    preferred_element_type=F32)

    @pl.when(j == pl.num_programs(1) - 1)
    def _():
        out_ref[...] = xn_ref[...] + g2_ref[0] * acc_ref[...].T


def _peer_dense(h2t, coefs, u, v, xn, mods, layer, n_tiles, lat_tiles, tiles_per_seq, n_batch):
    d = xn.shape[1]
    t = PEER_TILE
    ce = PEER_CHUNK * N_KEYS
    n_exp = u.shape[1]

    def mrow(i):
        return layer * MOD_ROWS + jnp.where(i < lat_tiles, i // tiles_per_seq, n_batch)

    cspec = pl.BlockSpec((t // SUB_LANES, PEER_HEADS, N_KEYS, SUB_LANES), lambda i, j: (i, 0, 0, 0))
    return pl.pallas_call(
        _peer_dense_kernel,
        grid=(n_tiles, n_exp // ce),
        in_specs=[pl.BlockSpec((d, t), lambda i, j: (0, i)),
                  cspec, cspec, cspec, cspec,
                  pl.BlockSpec((1, ce, d), lambda i, j: (layer, j, 0)),
                  pl.BlockSpec((1, ce, d), lambda i, j: (layer, j, 0)),
                  pl.BlockSpec((t, d), lambda i, j: (i, 0)),
                  pl.BlockSpec((1, 1, d), lambda i, j: (mrow(i), 0, 5))],
        out_specs=pl.BlockSpec((t, d), lambda i, j: (i, 0)),
        out_shape=jax.ShapeDtypeStruct((n_tiles * t, d), F32),
        scratch_shapes=[pltpu.VMEM((d, t), F32),
                        pltpu.VMEM((ce, t), F32),
                        pltpu.VMEM((ce, t), BF16)],
        compiler_params=_cparams(("arbitrary", "arbitrary")),
    )(h2t, *coefs, u, v, xn, mods)


def _rope_tables(seq, ctx_len):
    pos = jnp.arange(seq)
    row = (pos // GRID_W).astype(F32)
    col = (pos % GRID_W).astype(F32)
    half = HEAD_DIM // 2
    inv = ROPE_THETA ** (-jnp.arange(0, half, 2, dtype=F32) / half)
    ar = row[:, None] * inv
    ac = col[:, None] * inv
    cos = jnp.concatenate([jnp.cos(ar), jnp.cos(ar), jnp.cos(ac), jnp.cos(ac)], axis=-1)
    sin = jnp.concatenate([-jnp.sin(ar), jnp.sin(ar), -jnp.sin(ac), jnp.sin(ac)], axis=-1)
    cos = jnp.concatenate([cos, jnp.ones((ctx_len, HEAD_DIM), F32)], axis=0)
    sin = jnp.concatenate([sin, jnp.zeros((ctx_len, HEAD_DIM), F32)], axis=0)
    return cos, sin


def kernel(x, c, ctx, c_ctx, ada_w, ada_b, norm1_gain, norm2_gain, w_qkv, q_norm_gain, k_norm_gain,
           w_o, attn_sinks, peer_w_q, peer_sub_keys, peer_u, peer_v):
    n_batch, seq, d = x.shape
    ctx_len = ctx.shape[1]
    depth = ada_w.shape[0]
    n_lat = n_batch * seq
    n_ctx = n_batch * ctx_len
    assert depth == 2 and d == D_MODEL and n_batch < MOD_ROWS
    assert seq % PEER_TILE == 0 and n_ctx % PEER_TILE == 0 and ctx_len == ROW_TILE
    assert seq >= Q_TILE + 2 * WINDOW and n_lat % ctx_len == 0
    assert seq % Q_TILE == 0 and ctx_len % Q_TILE == 0 and (N_MOD * d) % MOD_COLS == 0

    cc = jnp.concatenate([c, c_ctx[None, :], jnp.zeros((MOD_ROWS - n_batch - 1, d), F32)], axis=0)
    mods = _modulation(cc, ada_w, ada_b).reshape(depth * MOD_ROWS, 1, N_MOD * d)
    cos_t, sin_t = _rope_tables(seq, ctx_len)
    u_tab = peer_u.astype(BF16)
    v_tab = peer_v.astype(BF16)
    rows = (x.reshape(n_lat, d), ctx.reshape(n_ctx, d), 0)
    scale = HEAD_DIM ** -0.5 * math.log2(math.e)

    for layer in range(depth):
        last = layer == depth - 1
        q, k, vt = _qkv(rows, n_lat + n_ctx, mods, layer, norm1_gain[layer][None, :],
                        w_qkv[layer].astype(BF16),
                        (q_norm_gain[layer] * scale)[None, :], k_norm_gain[layer][None, :],
                        cos_t, sin_t, n_lat // ROW_TILE, seq // ROW_TILE, n_batch)
        if layer % 2 == 0:
            o = _global_attention(q, k, vt, n_batch, seq, ctx_len)
        else:
            sink = (attn_sinks[layer // 2] * math.log2(math.e)).reshape(N_KV_HEADS, 1, KV_REP, 1)
            sink_cols = jnp.broadcast_to(sink, (N_KV_HEADS, 1, KV_REP, Q_TILE))
            o = _window_attention(q, k, vt, sink_cols.reshape(N_KV_HEADS, 1, KV_REP * Q_TILE),
                                  n_batch, seq, ctx_len)
        n_rows = n_lat if last else n_lat + n_ctx
        xn, h2t, coefs = _peer_pre(o, rows, mods, layer, w_o[layer].astype(BF16),
                                norm2_gain[layer][None, :], peer_w_q[layer].astype(BF16),
                                peer_sub_keys[layer].astype(BF16), n_rows // ROW_TILE,
                                n_lat // ROW_TILE, seq // ROW_TILE, n_batch)
        xs = _peer_dense(h2t, coefs, u_tab, v_tab,
                         xn, mods, layer, n_rows // PEER_TILE, n_lat // PEER_TILE,
                         seq // PEER_TILE, n_batch)
        rows = (xs, xs, n_lat // ROW_TILE)
    return xs[:n_lat].reshape(n_batch, seq, d)
```

```python
import functools
import math

import jax
import jax.numpy as jnp
from jax import lax
from jax.experimental import pallas as pl
from jax.experimental.pallas import tpu as pltpu

F32 = jnp.float32
BF16 = jnp.bfloat16

D_MODEL = 1024
N_HEADS = 8
N_KV_HEADS = 2
HEAD_DIM = 128
KV_REP = N_HEADS // N_KV_HEADS
QKV_DIM = (N_HEADS + 2 * N_KV_HEADS) * HEAD_DIM
GRID_W = 64
WINDOW = 128
ROPE_THETA = 10000.0
RMS_EPS = 1e-6
NEG_INF = -1e30
PEER_HEADS = 8
PEER_TOPK = 16
N_KEYS = 128
PEER_QDIM = 256
N_MOD = 6
MOD_ROWS = 8

ROW_TILE = 256
Q_TILE = 256
PEER_TILE = 512
PEER_CHUNK = 16
GATE_KEYS = 8
SUB_LANES = 128
GATE_ROWS = 32
SUBLANES = 8
MOD_COLS = 1536
THRESHOLD_SLACK = 2.0 ** -22
VMEM_LIMIT = 56 * 1024 * 1024


def _cparams(sem):
    return pltpu.CompilerParams(dimension_semantics=sem, vmem_limit_bytes=VMEM_LIMIT)


def _nt_dot(a, b):
    return lax.dot_general(a, b, (((1,), (1,)), ((), ())), preferred_element_type=F32)


def _mod_kernel(c_ref, w_ref, b_ref, o_ref):
    c = c_ref[...]
    a = c * (1.0 / (1.0 + jnp.exp(-c)))
    o_ref[0] = jnp.dot(a, w_ref[0], preferred_element_type=F32,
                       precision=lax.Precision.HIGHEST) + b_ref[0]


def _modulation(cc, ada_w, ada_b):
    depth, d, n = ada_w.shape
    tn = MOD_COLS
    return pl.pallas_call(
        _mod_kernel,
        grid=(depth, n // tn),
        in_specs=[pl.BlockSpec((MOD_ROWS, d), lambda l, j: (0, 0)),
                  pl.BlockSpec((1, d, tn), lambda l, j: (l, 0, j)),
                  pl.BlockSpec((1, 1, tn), lambda l, j: (l, 0, j))],
        out_specs=pl.BlockSpec((1, MOD_ROWS, tn), lambda l, j: (l, 0, j)),
        out_shape=jax.ShapeDtypeStruct((depth, MOD_ROWS, n), F32),
        compiler_params=_cparams(("arbitrary", "arbitrary")),
    )(cc, ada_w, ada_b.reshape(depth, 1, n))


def _rms(x):
    return x * lax.rsqrt(jnp.mean(x * x, axis=-1, keepdims=True) + RMS_EPS)


def _select_rows(lat_tiles, lat_ref, ctx_ref):
    return jnp.where(pl.program_id(0) < lat_tiles, lat_ref[...], ctx_ref[...])


def _row_specs(t, d, lat_tiles, ctx_off):
    return [pl.BlockSpec((t, d), lambda i: (jnp.minimum(i, lat_tiles - 1), 0)),
            pl.BlockSpec((t, d), lambda i: (ctx_off + jnp.maximum(i - lat_tiles, 0), 0))]


def _qkv_kernel(lat_tiles, xl_ref, xc_ref, sh_ref, sc_ref, g_ref, w_ref, qg_ref, kg_ref, cos_ref,
                sin_ref, q_ref, k_ref, vt_ref):
    h = _rms(_select_rows(lat_tiles, xl_ref, xc_ref)) * g_ref[...]
    h = h * (1.0 + sc_ref[0]) + sh_ref[0]
    y = jnp.dot(h.astype(BF16), w_ref[...], preferred_element_type=F32)
    cos = cos_ref[...]
    sin = sin_ref[...]
    lane = lax.broadcasted_iota(jnp.int32, cos.shape, 1)
    first = (lane % (HEAD_DIM // 2)) < (HEAD_DIM // 4)

    def head(col, gain):
        n = _rms(y[:, col * HEAD_DIM:(col + 1) * HEAD_DIM]) * gain
        sw = jnp.where(first, pltpu.roll(n, HEAD_DIM - HEAD_DIM // 4, 1),
                       pltpu.roll(n, HEAD_DIM // 4, 1))
        return n * cos + sw * sin

    for hh in range(N_HEADS):
        q_ref[:, hh * HEAD_DIM:(hh + 1) * HEAD_DIM] = head(hh, qg_ref[...]).astype(BF16)
    for g in range(N_KV_HEADS):
        k_ref[:, g * HEAD_DIM:(g + 1) * HEAD_DIM] = head(N_HEADS + g, kg_ref[...]).astype(BF16)
    v0 = (N_HEADS + N_KV_HEADS) * HEAD_DIM
    vt_ref[...] = y[:, v0:].T.astype(BF16)


def _qkv(rows, n, mods, layer, gain, w, qg, kg, cos_t, sin_t, lat_tiles, tiles_per_seq, n_batch):
    lat_src, ctx_src, ctx_off = rows
    d = lat_src.shape[1]
    t = ROW_TILE

    def mrow(i):
        return layer * MOD_ROWS + jnp.where(i < lat_tiles, i // tiles_per_seq, n_batch)

    def trow(i):
        return jnp.where(i < lat_tiles, i % tiles_per_seq, tiles_per_seq)

    kvd = N_KV_HEADS * HEAD_DIM
    return pl.pallas_call(
        functools.partial(_qkv_kernel, lat_tiles),
        grid=(n // t,),
        in_specs=_row_specs(t, d, lat_tiles, ctx_off) + [
                  pl.BlockSpec((1, 1, d), lambda i: (mrow(i), 0, 0)),
                  pl.BlockSpec((1, 1, d), lambda i: (mrow(i), 0, 1)),
                  pl.BlockSpec((1, d), lambda i: (0, 0)),
                  pl.BlockSpec((d, QKV_DIM), lambda i: (0, 0)),
                  pl.BlockSpec((1, HEAD_DIM), lambda i: (0, 0)),
                  pl.BlockSpec((1, HEAD_DIM), lambda i: (0, 0)),
                  pl.BlockSpec((t, HEAD_DIM), lambda i: (trow(i), 0)),
                  pl.BlockSpec((t, HEAD_DIM), lambda i: (trow(i), 0))],
        out_specs=[pl.BlockSpec((t, d), lambda i: (i, 0)),
                   pl.BlockSpec((t, kvd), lambda i: (i, 0)),
                   pl.BlockSpec((kvd, t), lambda i: (0, i))],
        out_shape=[jax.ShapeDtypeStruct((n, d), BF16),
                   jax.ShapeDtypeStruct((n, kvd), BF16),
                   jax.ShapeDtypeStruct((kvd, n), BF16)],
        compiler_params=_cparams(("arbitrary",)),
    )(lat_src, ctx_src, mods, mods, gain, w, qg, kg, cos_t, sin_t)


def _stack_heads(q_ref):
    return jnp.concatenate(
        [q_ref[:, r * HEAD_DIM:(r + 1) * HEAD_DIM] for r in range(KV_REP)], axis=0)


def _store_heads(ot, o_ref):
    for r in range(KV_REP):
        blk = ot[:, r * Q_TILE:(r + 1) * Q_TILE]
        o_ref[:, r * HEAD_DIM:(r + 1) * HEAD_DIM] = blk.T.astype(BF16)


def _global_attn_kernel(lat_q_tiles, q_ref, kl_ref, kc_ref, vtl_ref, vtc_ref, o_ref):
    qt = pl.program_id(2)
    q4 = _stack_heads(q_ref)
    sc = _nt_dot(kc_ref[...], q4)

    def finish(ot, den):
        _store_heads(ot * (1.0 / den), o_ref)

    @pl.when(qt < lat_q_tiles)
    def _():
        sl = _nt_dot(kl_ref[...], q4)
        m = jnp.maximum(jnp.max(sl, axis=0, keepdims=True), jnp.max(sc, axis=0, keepdims=True))
        p_l = jnp.exp2(sl - m)
        p_c = jnp.exp2(sc - m)
        den = jnp.sum(p_l, axis=0, keepdims=True) + jnp.sum(p_c, axis=0, keepdims=True)
        finish(jnp.dot(vtl_ref[...], p_l.astype(BF16), preferred_element_type=F32)
               + jnp.dot(vtc_ref[...], p_c.astype(BF16), preferred_element_type=F32), den)

    @pl.when(qt >= lat_q_tiles)
    def _():
        p_c = jnp.exp2(sc - jnp.max(sc, axis=0, keepdims=True))
        finish(jnp.dot(vtc_ref[...], p_c.astype(BF16), preferred_element_type=F32),
               jnp.sum(p_c, axis=0, keepdims=True))


def _global_attention(q, k, vt, n_batch, seq, ctx_len):
    n, d = q.shape
    lat_q = seq // Q_TILE
    ctx_q = ctx_len // Q_TILE
    ctx_blk0 = (n_batch * seq) // ctx_len
    gw = KV_REP * HEAD_DIM

    def qrow(b, g, t):
        return jnp.where(t < lat_q, b * lat_q + t, n_batch * lat_q + b * ctx_q + (t - lat_q))

    return pl.pallas_call(
        functools.partial(_global_attn_kernel, lat_q),
        grid=(n_batch, N_KV_HEADS, lat_q + ctx_q),
        in_specs=[pl.BlockSpec((Q_TILE, gw), lambda b, g, t: (qrow(b, g, t), g)),
                  pl.BlockSpec((seq, HEAD_DIM), lambda b, g, t: (b, g)),
                  pl.BlockSpec((ctx_len, HEAD_DIM), lambda b, g, t: (ctx_blk0 + b, g)),
                  pl.BlockSpec((HEAD_DIM, seq), lambda b, g, t: (g, b)),
                  pl.BlockSpec((HEAD_DIM, ctx_len), lambda b, g, t: (g, ctx_blk0 + b))],
        out_specs=pl.BlockSpec((Q_TILE, gw), lambda b, g, t: (qrow(b, g, t), g)),
        out_shape=jax.ShapeDtypeStruct((n, d), BF16),
        compiler_params=_cparams(("arbitrary", "arbitrary", "arbitrary")),
    )(q, k, k, vt, vt)


def _window_attn_kernel(seq, q_ref, kl_ref, kc_ref, vtl_ref, vtc_ref, sink_ref, o_ref):
    qt = pl.program_id(2)
    span = Q_TILE + 2 * WINDOW
    start = pl.multiple_of(jnp.clip(qt * Q_TILE - WINDOW, 0, seq - span), math.gcd(Q_TILE, WINDOW))
    q4 = _stack_heads(q_ref)
    cols = KV_REP * Q_TILE
    s_w = _nt_dot(kl_ref[pl.ds(start, span), :], q4)
    s_c = _nt_dot(kc_ref[...], q4)
    kpos = start + lax.broadcasted_iota(jnp.int32, (span, 1), 0)
    qpos = qt * Q_TILE + lax.broadcasted_iota(jnp.int32, (1, cols), 1) % Q_TILE
    s_w = jnp.where(jnp.abs(qpos - kpos) <= WINDOW, s_w, NEG_INF)
    sink = sink_ref[0]
    m = jnp.maximum(jnp.maximum(jnp.max(s_w, axis=0, keepdims=True),
                                jnp.max(s_c, axis=0, keepdims=True)), sink)
    p_w = jnp.exp2(s_w - m)
    p_c = jnp.exp2(s_c - m)
    den = (jnp.sum(p_w, axis=0, keepdims=True) + jnp.sum(p_c, axis=0, keepdims=True)
           + jnp.exp2(sink - m))
    ot = (jnp.dot(vtl_ref[:, pl.ds(start, span)], p_w.astype(BF16), preferred_element_type=F32)
          + jnp.dot(vtc_ref[...], p_c.astype(BF16), preferred_element_type=F32))
    _store_heads(ot * (1.0 / den), o_ref)


def _window_attention(q, k, vt, sink_cols, n_batch, seq, ctx_len):
    d = q.shape[1]
    n = n_batch * seq
    lat_q = seq // Q_TILE
    ctx_blk0 = (n_batch * seq) // ctx_len
    gw = KV_REP * HEAD_DIM
    return pl.pallas_call(
        functools.partial(_window_attn_kernel, seq),
        grid=(n_batch, N_KV_HEADS, lat_q),
        in_specs=[pl.BlockSpec((Q_TILE, gw), lambda b, g, t: (b * lat_q + t, g)),
                  pl.BlockSpec((seq, HEAD_DIM), lambda b, g, t: (b, g)),
                  pl.BlockSpec((ctx_len, HEAD_DIM), lambda b, g, t: (ctx_blk0 + b, g)),
                  pl.BlockSpec((HEAD_DIM, seq), lambda b, g, t: (g, b)),
                  pl.BlockSpec((HEAD_DIM, ctx_len), lambda b, g, t: (g, ctx_blk0 + b)),
                  pl.BlockSpec((1, 1, KV_REP * Q_TILE), lambda b, g, t: (g, 0, 0))],
        out_specs=pl.BlockSpec((Q_TILE, gw), lambda b, g, t: (b * lat_q + t, g)),
        out_shape=jax.ShapeDtypeStruct((n, d), BF16),
        compiler_params=_cparams(("arbitrary", "arbitrary", "arbitrary")),
    )(q, k, k, vt, vt, sink_cols)


def _peer_pre_kernel(lat_tiles, o_ref, xl_ref, xc_ref, wo_ref, g1_ref, n2_ref, sh_ref, sc_ref, wq_ref,
                     keys_ref, xn_ref, h2t_ref, s2_ref, e2_ref, thr_ref, e1_ref, st_ref):
    y = jnp.dot(o_ref[...], wo_ref[...], preferred_element_type=F32)
    xn = _select_rows(lat_tiles, xl_ref, xc_ref) + g1_ref[0] * y
    xn_ref[...] = xn
    h2 = _rms(xn) * n2_ref[...]
    h2 = h2 * (1.0 + sc_ref[0]) + sh_ref[0]
    h2t_ref[...] = h2.T.astype(BF16)
    qp = jnp.dot(h2.astype(BF16), wq_ref[...], preferred_element_type=F32)
    half = PEER_QDIM // 2
    for hs in range(2 * PEER_HEADS):
        qs = qp[:, hs * half:(hs + 1) * half].astype(BF16)
        st_ref[hs] = _nt_dot(keys_ref[hs % 2], qs)
    _peer_topk(st_ref, s2_ref, e2_ref, thr_ref, e1_ref)


def _peer_pre(o, rows, mods, layer, w_o, n2, w_q, keys, n_tiles, lat_tiles, tiles_per_seq, n_batch):
    lat_src, ctx_src, ctx_off = rows
    d = o.shape[1]
    t = ROW_TILE
    n = n_tiles * t
    qd = PEER_HEADS * PEER_QDIM

    def mrow(i):
        return layer * MOD_ROWS + jnp.where(i < lat_tiles, i // tiles_per_seq, n_batch)

    cspec = pl.BlockSpec((t // SUB_LANES, PEER_HEADS, N_KEYS, SUB_LANES), lambda i: (i, 0, 0, 0))
    cshape = jax.ShapeDtypeStruct((n // SUB_LANES, PEER_HEADS, N_KEYS, SUB_LANES), F32)
    outs = pl.pallas_call(
        functools.partial(_peer_pre_kernel, lat_tiles),
        grid=(n_tiles,),
        in_specs=[pl.BlockSpec((t, d), lambda i: (i, 0))] + _row_specs(t, d, lat_tiles, ctx_off) + [
                  pl.BlockSpec((d, d), lambda i: (0, 0)),
                  pl.BlockSpec((1, 1, d), lambda i: (mrow(i), 0, 2)),
                  pl.BlockSpec((1, d), lambda i: (0, 0)),
                  pl.BlockSpec((1, 1, d), lambda i: (mrow(i), 0, 3)),
                  pl.BlockSpec((1, 1, d), lambda i: (mrow(i), 0, 4)),
                  pl.BlockSpec((d, qd), lambda i: (0, 0)),
                  pl.BlockSpec((2, N_KEYS, PEER_QDIM // 2), lambda i: (0, 0, 0))],
        out_specs=[pl.BlockSpec((t, d), lambda i: (i, 0)),
                   pl.BlockSpec((d, t), lambda i: (0, i)),
                   cspec, cspec, cspec, cspec],
        out_shape=[jax.ShapeDtypeStruct((n, d), F32),
                   jax.ShapeDtypeStruct((d, n), BF16),
                   cshape, cshape, cshape, cshape],
        scratch_shapes=[pltpu.VMEM((2 * PEER_HEADS, N_KEYS, t), F32)],
        compiler_params=_cparams(("arbitrary",)),
    )(o, lat_src, ctx_src, w_o, mods, n2, mods, mods, w_q, keys)
    return outs[0], outs[1], outs[2:]


def _sorting_network(n):
    def merge(lo, hi, r):
        step = r * 2
        if step < hi - lo:
            yield from merge(lo, hi, step)
            yield from merge(lo + r, hi, step)
            yield from [(i, i + r) for i in range(lo + r, hi - r, step)]
        else:
            yield (lo, lo + r)

    def sort(lo, hi):
        if hi - lo >= 1:
            mid = lo + (hi - lo) // 2
            yield from sort(lo, mid)
            yield from sort(mid + 1, hi)
            yield from merge(lo, hi, 1)

    return list(sort(0, n - 1))


def _exchange(x, i, j):
    x[i], x[j] = jnp.maximum(x[i], x[j]), jnp.minimum(x[i], x[j])


def _top_values(s):
    k = PEER_TOPK
    assert s.shape[0] == k * SUBLANES
    x = [s[i * SUBLANES:(i + 1) * SUBLANES] for i in range(k)]
    for i, j in _sorting_network(k):
        _exchange(x, i, j)
    shift = SUBLANES // 2
    while shift:
        x = [jnp.maximum(x[i], pltpu.roll(x[k - 1 - i], shift, 0)) for i in range(k)]
        d = k // 2
        while d:
            for i in range(k):
                if not i & d:
                    _exchange(x, i, i + d)
            d //= 2
        shift //= 2
    return [xi[0:1] for xi in x]


def _kth_largest(s, k):
    for _ in range(k - 1):
        s = jnp.where(s == jnp.max(s, axis=0, keepdims=True), -jnp.inf, s)
    return jnp.max(s, axis=0, keepdims=True)


def _candidate_sums(v1, v2):
    assert PEER_TOPK == 16
    a1 = jnp.concatenate(v1, axis=0)
    a2 = jnp.concatenate(v2, axis=0)
    row = lax.broadcasted_iota(jnp.int32, (8, a1.shape[1]), 0)
    ninf = -jnp.inf
    return jnp.concatenate([
        v1[0] + a2,
        v1[1] + a2[0:8],
        jnp.where(row < 5, v1[2] + a2[0:8], ninf),
        jnp.where(row < 4, v1[3] + a2[0:8], ninf),
        a1[8:16] + v2[0],
        jnp.where(row >= 4, a1[0:8] + v2[0], ninf),
        jnp.where(row >= 4, a1[0:8] + v2[1], ninf),
        jnp.where(row == 4, a1[0:8] + v2[2], ninf),
    ], axis=0)


def _peer_topk(st_ref, s2_ref, e2_ref, thr_ref, e1_ref):
    def per_head(h, carry):
        s1 = st_ref[2 * h]
        s2 = st_ref[2 * h + 1]
        v1 = _top_values(s1)
        v2 = _top_values(s2)
        cand = _candidate_sums(v1, v2)
        thr = _kth_largest(cand, PEER_TOPK)
        top = v1[0] + v2[0]
        z = jnp.sum(jnp.where(cand >= thr, jnp.exp(cand - top), 0.0), axis=0, keepdims=True)
        guard = thr - jnp.abs(thr) * THRESHOLD_SLACK
        outs = ((s2_ref, s2), (e2_ref, jnp.exp(s2 - v2[0]) * (1.0 / z)),
                (thr_ref, guard - s1), (e1_ref, jnp.exp(s1 - v1[0])))
        for ref, val in outs:
            for lb in range(val.shape[1] // SUB_LANES):
                ref[lb, h] = val[:, lb * SUB_LANES:(lb + 1) * SUB_LANES]
        return carry

    lax.fori_loop(0, PEER_HEADS, per_head, 0)


def _gelu_tanh(x):
    k = -2.0 * math.log2(math.e) * math.sqrt(2.0 / math.pi)
    return x / (1.0 + jnp.exp2(x * (k + (k * 0.044715) * (x * x))))


def _peer_dense_kernel(h2t_ref, s2_ref, e2_ref, thr_ref, e1_ref, u_ref, v_ref, xn_ref, g2_ref,
                       out_ref, acc_ref, act_ref, slab_ref):
    j = pl.program_id(1)
    lanes = acc_ref.shape[1]
    lane_blocks = lanes // SUB_LANES
    row_blocks = N_KEYS // GATE_ROWS

    @pl.when(j == 0)
    def _():
        acc_ref[...] = jnp.zeros_like(acc_ref)

    act_ref[...] = jnp.dot(u_ref[0], h2t_ref[...], preferred_element_type=F32)

    def gate_block(blk, carry):
        grp = blk // (row_blocks * lane_blocks)
        bc = (blk // lane_blocks) % row_blocks
        lc = blk % lane_blocks
        rows = pl.ds(pl.multiple_of(bc * GATE_ROWS, GATE_ROWS), GATE_ROWS)
        cols = pl.ds(pl.multiple_of(lc * SUB_LANES, SUB_LANES), SUB_LANES)
        a0 = j * PEER_CHUNK + grp * GATE_KEYS
        w = [None] * GATE_KEYS
        for h in range(PEER_HEADS):
            s2 = s2_ref[lc, h, rows, :]
            e2 = e2_ref[lc, h, rows, :]
            for al in range(GATE_KEYS):
                thr = thr_ref[lc, h, pl.ds(a0 + al, GATE_ROWS, stride=0), :]
                e1 = e1_ref[lc, h, pl.ds(a0 + al, GATE_ROWS, stride=0), :]
                term = jnp.where(s2 >= thr, e2, 0.0) * e1
                w[al] = term if h == 0 else w[al] + term
        for al in range(GATE_KEYS):
            r0 = (grp * GATE_KEYS + al) * N_KEYS + bc * GATE_ROWS
            arows = pl.ds(pl.multiple_of(r0, GATE_ROWS), GATE_ROWS)
            slab_ref[arows, cols] = (w[al] * _gelu_tanh(act_ref[arows, cols])).astype(BF16)
        return carry

    lax.fori_loop(0, (PEER_CHUNK // GATE_KEYS) * row_blocks * lane_blocks, gate_block, 0)
    acc_ref[...] += lax.dot_general(v_ref[0], slab_ref[...], (((0,), (0,)), ((), ())),
                                    preferred_element_type=F32)

    @pl.when(j == pl.num_programs(1) - 1)
    def _():
        out_ref[...] = xn_ref[...] + g2_ref[0] * acc_ref[...].T


def _peer_dense(h2t, coefs, u, v, xn, mods, layer, n_tiles, lat_tiles, tiles_per_seq, n_batch):
    d = xn.shape[1]
    t = PEER_TILE
    ce = PEER_CHUNK * N_KEYS
    n_exp = u.shape[1]

    def mrow(i):
        return layer * MOD_ROWS + jnp.where(i < lat_tiles, i // tiles_per_seq, n_batch)

    cspec = pl.BlockSpec((t // SUB_LANES, PEER_HEADS, N_KEYS, SUB_LANES), lambda i, j: (i, 0, 0, 0))
    return pl.pallas_call(
        _peer_dense_kernel,
        grid=(n_tiles, n_exp // ce),
        in_specs=[pl.BlockSpec((d, t), lambda i, j: (0, i)),
                  cspec, cspec, cspec, cspec,
                  pl.BlockSpec((1, ce, d), lambda i, j: (layer, j, 0)),
                  pl.BlockSpec((1, ce, d), lambda i, j: (layer, j, 0)),
                  pl.BlockSpec((t, d), lambda i, j: (i, 0)),
                  pl.BlockSpec((1, 1, d), lambda i, j: (mrow(i), 0, 5))],
        out_specs=pl.BlockSpec((t, d), lambda i, j: (i, 0)),
        out_shape=jax.ShapeDtypeStruct((n_tiles * t, d), F32),
        scratch_shapes=[pltpu.VMEM((d, t), F32),
                        pltpu.VMEM((ce, t), F32),
                        pltpu.VMEM((ce, t), BF16)],
        compiler_params=_cparams(("arbitrary", "arbitrary")),
    )(h2t, *coefs, u, v, xn, mods)


def _rope_tables(seq, ctx_len):
    pos = jnp.arange(seq)
    row = (pos // GRID_W).astype(F32)
    col = (pos % GRID_W).astype(F32)
    half = HEAD_DIM // 2
    inv = ROPE_THETA ** (-jnp.arange(0, half, 2, dtype=F32) / half)
    ar = row[:, None] * inv
    ac = col[:, None] * inv
    cos = jnp.concatenate([jnp.cos(ar), jnp.cos(ar), jnp.cos(ac), jnp.cos(ac)], axis=-1)
    sin = jnp.concatenate([-jnp.sin(ar), jnp.sin(ar), -jnp.sin(ac), jnp.sin(ac)], axis=-1)
    cos = jnp.concatenate([cos, jnp.ones((ctx_len, HEAD_DIM), F32)], axis=0)
    sin = jnp.concatenate([sin, jnp.zeros((ctx_len, HEAD_DIM), F32)], axis=0)
    return cos, sin


def kernel(x, c, ctx, c_ctx, ada_w, ada_b, norm1_gain, norm2_gain, w_qkv, q_norm_gain, k_norm_gain,
           w_o, attn_sinks, peer_w_q, peer_sub_keys, peer_u, peer_v):
    n_batch, seq, d = x.shape
    ctx_len = ctx.shape[1]
    depth = ada_w.shape[0]
    n_lat = n_batch * seq
    n_ctx = n_batch * ctx_len
    assert depth == 2 and d == D_MODEL and n_batch < MOD_ROWS
    assert seq % PEER_TILE == 0 and n_ctx % PEER_TILE == 0 and ctx_len == ROW_TILE
    assert seq >= Q_TILE + 2 * WINDOW and n_lat % ctx_len == 0
    assert seq % Q_TILE == 0 and ctx_len % Q_TILE == 0 and (N_MOD * d) % MOD_COLS == 0

    cc = jnp.concatenate([c, c_ctx[None, :], jnp.zeros((MOD_ROWS - n_batch - 1, d), F32)], axis=0)
    mods = _modulation(cc, ada_w, ada_b).reshape(depth * MOD_ROWS, 1, N_MOD * d)
    cos_t, sin_t = _rope_tables(seq, ctx_len)
    u_tab = peer_u.astype(BF16)
    v_tab = peer_v.astype(BF16)
    rows = (x.reshape(n_lat, d), ctx.reshape(n_ctx, d), 0)
    scale = HEAD_DIM ** -0.5 * math.log2(math.e)

    for layer in range(depth):
        last = layer == depth - 1
        q, k, vt = _qkv(rows, n_lat + n_ctx, mods, layer, norm1_gain[layer][None, :],
                        w_qkv[layer].astype(BF16),
                        (q_norm_gain[layer] * scale)[None, :], k_norm_gain[layer][None, :],
                        cos_t, sin_t, n_lat // ROW_TILE, seq // ROW_TILE, n_batch)
        if layer % 2 == 0:
            o = _global_attention(q, k, vt, n_batch, seq, ctx_len)
        else:
            sink = (attn_sinks[layer // 2] * math.log2(math.e)).reshape(N_KV_HEADS, 1, KV_REP, 1)
            sink_cols = jnp.broadcast_to(sink, (N_KV_HEADS, 1, KV_REP, Q_TILE))
            o = _window_attention(q, k, vt, sink_cols.reshape(N_KV_HEADS, 1, KV_REP * Q_TILE),
                                  n_batch, seq, ctx_len)
        n_rows = n_lat if last else n_lat + n_ctx
        xn, h2t, coefs = _peer_pre(o, rows, mods, layer, w_o[layer].astype(BF16),
                                norm2_gain[layer][None, :], peer_w_q[layer].astype(BF16),
                                peer_sub_keys[layer].astype(BF16), n_rows // ROW_TILE,
                                n_lat // ROW_TILE, seq // ROW_TILE, n_batch)
        xs = _peer_dense(h2t, coefs, u_tab, v_tab,
                         xn, mods, layer, n_rows // PEER_TILE, n_lat // PEER_TILE,
                         seq // PEER_TILE, n_batch)
        rows = (xs, xs, n_lat // ROW_TILE)
    return xs[:n_lat].reshape(n_batch, seq, d)
```

```python
import functools
import math

import jax
import jax.numpy as jnp
from jax import lax
from jax.experimental import pallas as pl
from jax.experimental.pallas import tpu as pltpu

F32 = jnp.float32
BF16 = jnp.bfloat16

D_MODEL = 1024
N_HEADS = 8
N_KV_HEADS = 2
HEAD_DIM = 128
KV_REP = N_HEADS // N_KV_HEADS
QKV_DIM = (N_HEADS + 2 * N_KV_HEADS) * HEAD_DIM
GRID_W = 64
WINDOW = 128
ROPE_THETA = 10000.0
RMS_EPS = 1e-6
NEG_INF = -1e30
PEER_HEADS = 8
PEER_TOPK = 16
N_KEYS = 128
PEER_QDIM = 256
N_MOD = 6
MOD_ROWS = 8

ROW_TILE = 256
Q_TILE = 256
PEER_TILE = 512
PEER_CHUNK = 16
GATE_KEYS = 8
SUB_LANES = 128
GATE_ROWS = 32
SUBLANES = 8
MOD_COLS = 1536
THRESHOLD_SLACK = 2.0 ** -22
VMEM_LIMIT = 56 * 1024 * 1024


def _cparams(sem):
    return pltpu.CompilerParams(dimension_semantics=sem, vmem_limit_bytes=VMEM_LIMIT)


def _nt_dot(a, b):
    return lax.dot_general(a, b, (((1,), (1,)), ((), ())), preferred_element_type=F32)


def _mod_kernel(c_ref, w_ref, b_ref, o_ref):
    c = c_ref[...]
    a = c * (1.0 / (1.0 + jnp.exp(-c)))
    o_ref[0] = jnp.dot(a, w_ref[0], preferred_element_type=F32,
                       precision=lax.Precision.HIGHEST) + b_ref[0]


def _modulation(cc, ada_w, ada_b):
    depth, d, n = ada_w.shape
    tn = MOD_COLS
    return pl.pallas_call(
        _mod_kernel,
        grid=(depth, n // tn),
        in_specs=[pl.BlockSpec((MOD_ROWS, d), lambda l, j: (0, 0)),
                  pl.BlockSpec((1, d, tn), lambda l, j: (l, 0, j)),
                  pl.BlockSpec((1, 1, tn), lambda l, j: (l, 0, j))],
        out_specs=pl.BlockSpec((1, MOD_ROWS, tn), lambda l, j: (l, 0, j)),
        out_shape=jax.ShapeDtypeStruct((depth, MOD_ROWS, n), F32),
        compiler_params=_cparams(("arbitrary", "arbitrary")),
    )(cc, ada_w, ada_b.reshape(depth, 1, n))


def _rms(x):
    return x * lax.rsqrt(jnp.mean(x * x, axis=-1, keepdims=True) + RMS_EPS)


def _select_rows(lat_tiles, lat_ref, ctx_ref):
    return jnp.where(pl.program_id(0) < lat_tiles, lat_ref[...], ctx_ref[...])


def _row_specs(t, d, lat_tiles, ctx_off):
    return [pl.BlockSpec((t, d), lambda i: (jnp.minimum(i, lat_tiles - 1), 0)),
            pl.BlockSpec((t, d), lambda i: (ctx_off + jnp.maximum(i - lat_tiles, 0), 0))]


def _qkv_kernel(lat_tiles, xl_ref, xc_ref, sh_ref, sc_ref, g_ref, w_ref, qg_ref, kg_ref, cos_ref,
                sin_ref, q_ref, k_ref, vt_ref):
    h = _rms(_select_rows(lat_tiles, xl_ref, xc_ref)) * g_ref[...]
    h = h * (1.0 + sc_ref[0]) + sh_ref[0]
    y = jnp.dot(h.astype(BF16), w_ref[...], preferred_element_type=F32)
    cos = cos_ref[...]
    sin = sin_ref[...]
    lane = lax.broadcasted_iota(jnp.int32, cos.shape, 1)
    first = (lane % (HEAD_DIM // 2)) < (HEAD_DIM // 4)

    def head(col, gain):
        n = _rms(y[:, col * HEAD_DIM:(col + 1) * HEAD_DIM]) * gain
        sw = jnp.where(first, pltpu.roll(n, HEAD_DIM - HEAD_DIM // 4, 1),
                       pltpu.roll(n, HEAD_DIM // 4, 1))
        return n * cos + sw * sin

    for hh in range(N_HEADS):
        q_ref[:, hh * HEAD_DIM:(hh + 1) * HEAD_DIM] = head(hh, qg_ref[...]).astype(BF16)
    for g in range(N_KV_HEADS):
        k_ref[:, g * HEAD_DIM:(g + 1) * HEAD_DIM] = head(N_HEADS + g, kg_ref[...]).astype(BF16)
    v0 = (N_HEADS + N_KV_HEADS) * HEAD_DIM
    vt_ref[...] = y[:, v0:].T.astype(BF16)


def _qkv(rows, n, mods, layer, gain, w, qg, kg, cos_t, sin_t, lat_tiles, tiles_per_seq, n_batch):
    lat_src, ctx_src, ctx_off = rows
    d = lat_src.shape[1]
    t = ROW_TILE

    def mrow(i):
        return layer * MOD_ROWS + jnp.where(i < lat_tiles, i // tiles_per_seq, n_batch)

    def trow(i):
        return jnp.where(i < lat_tiles, i % tiles_per_seq, tiles_per_seq)

    kvd = N_KV_HEADS * HEAD_DIM
    return pl.pallas_call(
        functools.partial(_qkv_kernel, lat_tiles),
        grid=(n // t,),
        in_specs=_row_specs(t, d, lat_tiles, ctx_off) + [
                  pl.BlockSpec((1, 1, d), lambda i: (mrow(i), 0, 0)),
                  pl.BlockSpec((1, 1, d), lambda i: (mrow(i), 0, 1)),
                  pl.BlockSpec((1, d), lambda i: (0, 0)),
                  pl.BlockSpec((d, QKV_DIM), lambda i: (0, 0)),
                  pl.BlockSpec((1, HEAD_DIM), lambda i: (0, 0)),
                  pl.BlockSpec((1, HEAD_DIM), lambda i: (0, 0)),
                  pl.BlockSpec((t, HEAD_DIM), lambda i: (trow(i), 0)),
                  pl.BlockSpec((t, HEAD_DIM), lambda i: (trow(i), 0))],
        out_specs=[pl.BlockSpec((t, d), lambda i: (i, 0)),
                   pl.BlockSpec((t, kvd), lambda i: (i, 0)),
                   pl.BlockSpec((kvd, t), lambda i: (0, i))],
        out_shape=[jax.ShapeDtypeStruct((n, d), BF16),
                   jax.ShapeDtypeStruct((n, kvd), BF16),
                   jax.ShapeDtypeStruct((kvd, n), BF16)],
        compiler_params=_cparams(("arbitrary",)),
    )(lat_src, ctx_src, mods, mods, gain, w, qg, kg, cos_t, sin_t)


def _stack_heads(q_ref):
    return jnp.concatenate(
        [q_ref[:, r * HEAD_DIM:(r + 1) * HEAD_DIM] for r in range(KV_REP)], axis=0)


def _store_heads(ot, o_ref):
    for r in range(KV_REP):
        blk = ot[:, r * Q_TILE:(r + 1) * Q_TILE]
        o_ref[:, r * HEAD_DIM:(r + 1) * HEAD_DIM] = blk.T.astype(BF16)


def _global_attn_kernel(lat_q_tiles, q_ref, kl_ref, kc_ref, vtl_ref, vtc_ref, o_ref):
    qt = pl.program_id(2)
    q4 = _stack_heads(q_ref)
    sc = _nt_dot(kc_ref[...], q4)

    def finish(ot, den):
        _store_heads(ot * (1.0 / den), o_ref)

    @pl.when(qt < lat_q_tiles)
    def _():
        sl = _nt_dot(kl_ref[...], q4)
        m = jnp.maximum(jnp.max(sl, axis=0, keepdims=True), jnp.max(sc, axis=0, keepdims=True))
        p_l = jnp.exp2(sl - m)
        p_c = jnp.exp2(sc - m)
        den = jnp.sum(p_l, axis=0, keepdims=True) + jnp.sum(p_c, axis=0, keepdims=True)
        finish(jnp.dot(vtl_ref[...], p_l.astype(BF16), preferred_element_type=F32)
               + jnp.dot(vtc_ref[...], p_c.astype(BF16), preferred_element_type=F32), den)

    @pl.when(qt >= lat_q_tiles)
    def _():
        p_c = jnp.exp2(sc - jnp.max(sc, axis=0, keepdims=True))
        finish(jnp.dot(vtc_ref[...], p_c.astype(BF16), preferred_element_type=F32),
               jnp.sum(p_c, axis=0, keepdims=True))


def _global_attention(q, k, vt, n_batch, seq, ctx_len):
    n, d = q.shape
    lat_q = seq // Q_TILE
    ctx_q = ctx_len // Q_TILE
    ctx_blk0 = (n_batch * seq) // ctx_len
    gw = KV_REP * HEAD_DIM

    def qrow(b, g, t):
        return jnp.where(t < lat_q, b * lat_q + t, n_batch * lat_q + b * ctx_q + (t - lat_q))

    return pl.pallas_call(
        functools.partial(_global_attn_kernel, lat_q),
        grid=(n_batch, N_KV_HEADS, lat_q + ctx_q),
        in_specs=[pl.BlockSpec((Q_TILE, gw), lambda b, g, t: (qrow(b, g, t), g)),
                  pl.BlockSpec((seq, HEAD_DIM), lambda b, g, t: (b, g)),
                  pl.BlockSpec((ctx_len, HEAD_DIM), lambda b, g, t: (ctx_blk0 + b, g)),
                  pl.BlockSpec((HEAD_DIM, seq), lambda b, g, t: (g, b)),
                  pl.BlockSpec((HEAD_DIM, ctx_len), lambda b, g, t: (g, ctx_blk0 + b))],
        out_specs=pl.BlockSpec((Q_TILE, gw), lambda b, g, t: (qrow(b, g, t), g)),
        out_shape=jax.ShapeDtypeStruct((n, d), BF16),
        compiler_params=_cparams(("arbitrary", "arbitrary", "arbitrary")),
    )(q, k, k, vt, vt)


def _window_attn_kernel(seq, q_ref, kl_ref, kc_ref, vtl_ref, vtc_ref, sink_ref, o_ref):
    qt = pl.program_id(2)
    span = Q_TILE + 2 * WINDOW
    start = pl.multiple_of(jnp.clip(qt * Q_TILE - WINDOW, 0, seq - span), math.gcd(Q_TILE, WINDOW))
    q4 = _stack_heads(q_ref)
    cols = KV_REP * Q_TILE
    s_w = _nt_dot(kl_ref[pl.ds(start, span), :], q4)
    s_c = _nt_dot(kc_ref[...], q4)
    kpos = start + lax.broadcasted_iota(jnp.int32, (span, 1), 0)
    qpos = qt * Q_TILE + lax.broadcasted_iota(jnp.int32, (1, cols), 1) % Q_TILE
    s_w = jnp.where(jnp.abs(qpos - kpos) <= WINDOW, s_w, NEG_INF)
    sink = sink_ref[0]
    m = jnp.maximum(jnp.maximum(jnp.max(s_w, axis=0, keepdims=True),
                                jnp.max(s_c, axis=0, keepdims=True)), sink)
    p_w = jnp.exp2(s_w - m)
    p_c = jnp.exp2(s_c - m)
    den = (jnp.sum(p_w, axis=0, keepdims=True) + jnp.sum(p_c, axis=0, keepdims=True)
           + jnp.exp2(sink - m))
    ot = (jnp.dot(vtl_ref[:, pl.ds(start, span)], p_w.astype(BF16), preferred_element_type=F32)
          + jnp.dot(vtc_ref[...], p_c.astype(BF16), preferred_element_type=F32))
    _store_heads(ot * (1.0 / den), o_ref)


def _window_attention(q, k, vt, sink_cols, n_batch, seq, ctx_len):
    d = q.shape[1]
    n = n_batch * seq
    lat_q = seq // Q_TILE
    ctx_blk0 = (n_batch * seq) // ctx_len
    gw = KV_REP * HEAD_DIM
    return pl.pallas_call(
        functools.partial(_window_attn_kernel, seq),
        grid=(n_batch, N_KV_HEADS, lat_q),
        in_specs=[pl.BlockSpec((Q_TILE, gw), lambda b, g, t: (b * lat_q + t, g)),
                  pl.BlockSpec((seq, HEAD_DIM), lambda b, g, t: (b, g)),
                  pl.BlockSpec((ctx_len, HEAD_DIM), lambda b, g, t: (ctx_blk0 + b, g)),
                  pl.BlockSpec((HEAD_DIM, seq), lambda b, g, t: (g, b)),
                  pl.BlockSpec((HEAD_DIM, ctx_len), lambda b, g, t: (g, ctx_blk0 + b)),
                  pl.BlockSpec((1, 1, KV_REP * Q_TILE), lambda b, g, t: (g, 0, 0))],
        out_specs=pl.BlockSpec((Q_TILE, gw), lambda b, g, t: (b * lat_q + t, g)),
        out_shape=jax.ShapeDtypeStruct((n, d), BF16),
        compiler_params=_cparams(("arbitrary", "arbitrary", "arbitrary")),
    )(q, k, k, vt, vt, sink_cols)


def _peer_pre_kernel(lat_tiles, o_ref, xl_ref, xc_ref, wo_ref, g1_ref, n2_ref, sh_ref, sc_ref, wq_ref,
                     keys_ref, xn_ref, h2t_ref, s2_ref, e2_ref, thr_ref, e1_ref, st_ref):
    y = jnp.dot(o_ref[...], wo_ref[...], preferred_element_type=F32)
    xn = _select_rows(lat_tiles, xl_ref, xc_ref) + g1_ref[0] * y
    xn_ref[...] = xn
    h2 = _rms(xn) * n2_ref[...]
    h2 = h2 * (1.0 + sc_ref[0]) + sh_ref[0]
    h2t_ref[...] = h2.T.astype(BF16)
    qp = jnp.dot(h2.astype(BF16), wq_ref[...], preferred_element_type=F32)
    half = PEER_QDIM // 2
    for hs in range(2 * PEER_HEADS):
        qs = qp[:, hs * half:(hs + 1) * half].astype(BF16)
        st_ref[hs] = _nt_dot(keys_ref[hs % 2], qs)
    _peer_topk(st_ref, s2_ref, e2_ref, thr_ref, e1_ref)


def _peer_pre(o, rows, mods, layer, w_o, n2, w_q, keys, n_tiles, lat_tiles, tiles_per_seq, n_batch):
    lat_src, ctx_src, ctx_off = rows
    d = o.shape[1]
    t = ROW_TILE
    n = n_tiles * t
    qd = PEER_HEADS * PEER_QDIM

    def mrow(i):
        return layer * MOD_ROWS + jnp.where(i < lat_tiles, i // tiles_per_seq, n_batch)

    cspec = pl.BlockSpec((t // SUB_LANES, PEER_HEADS, N_KEYS, SUB_LANES), lambda i: (i, 0, 0, 0))
    cshape = jax.ShapeDtypeStruct((n // SUB_LANES, PEER_HEADS, N_KEYS, SUB_LANES), F32)
    outs = pl.pallas_call(
        functools.partial(_peer_pre_kernel, lat_tiles),
        grid=(n_tiles,),
        in_specs=[pl.BlockSpec((t, d), lambda i: (i, 0))] + _row_specs(t, d, lat_tiles, ctx_off) + [
                  pl.BlockSpec((d, d), lambda i: (0, 0)),
                  pl.BlockSpec((1, 1, d), lambda i: (mrow(i), 0, 2)),
                  pl.BlockSpec((1, d), lambda i: (0, 0)),
                  pl.BlockSpec((1, 1, d), lambda i: (mrow(i), 0, 3)),
                  pl.BlockSpec((1, 1, d), lambda i: (mrow(i), 0, 4)),
                  pl.BlockSpec((d, qd), lambda i: (0, 0)),
                  pl.BlockSpec((2, N_KEYS, PEER_QDIM // 2), lambda i: (0, 0, 0))],
        out_specs=[pl.BlockSpec((t, d), lambda i: (i, 0)),
                   pl.BlockSpec((d, t), lambda i: (0, i)),
                   cspec, cspec, cspec, cspec],
        out_shape=[jax.ShapeDtypeStruct((n, d), F32),
                   jax.ShapeDtypeStruct((d, n), BF16),
                   cshape, cshape, cshape, cshape],
        scratch_shapes=[pltpu.VMEM((2 * PEER_HEADS, N_KEYS, t), F32)],
        compiler_params=_cparams(("arbitrary",)),
    )(o, lat_src, ctx_src, w_o, mods, n2, mods, mods, w_q, keys)
    return outs[0], outs[1], outs[2:]


def _sorting_network(n):
    def merge(lo, hi, r):
        step = r * 2
        if step < hi - lo:
            yield from merge(lo, hi, step)
            yield from merge(lo + r, hi, step)
            yield from [(i, i + r) for i in range(lo + r, hi - r, step)]
        else:
            yield (lo, lo + r)

    def sort(lo, hi):
        if hi - lo >= 1:
            mid = lo + (hi - lo) // 2
            yield from sort(lo, mid)
            yield from sort(mid + 1, hi)
            yield from merge(lo, hi, 1)

    return list(sort(0, n - 1))


def _exchange(x, i, j):
    x[i], x[j] = jnp.maximum(x[i], x[j]), jnp.minimum(x[i], x[j])


def _top_values(s):
    k = PEER_TOPK
    assert s.shape[0] == k * SUBLANES
    x = [s[i * SUBLANES:(i + 1) * SUBLANES] for i in range(k)]
    for i, j in _sorting_network(k):
        _exchange(x, i, j)
    shift = SUBLANES // 2
    while shift:
        x = [jnp.maximum(x[i], pltpu.roll(x[k - 1 - i], shift, 0)) for i in range(k)]
        d = k // 2
        while d:
            for i in range(k):
                if not i & d:
                    _exchange(x, i, i + d)
            d //= 2
        shift //= 2
    return [xi[0:1] for xi in x]


def _kth_largest(s, k):
    for _ in range(k - 1):
        s = jnp.where(s == jnp.max(s, axis=0, keepdims=True), -jnp.inf, s)
    return jnp.max(s, axis=0, keepdims=True)


def _candidate_sums(v1, v2):
    assert PEER_TOPK == 16
    a1 = jnp.concatenate(v1, axis=0)
    a2 = jnp.concatenate(v2, axis=0)
    row = lax.broadcasted_iota(jnp.int32, (8, a1.shape[1]), 0)
    ninf = -jnp.inf
    return jnp.concatenate([
        v1[0] + a2,
        v1[1] + a2[0:8],
        jnp.where(row < 5, v1[2] + a2[0:8], ninf),
        jnp.where(row < 4, v1[3] + a2[0:8], ninf),
        a1[8:16] + v2[0],
        jnp.where(row >= 4, a1[0:8] + v2[0], ninf),
        jnp.where(row >= 4, a1[0:8] + v2[1], ninf),
        jnp.where(row == 4, a1[0:8] + v2[2], ninf),
    ], axis=0)


def _peer_topk(st_ref, s2_ref, e2_ref, thr_ref, e1_ref):
    def per_head(h, carry):
        s1 = st_ref[2 * h]
        s2 = st_ref[2 * h + 1]
        v1 = _top_values(s1)
        v2 = _top_values(s2)
        cand = _candidate_sums(v1, v2)
        thr = _kth_largest(cand, PEER_TOPK)
        top = v1[0] + v2[0]
        z = jnp.sum(jnp.where(cand >= thr, jnp.exp(cand - top), 0.0), axis=0, keepdims=True)
        guard = thr - jnp.abs(thr) * THRESHOLD_SLACK
        outs = ((s2_ref, s2), (e2_ref, jnp.exp(s2 - v2[0]) * (1.0 / z)),
                (thr_ref, guard - s1), (e1_ref, jnp.exp(s1 - v1[0])))
        for ref, val in outs:
            for lb in range(val.shape[1] // SUB_LANES):
                ref[lb, h] = val[:, lb * SUB_LANES:(lb + 1) * SUB_LANES]
        return carry

    lax.fori_loop(0, PEER_HEADS, per_head, 0)


def _gelu_tanh(x):
    k = -2.0 * math.log2(math.e) * math.sqrt(2.0 / math.pi)
    return x / (1.0 + jnp.exp2(x * (k + (k * 0.044715) * (x * x))))


def _peer_dense_kernel(h2t_ref, s2_ref, e2_ref, thr_ref, e1_ref, u_ref, v_ref, xn_ref, g2_ref,
                       out_ref, acc_ref, act_ref, slab_ref):
    j = pl.program_id(1)
    lanes = acc_ref.shape[1]
    lane_blocks = lanes // SUB_LANES
    row_blocks = N_KEYS // GATE_ROWS

    @pl.when(j == 0)
    def _():
        acc_ref[...] = jnp.zeros_like(acc_ref)

    act_ref[...] = jnp.dot(u_ref[0], h2t_ref[...], preferred_element_type=F32)

    def gate_block(blk, carry):
        grp = blk // (row_blocks * lane_blocks)
        bc = (blk // lane_blocks) % row_blocks
        lc = blk % lane_blocks
        rows = pl.ds(pl.multiple_of(bc * GATE_ROWS, GATE_ROWS), GATE_ROWS)
        cols = pl.ds(pl.multiple_of(lc * SUB_LANES, SUB_LANES), SUB_LANES)
        a0 = j * PEER_CHUNK + grp * GATE_KEYS
        w = [None] * GATE_KEYS
        for h in range(PEER_HEADS):
            s2 = s2_ref[lc, h, rows, :]
            e2 = e2_ref[lc, h, rows, :]
            for al in range(GATE_KEYS):
                thr = thr_ref[lc, h, pl.ds(a0 + al, GATE_ROWS, stride=0), :]
                e1 = e1_ref[lc, h, pl.ds(a0 + al, GATE_ROWS, stride=0), :]
                term = jnp.where(s2 >= thr, e2, 0.0) * e1
                w[al] = term if h == 0 else w[al] + term
        for al in range(GATE_KEYS):
            r0 = (grp * GATE_KEYS + al) * N_KEYS + bc * GATE_ROWS
            arows = pl.ds(pl.multiple_of(r0, GATE_ROWS), GATE_ROWS)
            slab_ref[arows, cols] = (w[al] * _gelu_tanh(act_ref[arows, cols])).astype(BF16)
        return carry

    lax.fori_loop(0, (PEER_CHUNK // GATE_KEYS) * row_blocks * lane_blocks, gate_block, 0, unroll=2)
    acc_ref[...] += lax.dot_general(v_ref[0], slab_ref[...], (((0,), (0,)), ((), ())),
                                    preferred_element_type=F32)

    @pl.when(j == pl.num_programs(1) - 1)
    def _():
        out_ref[...] = xn_ref[...] + g2_ref[0] * acc_ref[...].T


def _peer_dense(h2t, coefs, u, v, xn, mods, layer, n_tiles, lat_tiles, tiles_per_seq, n_batch):
    d = xn.shape[1]
    t = PEER_TILE
    ce = PEER_CHUNK * N_KEYS
    n_exp = u.shape[1]

    def mrow(i):
        return layer * MOD_ROWS + jnp.where(i < lat_tiles, i // tiles_per_seq, n_batch)

    cspec = pl.BlockSpec((t // SUB_LANES, PEER_HEADS, N_KEYS, SUB_LANES), lambda i, j: (i, 0, 0, 0))
    return pl.pallas_call(
        _peer_dense_kernel,
        grid=(n_tiles, n_exp // ce),
        in_specs=[pl.BlockSpec((d, t), lambda i, j: (0, i)),
                  cspec, cspec, cspec, cspec,
                  pl.BlockSpec((1, ce, d), lambda i, j: (layer, j, 0)),
                  pl.BlockSpec((1, ce, d), lambda i, j: (layer, j, 0)),
                  pl.BlockSpec((t, d), lambda i, j: (i, 0)),
                  pl.BlockSpec((1, 1, d), lambda i, j: (mrow(i), 0, 5))],
        out_specs=pl.BlockSpec((t, d), lambda i, j: (i, 0)),
        out_shape=jax.ShapeDtypeStruct((n_tiles * t, d), F32),
        scratch_shapes=[pltpu.VMEM((d, t), F32),
                        pltpu.VMEM((ce, t), F32),
                        pltpu.VMEM((ce, t), BF16)],
        compiler_params=_cparams(("arbitrary", "arbitrary")),
    )(h2t, *coefs, u, v, xn, mods)


def _rope_tables(seq, ctx_len):
    pos = jnp.arange(seq)
    row = (pos // GRID_W).astype(F32)
    col = (pos % GRID_W).astype(F32)
    half = HEAD_DIM // 2
    inv = ROPE_THETA ** (-jnp.arange(0, half, 2, dtype=F32) / half)
    ar = row[:, None] * inv
    ac = col[:, None] * inv
    cos = jnp.concatenate([jnp.cos(ar), jnp.cos(ar), jnp.cos(ac), jnp.cos(ac)], axis=-1)
    sin = jnp.concatenate([-jnp.sin(ar), jnp.sin(ar), -jnp.sin(ac), jnp.sin(ac)], axis=-1)
    cos = jnp.concatenate([cos, jnp.ones((ctx_len, HEAD_DIM), F32)], axis=0)
    sin = jnp.concatenate([sin, jnp.zeros((ctx_len, HEAD_DIM), F32)], axis=0)
    return cos, sin


def kernel(x, c, ctx, c_ctx, ada_w, ada_b, norm1_gain, norm2_gain, w_qkv, q_norm_gain, k_norm_gain,
           w_o, attn_sinks, peer_w_q, peer_sub_keys, peer_u, peer_v):
    n_batch, seq, d = x.shape
    ctx_len = ctx.shape[1]
    depth = ada_w.shape[0]
    n_lat = n_batch * seq
    n_ctx = n_batch * ctx_len
    assert depth == 2 and d == D_MODEL and n_batch < MOD_ROWS
    assert seq % PEER_TILE == 0 and n_ctx % PEER_TILE == 0 and ctx_len == ROW_TILE
    assert seq >= Q_TILE + 2 * WINDOW and n_lat % ctx_len == 0
    assert seq % Q_TILE == 0 and ctx_len % Q_TILE == 0 and (N_MOD * d) % MOD_COLS == 0

    cc = jnp.concatenate([c, c_ctx[None, :], jnp.zeros((MOD_ROWS - n_batch - 1, d), F32)], axis=0)
    mods = _modulation(cc, ada_w, ada_b).reshape(depth * MOD_ROWS, 1, N_MOD * d)
    cos_t, sin_t = _rope_tables(seq, ctx_len)
    u_tab = peer_u.astype(BF16)
    v_tab = peer_v.astype(BF16)
    rows = (x.reshape(n_lat, d), ctx.reshape(n_ctx, d), 0)
    scale = HEAD_DIM ** -0.5 * math.log2(math.e)

    for layer in range(depth):
        last = layer == depth - 1
        q, k, vt = _qkv(rows, n_lat + n_ctx, mods, layer, norm1_gain[layer][None, :],
                        w_qkv[layer].astype(BF16),
                        (q_norm_gain[layer] * scale)[None, :], k_norm_gain[layer][None, :],
                        cos_t, sin_t, n_lat // ROW_TILE, seq // ROW_TILE, n_batch)
        if layer % 2 == 0:
            o = _global_attention(q, k, vt, n_batch, seq, ctx_len)
        else:
            sink = (attn_sinks[layer // 2] * math.log2(math.e)).reshape(N_KV_HEADS, 1, KV_REP, 1)
            sink_cols = jnp.broadcast_to(sink, (N_KV_HEADS, 1, KV_REP, Q_TILE))
            o = _window_attention(q, k, vt, sink_cols.reshape(N_KV_HEADS, 1, KV_REP * Q_TILE),
                                  n_batch, seq, ctx_len)
        n_rows = n_lat if last else n_lat + n_ctx
        xn, h2t, coefs = _peer_pre(o, rows, mods, layer, w_o[layer].astype(BF16),
                                norm2_gain[layer][None, :], peer_w_q[layer].astype(BF16),
                                peer_sub_keys[layer].astype(BF16), n_rows // ROW_TILE,
                                n_lat // ROW_TILE, seq // ROW_TILE, n_batch)
        xs = _peer_dense(h2t, coefs, u_tab, v_tab,
                         xn, mods, layer, n_rows // PEER_TILE, n_lat // PEER_TILE,
                         seq // PEER_TILE, n_batch)
        rows = (xs, xs, n_lat // ROW_TILE)
    return xs[:n_lat].reshape(n_batch, seq, d)
```

```python
import functools
import math

import jax
import jax.numpy as jnp
from jax import lax
from jax.experimental import pallas as pl
from jax.experimental.pallas import tpu as pltpu

F32 = jnp.float32
BF16 = jnp.bfloat16

D_MODEL = 1024
N_HEADS = 8
N_KV_HEADS = 2
HEAD_DIM = 128
KV_REP = N_HEADS // N_KV_HEADS
QKV_DIM = (N_HEADS + 2 * N_KV_HEADS) * HEAD_DIM
GRID_W = 64
WINDOW = 128
ROPE_THETA = 10000.0
RMS_EPS = 1e-6
NEG_INF = -1e30
PEER_HEADS = 8
PEER_TOPK = 16
N_KEYS = 128
PEER_QDIM = 256
N_MOD = 6
MOD_ROWS = 8

ROW_TILE = 256
Q_TILE = 256
PEER_TILE = 512
PEER_CHUNK = 16
GATE_KEYS = 8
SUB_LANES = 128
GATE_ROWS = 32
SUBLANES = 8
MOD_COLS = 1536
THRESHOLD_SLACK = 2.0 ** -22
VMEM_LIMIT = 56 * 1024 * 1024


def _cparams(sem):
    return pltpu.CompilerParams(dimension_semantics=sem, vmem_limit_bytes=VMEM_LIMIT)


def _nt_dot(a, b):
    return lax.dot_general(a, b, (((1,), (1,)), ((), ())), preferred_element_type=F32)


def _mod_kernel(c_ref, w_ref, b_ref, o_ref):
    c = c_ref[...]
    a = c * (1.0 / (1.0 + jnp.exp(-c)))
    o_ref[0] = jnp.dot(a, w_ref[0], preferred_element_type=F32,
                       precision=lax.Precision.HIGHEST) + b_ref[0]


def _modulation(cc, ada_w, ada_b):
    depth, d, n = ada_w.shape
    tn = MOD_COLS
    return pl.pallas_call(
        _mod_kernel,
        grid=(depth, n // tn),
        in_specs=[pl.BlockSpec((MOD_ROWS, d), lambda l, j: (0, 0)),
                  pl.BlockSpec((1, d, tn), lambda l, j: (l, 0, j)),
                  pl.BlockSpec((1, 1, tn), lambda l, j: (l, 0, j))],
        out_specs=pl.BlockSpec((1, MOD_ROWS, tn), lambda l, j: (l, 0, j)),
        out_shape=jax.ShapeDtypeStruct((depth, MOD_ROWS, n), F32),
        compiler_params=_cparams(("arbitrary", "arbitrary")),
    )(cc, ada_w, ada_b.reshape(depth, 1, n))


def _rms(x):
    return x * lax.rsqrt(jnp.mean(x * x, axis=-1, keepdims=True) + RMS_EPS)


def _select_rows(lat_tiles, lat_ref, ctx_ref):
    return jnp.where(pl.program_id(0) < lat_tiles, lat_ref[...], ctx_ref[...])


def _row_specs(t, d, lat_tiles, ctx_off):
    return [pl.BlockSpec((t, d), lambda i: (jnp.minimum(i, lat_tiles - 1), 0)),
            pl.BlockSpec((t, d), lambda i: (ctx_off + jnp.maximum(i - lat_tiles, 0), 0))]


def _qkv_kernel(lat_tiles, xl_ref, xc_ref, sh_ref, sc_ref, g_ref, w_ref, qg_ref, kg_ref, cos_ref,
                sin_ref, q_ref, k_ref, vt_ref):
    h = _rms(_select_rows(lat_tiles, xl_ref, xc_ref)) * g_ref[...]
    h = h * (1.0 + sc_ref[0]) + sh_ref[0]
    y = jnp.dot(h.astype(BF16), w_ref[...], preferred_element_type=F32)
    cos = cos_ref[...]
    sin = sin_ref[...]
    lane = lax.broadcasted_iota(jnp.int32, cos.shape, 1)
    first = (lane % (HEAD_DIM // 2)) < (HEAD_DIM // 4)

    def head(col, gain):
        n = _rms(y[:, col * HEAD_DIM:(col + 1) * HEAD_DIM]) * gain
        sw = jnp.where(first, pltpu.roll(n, HEAD_DIM - HEAD_DIM // 4, 1),
                       pltpu.roll(n, HEAD_DIM // 4, 1))
        return n * cos + sw * sin

    for hh in range(N_HEADS):
        q_ref[:, hh * HEAD_DIM:(hh + 1) * HEAD_DIM] = head(hh, qg_ref[...]).astype(BF16)
    for g in range(N_KV_HEADS):
        k_ref[:, g * HEAD_DIM:(g + 1) * HEAD_DIM] = head(N_HEADS + g, kg_ref[...]).astype(BF16)
    v0 = (N_HEADS + N_KV_HEADS) * HEAD_DIM
    vt_ref[...] = y[:, v0:].T.astype(BF16)


def _qkv(rows, n, mods, layer, gain, w, qg, kg, cos_t, sin_t, lat_tiles, tiles_per_seq, n_batch):
    lat_src, ctx_src, ctx_off = rows
    d = lat_src.shape[1]
    t = ROW_TILE

    def mrow(i):
        return layer * MOD_ROWS + jnp.where(i < lat_tiles, i // tiles_per_seq, n_batch)

    def trow(i):
        return jnp.where(i < lat_tiles, i % tiles_per_seq, tiles_per_seq)

    kvd = N_KV_HEADS * HEAD_DIM
    return pl.pallas_call(
        functools.partial(_qkv_kernel, lat_tiles),
        grid=(n // t,),
        in_specs=_row_specs(t, d, lat_tiles, ctx_off) + [
                  pl.BlockSpec((1, 1, d), lambda i: (mrow(i), 0, 0)),
                  pl.BlockSpec((1, 1, d), lambda i: (mrow(i), 0, 1)),
                  pl.BlockSpec((1, d), lambda i: (0, 0)),
                  pl.BlockSpec((d, QKV_DIM), lambda i: (0, 0)),
                  pl.BlockSpec((1, HEAD_DIM), lambda i: (0, 0)),
                  pl.BlockSpec((1, HEAD_DIM), lambda i: (0, 0)),
                  pl.BlockSpec((t, HEAD_DIM), lambda i: (trow(i), 0)),
                  pl.BlockSpec((t, HEAD_DIM), lambda i: (trow(i), 0))],
        out_specs=[pl.BlockSpec((t, d), lambda i: (i, 0)),
                   pl.BlockSpec((t, kvd), lambda i: (i, 0)),
                   pl.BlockSpec((kvd, t), lambda i: (0, i))],
        out_shape=[jax.ShapeDtypeStruct((n, d), BF16),
                   jax.ShapeDtypeStruct((n, kvd), BF16),
                   jax.ShapeDtypeStruct((kvd, n), BF16)],
        compiler_params=_cparams(("arbitrary",)),
    )(lat_src, ctx_src, mods, mods, gain, w, qg, kg, cos_t, sin_t)


def _stack_heads(q_ref):
    return jnp.concatenate(
        [q_ref[:, r * HEAD_DIM:(r + 1) * HEAD_DIM] for r in range(KV_REP)], axis=0)


def _store_heads(ot, o_ref):
    for r in range(KV_REP):
        blk = ot[:, r * Q_TILE:(r + 1) * Q_TILE]
        o_ref[:, r * HEAD_DIM:(r + 1) * HEAD_DIM] = blk.T.astype(BF16)


def _global_attn_kernel(lat_q_tiles, q_ref, kl_ref, kc_ref, vtl_ref, vtc_ref, o_ref):
    qt = pl.program_id(2)
    q4 = _stack_heads(q_ref)
    sc = _nt_dot(kc_ref[...], q4)

    def finish(ot, den):
        _store_heads(ot * (1.0 / den), o_ref)

    @pl.when(qt < lat_q_tiles)
    def _():
        sl = _nt_dot(kl_ref[...], q4)
        m = jnp.maximum(jnp.max(sl, axis=0, keepdims=True), jnp.max(sc, axis=0, keepdims=True))
        p_l = jnp.exp2(sl - m)
        p_c = jnp.exp2(sc - m)
        den = jnp.sum(p_l, axis=0, keepdims=True) + jnp.sum(p_c, axis=0, keepdims=True)
        finish(jnp.dot(vtl_ref[...], p_l.astype(BF16), preferred_element_type=F32)
               + jnp.dot(vtc_ref[...], p_c.astype(BF16), preferred_element_type=F32), den)

    @pl.when(qt >= lat_q_tiles)
    def _():
        p_c = jnp.exp2(sc - jnp.max(sc, axis=0, keepdims=True))
        finish(jnp.dot(vtc_ref[...], p_c.astype(BF16), preferred_element_type=F32),
               jnp.sum(p_c, axis=0, keepdims=True))


def _global_attention(q, k, vt, n_batch, seq, ctx_len):
    n, d = q.shape
    lat_q = seq // Q_TILE
    ctx_q = ctx_len // Q_TILE
    ctx_blk0 = (n_batch * seq) // ctx_len
    gw = KV_REP * HEAD_DIM

    def qrow(b, g, t):
        return jnp.where(t < lat_q, b * lat_q + t, n_batch * lat_q + b * ctx_q + (t - lat_q))

    return pl.pallas_call(
        functools.partial(_global_attn_kernel, lat_q),
        grid=(n_batch, N_KV_HEADS, lat_q + ctx_q),
        in_specs=[pl.BlockSpec((Q_TILE, gw), lambda b, g, t: (qrow(b, g, t), g)),
                  pl.BlockSpec((seq, HEAD_DIM), lambda b, g, t: (b, g)),
                  pl.BlockSpec((ctx_len, HEAD_DIM), lambda b, g, t: (ctx_blk0 + b, g)),
                  pl.BlockSpec((HEAD_DIM, seq), lambda b, g, t: (g, b)),
                  pl.BlockSpec((HEAD_DIM, ctx_len), lambda b, g, t: (g, ctx_blk0 + b))],
        out_specs=pl.BlockSpec((Q_TILE, gw), lambda b, g, t: (qrow(b, g, t), g)),
        out_shape=jax.ShapeDtypeStruct((n, d), BF16),
        compiler_params=_cparams(("arbitrary", "arbitrary", "arbitrary")),
    )(q, k, k, vt, vt)


def _window_attn_kernel(seq, q_ref, kl_ref, kc_ref, vtl_ref, vtc_ref, sink_ref, o_ref):
    qt = pl.program_id(2)
    span = Q_TILE + 2 * WINDOW
    start = pl.multiple_of(jnp.clip(qt * Q_TILE - WINDOW, 0, seq - span), math.gcd(Q_TILE, WINDOW))
    q4 = _stack_heads(q_ref)
    cols = KV_REP * Q_TILE
    s_w = _nt_dot(kl_ref[pl.ds(start, span), :], q4)
    s_c = _nt_dot(kc_ref[...], q4)
    kpos = start + lax.broadcasted_iota(jnp.int32, (span, 1), 0)
    qpos = qt * Q_TILE + lax.broadcasted_iota(jnp.int32, (1, cols), 1) % Q_TILE
    s_w = jnp.where(jnp.abs(qpos - kpos) <= WINDOW, s_w, NEG_INF)
    sink = sink_ref[0]
    m = jnp.maximum(jnp.maximum(jnp.max(s_w, axis=0, keepdims=True),
                                jnp.max(s_c, axis=0, keepdims=True)), sink)
    p_w = jnp.exp2(s_w - m)
    p_c = jnp.exp2(s_c - m)
    den = (jnp.sum(p_w, axis=0, keepdims=True) + jnp.sum(p_c, axis=0, keepdims=True)
           + jnp.exp2(sink - m))
    ot = (jnp.dot(vtl_ref[:, pl.ds(start, span)], p_w.astype(BF16), preferred_element_type=F32)
          + jnp.dot(vtc_ref[...], p_c.astype(BF16), preferred_element_type=F32))
    _store_heads(ot * (1.0 / den), o_ref)


def _window_attention(q, k, vt, sink_cols, n_batch, seq, ctx_len):
    d = q.shape[1]
    n = n_batch * seq
    lat_q = seq // Q_TILE
    ctx_blk0 = (n_batch * seq) // ctx_len
    gw = KV_REP * HEAD_DIM
    return pl.pallas_call(
        functools.partial(_window_attn_kernel, seq),
        grid=(n_batch, N_KV_HEADS, lat_q),
        in_specs=[pl.BlockSpec((Q_TILE, gw), lambda b, g, t: (b * lat_q + t, g)),
                  pl.BlockSpec((seq, HEAD_DIM), lambda b, g, t: (b, g)),
                  pl.BlockSpec((ctx_len, HEAD_DIM), lambda b, g, t: (ctx_blk0 + b, g)),
                  pl.BlockSpec((HEAD_DIM, seq), lambda b, g, t: (g, b)),
                  pl.BlockSpec((HEAD_DIM, ctx_len), lambda b, g, t: (g, ctx_blk0 + b)),
                  pl.BlockSpec((1, 1, KV_REP * Q_TILE), lambda b, g, t: (g, 0, 0))],
        out_specs=pl.BlockSpec((Q_TILE, gw), lambda b, g, t: (b * lat_q + t, g)),
        out_shape=jax.ShapeDtypeStruct((n, d), BF16),
        compiler_params=_cparams(("arbitrary", "arbitrary", "arbitrary")),
    )(q, k, k, vt, vt, sink_cols)


def _peer_pre_kernel(lat_tiles, o_ref, xl_ref, xc_ref, wo_ref, g1_ref, n2_ref, sh_ref, sc_ref, wq_ref,
                     keys_ref, xn_ref, h2t_ref, s2_ref, e2_ref, thr_ref, e1_ref, st_ref):
    y = jnp.dot(o_ref[...], wo_ref[...], preferred_element_type=F32)
    xn = _select_rows(lat_tiles, xl_ref, xc_ref) + g1_ref[0] * y
    xn_ref[...] = xn
    h2 = _rms(xn) * n2_ref[...]
    h2 = h2 * (1.0 + sc_ref[0]) + sh_ref[0]
    h2t_ref[...] = h2.T.astype(BF16)
    qp = jnp.dot(h2.astype(BF16), wq_ref[...], preferred_element_type=F32)
    half = PEER_QDIM // 2
    for hs in range(2 * PEER_HEADS):
        qs = qp[:, hs * half:(hs + 1) * half].astype(BF16)
        st_ref[hs] = _nt_dot(keys_ref[hs % 2], qs)
    _peer_topk(st_ref, s2_ref, e2_ref, thr_ref, e1_ref)


def _peer_pre(o, rows, mods, layer, w_o, n2, w_q, keys, n_tiles, lat_tiles, tiles_per_seq, n_batch):
    lat_src, ctx_src, ctx_off = rows
    d = o.shape[1]
    t = ROW_TILE
    n = n_tiles * t
    qd = PEER_HEADS * PEER_QDIM

    def mrow(i):
        return layer * MOD_ROWS + jnp.where(i < lat_tiles, i // tiles_per_seq, n_batch)

    cspec = pl.BlockSpec((t // SUB_LANES, PEER_HEADS, N_KEYS, SUB_LANES), lambda i: (i, 0, 0, 0))
    cshape = jax.ShapeDtypeStruct((n // SUB_LANES, PEER_HEADS, N_KEYS, SUB_LANES), F32)
    outs = pl.pallas_call(
        functools.partial(_peer_pre_kernel, lat_tiles),
        grid=(n_tiles,),
        in_specs=[pl.BlockSpec((t, d), lambda i: (i, 0))] + _row_specs(t, d, lat_tiles, ctx_off) + [
                  pl.BlockSpec((d, d), lambda i: (0, 0)),
                  pl.BlockSpec((1, 1, d), lambda i: (mrow(i), 0, 2)),
                  pl.BlockSpec((1, d), lambda i: (0, 0)),
                  pl.BlockSpec((1, 1, d), lambda i: (mrow(i), 0, 3)),
                  pl.BlockSpec((1, 1, d), lambda i: (mrow(i), 0, 4)),
                  pl.BlockSpec((d, qd), lambda i: (0, 0)),
                  pl.BlockSpec((2, N_KEYS, PEER_QDIM // 2), lambda i: (0, 0, 0))],
        out_specs=[pl.BlockSpec((t, d), lambda i: (i, 0)),
                   pl.BlockSpec((d, t), lambda i: (0, i)),
                   cspec, cspec, cspec, cspec],
        out_shape=[jax.ShapeDtypeStruct((n, d), F32),
                   jax.ShapeDtypeStruct((d, n), BF16),
                   cshape, cshape, cshape, cshape],
        scratch_shapes=[pltpu.VMEM((2 * PEER_HEADS, N_KEYS, t), F32)],
        compiler_params=_cparams(("arbitrary",)),
    )(o, lat_src, ctx_src, w_o, mods, n2, mods, mods, w_q, keys)
    return outs[0], outs[1], outs[2:]


def _sorting_network(n):
    def merge(lo, hi, r):
        step = r * 2
        if step < hi - lo:
            yield from merge(lo, hi, step)
            yield from merge(lo + r, hi, step)
            yield from [(i, i + r) for i in range(lo + r, hi - r, step)]
        else:
            yield (lo, lo + r)

    def sort(lo, hi):
        if hi - lo >= 1:
            mid = lo + (hi - lo) // 2
            yield from sort(lo, mid)
            yield from sort(mid + 1, hi)
            yield from merge(lo, hi, 1)

    return list(sort(0, n - 1))


def _exchange(x, i, j):
    x[i], x[j] = jnp.maximum(x[i], x[j]), jnp.minimum(x[i], x[j])


def _top_values(s):
    k = PEER_TOPK
    assert s.shape[0] == k * SUBLANES
    x = [s[i * SUBLANES:(i + 1) * SUBLANES] for i in range(k)]
    for i, j in _sorting_network(k):
        _exchange(x, i, j)
    shift = SUBLANES // 2
    while shift:
        x = [jnp.maximum(x[i], pltpu.roll(x[k - 1 - i], shift, 0)) for i in range(k)]
        d = k // 2
        while d:
            for i in range(k):
                if not i & d:
                    _exchange(x, i, i + d)
            d //= 2
        shift //= 2
    return [xi[0:1] for xi in x]


def _kth_largest(s, k):
    for _ in range(k - 1):
        s = jnp.where(s == jnp.max(s, axis=0, keepdims=True), -jnp.inf, s)
    return jnp.max(s, axis=0, keepdims=True)


def _candidate_sums(v1, v2):
    assert PEER_TOPK == 16
    a1 = jnp.concatenate(v1, axis=0)
    a2 = jnp.concatenate(v2, axis=0)
    row = lax.broadcasted_iota(jnp.int32, (8, a1.shape[1]), 0)
    ninf = -jnp.inf
    return jnp.concatenate([
        v1[0] + a2,
        v1[1] + a2[0:8],
        jnp.where(row < 5, v1[2] + a2[0:8], ninf),
        jnp.where(row < 4, v1[3] + a2[0:8], ninf),
        a1[8:16] + v2[0],
        jnp.where(row >= 4, a1[0:8] + v2[0], ninf),
        jnp.where(row >= 4, a1[0:8] + v2[1], ninf),
        jnp.where(row == 4, a1[0:8] + v2[2], ninf),
    ], axis=0)


def _peer_topk(st_ref, s2_ref, e2_ref, thr_ref, e1_ref):
    def per_head(h, carry):
        s1 = st_ref[2 * h]
        s2 = st_ref[2 * h + 1]
        v1 = _top_values(s1)
        v2 = _top_values(s2)
        cand = _candidate_sums(v1, v2)
        thr = _kth_largest(cand, PEER_TOPK)
        top = v1[0] + v2[0]
        z = jnp.sum(jnp.where(cand >= thr, jnp.exp(cand - top), 0.0), axis=0, keepdims=True)
        guard = thr - jnp.abs(thr) * THRESHOLD_SLACK
        outs = ((s2_ref, s2), (e2_ref, jnp.exp(s2 - v2[0]) * (1.0 / z)),
                (thr_ref, guard - s1), (e1_ref, jnp.exp(s1 - v1[0])))
        for ref, val in outs:
            for lb in range(val.shape[1] // SUB_LANES):
                ref[lb, h] = val[:, lb * SUB_LANES:(lb + 1) * SUB_LANES]
        return carry

    lax.fori_loop(0, PEER_HEADS, per_head, 0)


def _gelu_tanh(x):
    k = -2.0 * math.log2(math.e) * math.sqrt(2.0 / math.pi)
    return x / (1.0 + jnp.exp2(x * (k + (k * 0.044715) * (x * x))))


def _peer_dense_kernel(h2t_ref, s2_ref, e2_ref, thr_ref, e1_ref, u_ref, v_ref, xn_ref, g2_ref,
                       out_ref, acc_ref, act_ref, slab_ref):
    j = pl.program_id(1)
    lanes = acc_ref.shape[1]
    lane_blocks = lanes // SUB_LANES
    row_blocks = N_KEYS // GATE_ROWS

    @pl.when(j == 0)
    def _():
        acc_ref[...] = jnp.zeros_like(acc_ref)

    act_ref[...] = jnp.dot(u_ref[0], h2t_ref[...], preferred_element_type=F32)

    def gate_block(blk, carry):
        grp = blk // (row_blocks * lane_blocks)
        bc = (blk // lane_blocks) % row_blocks
        lc = blk % lane_blocks
        rows = pl.ds(pl.multiple_of(bc * GATE_ROWS, GATE_ROWS), GATE_ROWS)
        cols = pl.ds(pl.multiple_of(lc * SUB_LANES, SUB_LANES), SUB_LANES)
        a0 = j * PEER_CHUNK + grp * GATE_KEYS
        w = [None] * GATE_KEYS
        for h in range(PEER_HEADS):
            s2 = s2_ref[lc, h, rows, :]
            e2 = e2_ref[lc, h, rows, :]
            for al in range(GATE_KEYS):
                thr = thr_ref[lc, h, pl.ds(a0 + al, GATE_ROWS, stride=0), :]
                e1 = e1_ref[lc, h, pl.ds(a0 + al, GATE_ROWS, stride=0), :]
                term = jnp.where(s2 >= thr, e2, 0.0) * e1
                w[al] = term if h == 0 else w[al] + term
        for al in range(GATE_KEYS):
            r0 = (grp * GATE_KEYS + al) * N_KEYS + bc * GATE_ROWS
            arows = pl.ds(pl.multiple_of(r0, GATE_ROWS), GATE_ROWS)
            slab_ref[arows, cols] = (w[al] * _gelu_tanh(act_ref[arows, cols])).astype(BF16)
        return carry

    lax.fori_loop(0, (PEER_CHUNK // GATE_KEYS) * row_blocks * lane_blocks, gate_block, 0, unroll=4)
    acc_ref[...] += lax.dot_general(v_ref[0], slab_ref[...], (((0,), (0,)), ((), ())),
                                    preferred_element_type=F32)

    @pl.when(j == pl.num_programs(1) - 1)
    def _():
        out_ref[...] = xn_ref[...] + g2_ref[0] * acc_ref[...].T


def _peer_dense(h2t, coefs, u, v, xn, mods, layer, n_tiles, lat_tiles, tiles_per_seq, n_batch):
    d = xn.shape[1]
    t = PEER_TILE
    ce = PEER_CHUNK * N_KEYS
    n_exp = u.shape[1]

    def mrow(i):
        return layer * MOD_ROWS + jnp.where(i < lat_tiles, i // tiles_per_seq, n_batch)

    cspec = pl.BlockSpec((t // SUB_LANES, PEER_HEADS, N_KEYS, SUB_LANES), lambda i, j: (i, 0, 0, 0))
    return pl.pallas_call(
        _peer_dense_kernel,
        grid=(n_tiles, n_exp // ce),
        in_specs=[pl.BlockSpec((d, t), lambda i, j: (0, i)),
                  cspec, cspec, cspec, cspec,
                  pl.BlockSpec((1, ce, d), lambda i, j: (layer, j, 0)),
                  pl.BlockSpec((1, ce, d), lambda i, j: (layer, j, 0)),
                  pl.BlockSpec((t, d), lambda i, j: (i, 0)),
                  pl.BlockSpec((1, 1, d), lambda i, j: (mrow(i), 0, 5))],
        out_specs=pl.BlockSpec((t, d), lambda i, j: (i, 0)),
        out_shape=jax.ShapeDtypeStruct((n_tiles * t, d), F32),
        scratch_shapes=[pltpu.VMEM((d, t), F32),
                        pltpu.VMEM((ce, t), F32),
                        pltpu.VMEM((ce, t), BF16)],
        compiler_params=_cparams(("arbitrary", "arbitrary")),
    )(h2t, *coefs, u, v, xn, mods)


def _rope_tables(seq, ctx_len):
    pos = jnp.arange(seq)
    row = (pos // GRID_W).astype(F32)
    col = (pos % GRID_W).astype(F32)
    half = HEAD_DIM // 2
    inv = ROPE_THETA ** (-jnp.arange(0, half, 2, dtype=F32) / half)
    ar = row[:, None] * inv
    ac = col[:, None] * inv
    cos = jnp.concatenate([jnp.cos(ar), jnp.cos(ar), jnp.cos(ac), jnp.cos(ac)], axis=-1)
    sin = jnp.concatenate([-jnp.sin(ar), jnp.sin(ar), -jnp.sin(ac), jnp.sin(ac)], axis=-1)
    cos = jnp.concatenate([cos, jnp.ones((ctx_len, HEAD_DIM), F32)], axis=0)
    sin = jnp.concatenate([sin, jnp.zeros((ctx_len, HEAD_DIM), F32)], axis=0)
    return cos, sin


def kernel(x, c, ctx, c_ctx, ada_w, ada_b, norm1_gain, norm2_gain, w_qkv, q_norm_gain, k_norm_gain,
           w_o, attn_sinks, peer_w_q, peer_sub_keys, peer_u, peer_v):
    n_batch, seq, d = x.shape
    ctx_len = ctx.shape[1]
    depth = ada_w.shape[0]
    n_lat = n_batch * seq
    n_ctx = n_batch * ctx_len
    assert depth == 2 and d == D_MODEL and n_batch < MOD_ROWS
    assert seq % PEER_TILE == 0 and n_ctx % PEER_TILE == 0 and ctx_len == ROW_TILE
    assert seq >= Q_TILE + 2 * WINDOW and n_lat % ctx_len == 0
    assert seq % Q_TILE == 0 and ctx_len % Q_TILE == 0 and (N_MOD * d) % MOD_COLS == 0

    cc = jnp.concatenate([c, c_ctx[None, :], jnp.zeros((MOD_ROWS - n_batch - 1, d), F32)], axis=0)
    mods = _modulation(cc, ada_w, ada_b).reshape(depth * MOD_ROWS, 1, N_MOD * d)
    cos_t, sin_t = _rope_tables(seq, ctx_len)
    u_tab = peer_u.astype(BF16)
    v_tab = peer_v.astype(BF16)
    rows = (x.reshape(n_lat, d), ctx.reshape(n_ctx, d), 0)
    scale = HEAD_DIM ** -0.5 * math.log2(math.e)

    for layer in range(depth):
        last = layer == depth - 1
        q, k, vt = _qkv(rows, n_lat + n_ctx, mods, layer, norm1_gain[layer][None, :],
                        w_qkv[layer].astype(BF16),
                        (q_norm_gain[layer] * scale)[None, :], k_norm_gain[layer][None, :],
                        cos_t, sin_t, n_lat // ROW_TILE, seq // ROW_TILE, n_batch)
        if layer % 2 == 0:
            o = _global_attention(q, k, vt, n_batch, seq, ctx_len)
        else:
            sink = (attn_sinks[layer // 2] * math.log2(math.e)).reshape(N_KV_HEADS, 1, KV_REP, 1)
            sink_cols = jnp.broadcast_to(sink, (N_KV_HEADS, 1, KV_REP, Q_TILE))
            o = _window_attention(q, k, vt, sink_cols.reshape(N_KV_HEADS, 1, KV_REP * Q_TILE),
                                  n_batch, seq, ctx_len)
        n_rows = n_lat if last else n_lat + n_ctx
        xn, h2t, coefs = _peer_pre(o, rows, mods, layer, w_o[layer].astype(BF16),
                                norm2_gain[layer][None, :], peer_w_q[layer].astype(BF16),
                                peer_sub_keys[layer].astype(BF16), n_rows // ROW_TILE,
                                n_lat // ROW_TILE, seq // ROW_TILE, n_batch)
        xs = _peer_dense(h2t, coefs, u_tab, v_tab,
                         xn, mods, layer, n_rows // PEER_TILE, n_lat // PEER_TILE,
                         seq // PEER_TILE, n_batch)
        rows = (xs, xs, n_lat // ROW_TILE)
    return xs[:n_lat].reshape(n_batch, seq, d)
```

```python
import functools
import math

import jax
import jax.numpy as jnp
from jax import lax
from jax.experimental import pallas as pl
from jax.experimental.pallas import tpu as pltpu

F32 = jnp.float32
BF16 = jnp.bfloat16

D_MODEL = 1024
N_HEADS = 8
N_KV_HEADS = 2
HEAD_DIM = 128
KV_REP = N_HEADS // N_KV_HEADS
QKV_DIM = (N_HEADS + 2 * N_KV_HEADS) * HEAD_DIM
GRID_W = 64
WINDOW = 128
ROPE_THETA = 10000.0
RMS_EPS = 1e-6
NEG_INF = -1e30
PEER_HEADS = 8
PEER_TOPK = 16
N_KEYS = 128
PEER_QDIM = 256
N_MOD = 6
MOD_ROWS = 8

ROW_TILE = 256
Q_TILE = 256
PEER_TILE = 512
PEER_CHUNK = 16
GATE_KEYS = 8
SUB_LANES = 128
GATE_ROWS = 32
SUBLANES = 8
MOD_COLS = 1536
THRESHOLD_SLACK = 2.0 ** -22
VMEM_LIMIT = 56 * 1024 * 1024


def _cparams(sem):
    return pltpu.CompilerParams(dimension_semantics=sem, vmem_limit_bytes=VMEM_LIMIT)


def _nt_dot(a, b):
    return lax.dot_general(a, b, (((1,), (1,)), ((), ())), preferred_element_type=F32)


def _mod_kernel(c_ref, w_ref, b_ref, o_ref):
    c = c_ref[...]
    a = c * (1.0 / (1.0 + jnp.exp(-c)))
    o_ref[0] = jnp.dot(a, w_ref[0], preferred_element_type=F32,
                       precision=lax.Precision.HIGHEST) + b_ref[0]


def _modulation(cc, ada_w, ada_b):
    depth, d, n = ada_w.shape
    tn = MOD_COLS
    return pl.pallas_call(
        _mod_kernel,
        grid=(depth, n // tn),
        in_specs=[pl.BlockSpec((MOD_ROWS, d), lambda l, j: (0, 0)),
                  pl.BlockSpec((1, d, tn), lambda l, j: (l, 0, j)),
                  pl.BlockSpec((1, 1, tn), lambda l, j: (l, 0, j))],
        out_specs=pl.BlockSpec((1, MOD_ROWS, tn), lambda l, j: (l, 0, j)),
        out_shape=jax.ShapeDtypeStruct((depth, MOD_ROWS, n), F32),
        compiler_params=_cparams(("arbitrary", "arbitrary")),
    )(cc, ada_w, ada_b.reshape(depth, 1, n))


def _rms(x):
    return x * lax.rsqrt(jnp.mean(x * x, axis=-1, keepdims=True) + RMS_EPS)


def _select_rows(lat_tiles, lat_ref, ctx_ref):
    return jnp.where(pl.program_id(0) < lat_tiles, lat_ref[...], ctx_ref[...])


def _row_specs(t, d, lat_tiles, ctx_off):
    return [pl.BlockSpec((t, d), lambda i: (jnp.minimum(i, lat_tiles - 1), 0)),
            pl.BlockSpec((t, d), lambda i: (ctx_off + jnp.maximum(i - lat_tiles, 0), 0))]


def _qkv_kernel(lat_tiles, xl_ref, xc_ref, sh_ref, sc_ref, g_ref, w_ref, qg_ref, kg_ref, cos_ref,
                sin_ref, q_ref, k_ref, vt_ref):
    h = _rms(_select_rows(lat_tiles, xl_ref, xc_ref)) * g_ref[...]
    h = h * (1.0 + sc_ref[0]) + sh_ref[0]
    y = jnp.dot(h.astype(BF16), w_ref[...], preferred_element_type=F32)
    cos = cos_ref[...]
    sin = sin_ref[...]
    lane = lax.broadcasted_iota(jnp.int32, cos.shape, 1)
    first = (lane % (HEAD_DIM // 2)) < (HEAD_DIM // 4)

    def head(col, gain):
        n = _rms(y[:, col * HEAD_DIM:(col + 1) * HEAD_DIM]) * gain
        sw = jnp.where(first, pltpu.roll(n, HEAD_DIM - HEAD_DIM // 4, 1),
                       pltpu.roll(n, HEAD_DIM // 4, 1))
        return n * cos + sw * sin

    for hh in range(N_HEADS):
        q_ref[:, hh * HEAD_DIM:(hh + 1) * HEAD_DIM] = head(hh, qg_ref[...]).astype(BF16)
    for g in range(N_KV_HEADS):
        k_ref[:, g * HEAD_DIM:(g + 1) * HEAD_DIM] = head(N_HEADS + g, kg_ref[...]).astype(BF16)
    v0 = (N_HEADS + N_KV_HEADS) * HEAD_DIM
    vt_ref[...] = y[:, v0:].T.astype(BF16)


def _qkv(rows, n, mods, layer, gain, w, qg, kg, cos_t, sin_t, lat_tiles, tiles_per_seq, n_batch):
    lat_src, ctx_src, ctx_off = rows
    d = lat_src.shape[1]
    t = ROW_TILE

    def mrow(i):
        return layer * MOD_ROWS + jnp.where(i < lat_tiles, i // tiles_per_seq, n_batch)

    def trow(i):
        return jnp.where(i < lat_tiles, i % tiles_per_seq, tiles_per_seq)

    kvd = N_KV_HEADS * HEAD_DIM
    return pl.pallas_call(
        functools.partial(_qkv_kernel, lat_tiles),
        grid=(n // t,),
        in_specs=_row_specs(t, d, lat_tiles, ctx_off) + [
                  pl.BlockSpec((1, 1, d), lambda i: (mrow(i), 0, 0)),
                  pl.BlockSpec((1, 1, d), lambda i: (mrow(i), 0, 1)),
                  pl.BlockSpec((1, d), lambda i: (0, 0)),
                  pl.BlockSpec((d, QKV_DIM), lambda i: (0, 0)),
                  pl.BlockSpec((1, HEAD_DIM), lambda i: (0, 0)),
                  pl.BlockSpec((1, HEAD_DIM), lambda i: (0, 0)),
                  pl.BlockSpec((t, HEAD_DIM), lambda i: (trow(i), 0)),
                  pl.BlockSpec((t, HEAD_DIM), lambda i: (trow(i), 0))],
        out_specs=[pl.BlockSpec((t, d), lambda i: (i, 0)),
                   pl.BlockSpec((t, kvd), lambda i: (i, 0)),
                   pl.BlockSpec((kvd, t), lambda i: (0, i))],
        out_shape=[jax.ShapeDtypeStruct((n, d), BF16),
                   jax.ShapeDtypeStruct((n, kvd), BF16),
                   jax.ShapeDtypeStruct((kvd, n), BF16)],
        compiler_params=_cparams(("arbitrary",)),
    )(lat_src, ctx_src, mods, mods, gain, w, qg, kg, cos_t, sin_t)


def _stack_heads(q_ref):
    return jnp.concatenate(
        [q_ref[:, r * HEAD_DIM:(r + 1) * HEAD_DIM] for r in range(KV_REP)], axis=0)


def _store_heads(ot, o_ref):
    for r in range(KV_REP):
        blk = ot[:, r * Q_TILE:(r + 1) * Q_TILE]
        o_ref[:, r * HEAD_DIM:(r + 1) * HEAD_DIM] = blk.T.astype(BF16)


def _global_attn_kernel(lat_q_tiles, q_ref, kl_ref, kc_ref, vtl_ref, vtc_ref, o_ref):
    qt = pl.program_id(2)
    q4 = _stack_heads(q_ref)
    sc = _nt_dot(kc_ref[...], q4)

    def finish(ot, den):
        _store_heads(ot * (1.0 / den), o_ref)

    @pl.when(qt < lat_q_tiles)
    def _():
        sl = _nt_dot(kl_ref[...], q4)
        m = jnp.maximum(jnp.max(sl, axis=0, keepdims=True), jnp.max(sc, axis=0, keepdims=True))
        p_l = jnp.exp2(sl - m)
        p_c = jnp.exp2(sc - m)
        den = jnp.sum(p_l, axis=0, keepdims=True) + jnp.sum(p_c, axis=0, keepdims=True)
        finish(jnp.dot(vtl_ref[...], p_l.astype(BF16), preferred_element_type=F32)
               + jnp.dot(vtc_ref[...], p_c.astype(BF16), preferred_element_type=F32), den)

    @pl.when(qt >= lat_q_tiles)
    def _():
        p_c = jnp.exp2(sc - jnp.max(sc, axis=0, keepdims=True))
        finish(jnp.dot(vtc_ref[...], p_c.astype(BF16), preferred_element_type=F32),
               jnp.sum(p_c, axis=0, keepdims=True))


def _global_attention(q, k, vt, n_batch, seq, ctx_len):
    n, d = q.shape
    lat_q = seq // Q_TILE
    ctx_q = ctx_len // Q_TILE
    ctx_blk0 = (n_batch * seq) // ctx_len
    gw = KV_REP * HEAD_DIM

    def qrow(b, g, t):
        return jnp.where(t < lat_q, b * lat_q + t, n_batch * lat_q + b * ctx_q + (t - lat_q))

    return pl.pallas_call(
        functools.partial(_global_attn_kernel, lat_q),
        grid=(n_batch, N_KV_HEADS, lat_q + ctx_q),
        in_specs=[pl.BlockSpec((Q_TILE, gw), lambda b, g, t: (qrow(b, g, t), g)),
                  pl.BlockSpec((seq, HEAD_DIM), lambda b, g, t: (b, g)),
                  pl.BlockSpec((ctx_len, HEAD_DIM), lambda b, g, t: (ctx_blk0 + b, g)),
                  pl.BlockSpec((HEAD_DIM, seq), lambda b, g, t: (g, b)),
                  pl.BlockSpec((HEAD_DIM, ctx_len), lambda b, g, t: (g, ctx_blk0 + b))],
        out_specs=pl.BlockSpec((Q_TILE, gw), lambda b, g, t: (qrow(b, g, t), g)),
        out_shape=jax.ShapeDtypeStruct((n, d), BF16),
        compiler_params=_cparams(("arbitrary", "arbitrary", "arbitrary")),
    )(q, k, k, vt, vt)


def _window_attn_kernel(seq, q_ref, kl_ref, kc_ref, vtl_ref, vtc_ref, sink_ref, o_ref):
    qt = pl.program_id(2)
    span = Q_TILE + 2 * WINDOW
    start = pl.multiple_of(jnp.clip(qt * Q_TILE - WINDOW, 0, seq - span), math.gcd(Q_TILE, WINDOW))
    q4 = _stack_heads(q_ref)
    cols = KV_REP * Q_TILE
    s_w = _nt_dot(kl_ref[pl.ds(start, span), :], q4)
    s_c = _nt_dot(kc_ref[...], q4)
    kpos = start + lax.broadcasted_iota(jnp.int32, (span, 1), 0)
    qpos = qt * Q_TILE + lax.broadcasted_iota(jnp.int32, (1, cols), 1) % Q_TILE
    s_w = jnp.where(jnp.abs(qpos - kpos) <= WINDOW, s_w, NEG_INF)
    sink = sink_ref[0]
    m = jnp.maximum(jnp.maximum(jnp.max(s_w, axis=0, keepdims=True),
                                jnp.max(s_c, axis=0, keepdims=True)), sink)
    p_w = jnp.exp2(s_w - m)
    p_c = jnp.exp2(s_c - m)
    den = (jnp.sum(p_w, axis=0, keepdims=True) + jnp.sum(p_c, axis=0, keepdims=True)
           + jnp.exp2(sink - m))
    ot = (jnp.dot(vtl_ref[:, pl.ds(start, span)], p_w.astype(BF16), preferred_element_type=F32)
          + jnp.dot(vtc_ref[...], p_c.astype(BF16), preferred_element_type=F32))
    _store_heads(ot * (1.0 / den), o_ref)


def _window_attention(q, k, vt, sink_cols, n_batch, seq, ctx_len):
    d = q.shape[1]
    n = n_batch * seq
    lat_q = seq // Q_TILE
    ctx_blk0 = (n_batch * seq) // ctx_len
    gw = KV_REP * HEAD_DIM
    return pl.pallas_call(
        functools.partial(_window_attn_kernel, seq),
        grid=(n_batch, N_KV_HEADS, lat_q),
        in_specs=[pl.BlockSpec((Q_TILE, gw), lambda b, g, t: (b * lat_q + t, g)),
                  pl.BlockSpec((seq, HEAD_DIM), lambda b, g, t: (b, g)),
                  pl.BlockSpec((ctx_len, HEAD_DIM), lambda b, g, t: (ctx_blk0 + b, g)),
                  pl.BlockSpec((HEAD_DIM, seq), lambda b, g, t: (g, b)),
                  pl.BlockSpec((HEAD_DIM, ctx_len), lambda b, g, t: (g, ctx_blk0 + b)),
                  pl.BlockSpec((1, 1, KV_REP * Q_TILE), lambda b, g, t: (g, 0, 0))],
        out_specs=pl.BlockSpec((Q_TILE, gw), lambda b, g, t: (b * lat_q + t, g)),
        out_shape=jax.ShapeDtypeStruct((n, d), BF16),
        compiler_params=_cparams(("arbitrary", "arbitrary", "arbitrary")),
    )(q, k, k, vt, vt, sink_cols)


def _peer_pre_kernel(lat_tiles, o_ref, xl_ref, xc_ref, wo_ref, g1_ref, n2_ref, sh_ref, sc_ref, wq_ref,
                     keys_ref, xn_ref, h2t_ref, s2_ref, e2_ref, thr_ref, e1_ref, st_ref):
    y = jnp.dot(o_ref[...], wo_ref[...], preferred_element_type=F32)
    xn = _select_rows(lat_tiles, xl_ref, xc_ref) + g1_ref[0] * y
    xn_ref[...] = xn
    h2 = _rms(xn) * n2_ref[...]
    h2 = h2 * (1.0 + sc_ref[0]) + sh_ref[0]
    h2t_ref[...] = h2.T.astype(BF16)
    qp = jnp.dot(h2.astype(BF16), wq_ref[...], preferred_element_type=F32)
    half = PEER_QDIM // 2
    for hs in range(2 * PEER_HEADS):
        qs = qp[:, hs * half:(hs + 1) * half].astype(BF16)
        st_ref[hs] = _nt_dot(keys_ref[hs % 2], qs)
    _peer_topk(st_ref, s2_ref, e2_ref, thr_ref, e1_ref)


def _peer_pre(o, rows, mods, layer, w_o, n2, w_q, keys, n_tiles, lat_tiles, tiles_per_seq, n_batch):
    lat_src, ctx_src, ctx_off = rows
    d = o.shape[1]
    t = ROW_TILE
    n = n_tiles * t
    qd = PEER_HEADS * PEER_QDIM

    def mrow(i):
        return layer * MOD_ROWS + jnp.where(i < lat_tiles, i // tiles_per_seq, n_batch)

    cspec = pl.BlockSpec((t // SUB_LANES, PEER_HEADS, N_KEYS, SUB_LANES), lambda i: (i, 0, 0, 0))
    cshape = jax.ShapeDtypeStruct((n // SUB_LANES, PEER_HEADS, N_KEYS, SUB_LANES), F32)
    outs = pl.pallas_call(
        functools.partial(_peer_pre_kernel, lat_tiles),
        grid=(n_tiles,),
        in_specs=[pl.BlockSpec((t, d), lambda i: (i, 0))] + _row_specs(t, d, lat_tiles, ctx_off) + [
                  pl.BlockSpec((d, d), lambda i: (0, 0)),
                  pl.BlockSpec((1, 1, d), lambda i: (mrow(i), 0, 2)),
                  pl.BlockSpec((1, d), lambda i: (0, 0)),
                  pl.BlockSpec((1, 1, d), lambda i: (mrow(i), 0, 3)),
                  pl.BlockSpec((1, 1, d), lambda i: (mrow(i), 0, 4)),
                  pl.BlockSpec((d, qd), lambda i: (0, 0)),
                  pl.BlockSpec((2, N_KEYS, PEER_QDIM // 2), lambda i: (0, 0, 0))],
        out_specs=[pl.BlockSpec((t, d), lambda i: (i, 0)),
                   pl.BlockSpec((d, t), lambda i: (0, i)),
                   cspec, cspec, cspec, cspec],
        out_shape=[jax.ShapeDtypeStruct((n, d), F32),
                   jax.ShapeDtypeStruct((d, n), BF16),
                   cshape, cshape, cshape, cshape],
        scratch_shapes=[pltpu.VMEM((2 * PEER_HEADS, N_KEYS, t), F32)],
        compiler_params=_cparams(("arbitrary",)),
    )(o, lat_src, ctx_src, w_o, mods, n2, mods, mods, w_q, keys)
    return outs[0], outs[1], outs[2:]


def _sorting_network(n):
    def merge(lo, hi, r):
        step = r * 2
        if step < hi - lo:
            yield from merge(lo, hi, step)
            yield from merge(lo + r, hi, step)
            yield from [(i, i + r) for i in range(lo + r, hi - r, step)]
        else:
            yield (lo, lo + r)

    def sort(lo, hi):
        if hi - lo >= 1:
            mid = lo + (hi - lo) // 2
            yield from sort(lo, mid)
            yield from sort(mid + 1, hi)
            yield from merge(lo, hi, 1)

    return list(sort(0, n - 1))


def _exchange(x, i, j):
    x[i], x[j] = jnp.maximum(x[i], x[j]), jnp.minimum(x[i], x[j])


def _top_values(s):
    k = PEER_TOPK
    assert s.shape[0] == k * SUBLANES
    x = [s[i * SUBLANES:(i + 1) * SUBLANES] for i in range(k)]
    for i, j in _sorting_network(k):
        _exchange(x, i, j)
    shift = SUBLANES // 2
    while shift:
        x = [jnp.maximum(x[i], pltpu.roll(x[k - 1 - i], shift, 0)) for i in range(k)]
        d = k // 2
        while d:
            for i in range(k):
                if not i & d:
                    _exchange(x, i, i + d)
            d //= 2
        shift //= 2
    return [xi[0:1] for xi in x]


def _kth_largest(s, k):
    for _ in range(k - 1):
        s = jnp.where(s == jnp.max(s, axis=0, keepdims=True), -jnp.inf, s)
    return jnp.max(s, axis=0, keepdims=True)


def _candidate_sums(v1, v2):
    assert PEER_TOPK == 16
    a1 = jnp.concatenate(v1, axis=0)
    a2 = jnp.concatenate(v2, axis=0)
    row = lax.broadcasted_iota(jnp.int32, (8, a1.shape[1]), 0)
    ninf = -jnp.inf
    return jnp.concatenate([
        v1[0] + a2,
        v1[1] + a2[0:8],
        jnp.where(row < 5, v1[2] + a2[0:8], ninf),
        jnp.where(row < 4, v1[3] + a2[0:8], ninf),
        a1[8:16] + v2[0],
        jnp.where(row >= 4, a1[0:8] + v2[0], ninf),
        jnp.where(row >= 4, a1[0:8] + v2[1], ninf),
        jnp.where(row == 4, a1[0:8] + v2[2], ninf),
    ], axis=0)


def _peer_topk(st_ref, s2_ref, e2_ref, thr_ref, e1_ref):
    def per_head(h, carry):
        s1 = st_ref[2 * h]
        s2 = st_ref[2 * h + 1]
        v1 = _top_values(s1)
        v2 = _top_values(s2)
        cand = _candidate_sums(v1, v2)
        thr = _kth_largest(cand, PEER_TOPK)
        top = v1[0] + v2[0]
        z = jnp.sum(jnp.where(cand >= thr, jnp.exp(cand - top), 0.0), axis=0, keepdims=True)
        guard = thr - jnp.abs(thr) * THRESHOLD_SLACK
        outs = ((s2_ref, s2), (e2_ref, jnp.exp(s2 - v2[0]) * (1.0 / z)),
                (thr_ref, guard - s1), (e1_ref, jnp.exp(s1 - v1[0])))
        for ref, val in outs:
            for lb in range(val.shape[1] // SUB_LANES):
                ref[lb, h] = val[:, lb * SUB_LANES:(lb + 1) * SUB_LANES]
        return carry

    lax.fori_loop(0, PEER_HEADS, per_head, 0)


def _gelu_tanh(x):
    k = -2.0 * math.log2(math.e) * math.sqrt(2.0 / math.pi)
    return x / (1.0 + jnp.exp2(x * (k + (k * 0.044715) * (x * x))))


def _peer_dense_kernel(h2t_ref, s2_ref, e2_ref, thr_ref, e1_ref, u_ref, v_ref, xn_ref, g2_ref,
                       out_ref, acc_ref, act_ref, slab_ref):
    j = pl.program_id(1)
    lanes = acc_ref.shape[1]
    lane_blocks = lanes // SUB_LANES
    row_blocks = N_KEYS // GATE_ROWS

    @pl.when(j == 0)
    def _():
        acc_ref[...] = jnp.zeros_like(acc_ref)

    act_ref[...] = jnp.dot(u_ref[0], h2t_ref[...], preferred_element_type=F32)

    def gate_block(blk, carry):
        grp = blk // (row_blocks * lane_blocks)
        bc = (blk // lane_blocks) % row_blocks
        lc = blk % lane_blocks
        rows = pl.ds(pl.multiple_of(bc * GATE_ROWS, GATE_ROWS), GATE_ROWS)
        cols = pl.ds(pl.multiple_of(lc * SUB_LANES, SUB_LANES), SUB_LANES)
        a0 = j * PEER_CHUNK + grp * GATE_KEYS
        w = [None] * GATE_KEYS
        for h in range(PEER_HEADS):
            s2 = s2_ref[lc, h, rows, :]
            e2 = e2_ref[lc, h, rows, :]
            for al in range(GATE_KEYS):
                thr = thr_ref[lc, h, pl.ds(a0 + al, GATE_ROWS, stride=0), :]
                e1 = e1_ref[lc, h, pl.ds(a0 + al, GATE_ROWS, stride=0), :]
                term = jnp.where(s2 >= thr, e2, 0.0) * e1
                w[al] = term if h == 0 else w[al] + term
        for al in range(GATE_KEYS):
            r0 = (grp * GATE_KEYS + al) * N_KEYS + bc * GATE_ROWS
            arows = pl.ds(pl.multiple_of(r0, GATE_ROWS), GATE_ROWS)
            slab_ref[arows, cols] = (w[al] * _gelu_tanh(act_ref[arows, cols])).astype(BF16)
        return carry

    lax.fori_loop(0, (PEER_CHUNK // GATE_KEYS) * row_blocks * lane_blocks, gate_block, 0, unroll=8)
    acc_ref[...] += lax.dot_general(v_ref[0], slab_ref[...], (((0,), (0,)), ((), ())),
                                    preferred_element_type=F32)

    @pl.when(j == pl.num_programs(1) - 1)
    def _():
        out_ref[...] = xn_ref[...] + g2_ref[0] * acc_ref[...].T


def _peer_dense(h2t, coefs, u, v, xn, mods, layer, n_tiles, lat_tiles, tiles_per_seq, n_batch):
    d = xn.shape[1]
    t = PEER_TILE
    ce = PEER_CHUNK * N_KEYS
    n_exp = u.shape[1]

    def mrow(i):
        return layer * MOD_ROWS + jnp.where(i < lat_tiles, i // tiles_per_seq, n_batch)

    cspec = pl.BlockSpec((t // SUB_LANES, PEER_HEADS, N_KEYS, SUB_LANES), lambda i, j: (i, 0, 0, 0))
    return pl.pallas_call(
        _peer_dense_kernel,
        grid=(n_tiles, n_exp // ce),
        in_specs=[pl.BlockSpec((d, t), lambda i, j: (0, i)),
                  cspec, cspec, cspec, cspec,
                  pl.BlockSpec((1, ce, d), lambda i, j: (layer, j, 0)),
                  pl.BlockSpec((1, ce, d), lambda i, j: (layer, j, 0)),
                  pl.BlockSpec((t, d), lambda i, j: (i, 0)),
                  pl.BlockSpec((1, 1, d), lambda i, j: (mrow(i), 0, 5))],
        out_specs=pl.BlockSpec((t, d), lambda i, j: (i, 0)),
        out_shape=jax.ShapeDtypeStruct((n_tiles * t, d), F32),
        scratch_shapes=[pltpu.VMEM((d, t), F32),
                        pltpu.VMEM((ce, t), F32),
                        pltpu.VMEM((ce, t), BF16)],
        compiler_params=_cparams(("arbitrary", "arbitrary")),
    )(h2t, *coefs, u, v, xn, mods)


def _rope_tables(seq, ctx_len):
    pos = jnp.arange(seq)
    row = (pos // GRID_W).astype(F32)
    col = (pos % GRID_W).astype(F32)
    half = HEAD_DIM // 2
    inv = ROPE_THETA ** (-jnp.arange(0, half, 2, dtype=F32) / half)
    ar = row[:, None] * inv
    ac = col[:, None] * inv
    cos = jnp.concatenate([jnp.cos(ar), jnp.cos(ar), jnp.cos(ac), jnp.cos(ac)], axis=-1)
    sin = jnp.concatenate([-jnp.sin(ar), jnp.sin(ar), -jnp.sin(ac), jnp.sin(ac)], axis=-1)
    cos = jnp.concatenate([cos, jnp.ones((ctx_len, HEAD_DIM), F32)], axis=0)
    sin = jnp.concatenate([sin, jnp.zeros((ctx_len, HEAD_DIM), F32)], axis=0)
    return cos, sin


def kernel(x, c, ctx, c_ctx, ada_w, ada_b, norm1_gain, norm2_gain, w_qkv, q_norm_gain, k_norm_gain,
           w_o, attn_sinks, peer_w_q, peer_sub_keys, peer_u, peer_v):
    n_batch, seq, d = x.shape
    ctx_len = ctx.shape[1]
    depth = ada_w.shape[0]
    n_lat = n_batch * seq
    n_ctx = n_batch * ctx_len
    assert depth == 2 and d == D_MODEL and n_batch < MOD_ROWS
    assert seq % PEER_TILE == 0 and n_ctx % PEER_TILE == 0 and ctx_len == ROW_TILE
    assert seq >= Q_TILE + 2 * WINDOW and n_lat % ctx_len == 0
    assert seq % Q_TILE == 0 and ctx_len % Q_TILE == 0 and (N_MOD * d) % MOD_COLS == 0

    cc = jnp.concatenate([c, c_ctx[None, :], jnp.zeros((MOD_ROWS - n_batch - 1, d), F32)], axis=0)
    mods = _modulation(cc, ada_w, ada_b).reshape(depth * MOD_ROWS, 1, N_MOD * d)
    cos_t, sin_t = _rope_tables(seq, ctx_len)
    u_tab = peer_u.astype(BF16)
    v_tab = peer_v.astype(BF16)
    rows = (x.reshape(n_lat, d), ctx.reshape(n_ctx, d), 0)
    scale = HEAD_DIM ** -0.5 * math.log2(math.e)

    for layer in range(depth):
        last = layer == depth - 1
        q, k, vt = _qkv(rows, n_lat + n_ctx, mods, layer, norm1_gain[layer][None, :],
                        w_qkv[layer].astype(BF16),
                        (q_norm_gain[layer] * scale)[None, :], k_norm_gain[layer][None, :],
                        cos_t, sin_t, n_lat // ROW_TILE, seq // ROW_TILE, n_batch)
        if layer % 2 == 0:
            o = _global_attention(q, k, vt, n_batch, seq, ctx_len)
        else:
            sink = (attn_sinks[layer // 2] * math.log2(math.e)).reshape(N_KV_HEADS, 1, KV_REP, 1)
            sink_cols = jnp.broadcast_to(sink, (N_KV_HEADS, 1, KV_REP, Q_TILE))
            o = _window_attention(q, k, vt, sink_cols.reshape(N_KV_HEADS, 1, KV_REP * Q_TILE),
                                  n_batch, seq, ctx_len)
        n_rows = n_lat if last else n_lat + n_ctx
        xn, h2t, coefs = _peer_pre(o, rows, mods, layer, w_o[layer].astype(BF16),
                                norm2_gain[layer][None, :], peer_w_q[layer].astype(BF16),
                                peer_sub_keys[layer].astype(BF16), n_rows // ROW_TILE,
                                n_lat // ROW_TILE, seq // ROW_TILE, n_batch)
        xs = _peer_dense(h2t, coefs, u_tab, v_tab,
                         xn, mods, layer, n_rows // PEER_TILE, n_lat // PEER_TILE,
                         seq // PEER_TILE, n_batch)
        rows = (xs, xs, n_lat // ROW_TILE)
    return xs[:n_lat].reshape(n_batch, seq, d)
```
